```python
import jax, jax.numpy as jnp
from jax import lax
import numpy as np

D_MODEL = 4096
BATCH = 4
SEQ = 4096
DEPTH = 1

NSA_HEADS = D_MODEL // 256
NSA_HEAD_DIM = 128
NSA_KV_HEADS = NSA_HEADS // 4
NSA_GROUP = NSA_HEADS // NSA_KV_HEADS
CMP_BLOCK = 32
CMP_STRIDE = 16
SEL_BLOCK = 64
SEL_TOPN = 16
WINDOW = 512
NSA_Q_BLOCK = 64
NSA_BRANCHES = 3
BIG = 1e9
GLA_HEADS = D_MODEL // 256
GLA_DK = 64
GLA_DV = 128
GLA_GATE_RANK = 16
GLA_TAU = 16.0
GLA_CHUNK = 64
N_GROUPS = 4
EXPERTS_PER_GROUP = 8
N_EXPERTS = N_GROUPS * EXPERTS_PER_GROUP
EXPERT_TOPK = 2
D_EXPERT = D_MODEL // 4
NORM_EPS = 1e-6

NSA_Q = NSA_HEADS * NSA_HEAD_DIM
NSA_KV = NSA_KV_HEADS * NSA_HEAD_DIM
GLA_QK = GLA_HEADS * GLA_DK
GLA_V = GLA_HEADS * GLA_DV
SPLIT_SIZES = (NSA_Q, NSA_KV, NSA_KV, NSA_KV, NSA_KV, NSA_KV, NSA_KV, NSA_BRANCHES * NSA_HEADS,
               GLA_QK, GLA_QK, GLA_V, GLA_V, GLA_GATE_RANK, D_MODEL, D_MODEL)
IN_COLS = sum(SPLIT_SIZES)

kernel_name = "hybrid_nsa_gla_hiermoe_block"


def rms_norm(x, g):
    x32 = x.astype(jnp.float32)
    y = x32 * lax.rsqrt(jnp.mean(x32 * x32, axis=-1, keepdims=True) + NORM_EPS)
    return (y * g.astype(jnp.float32)).astype(x.dtype)


def masked_softmax(s, mask):
    s = jnp.where(mask, s, -jnp.inf)
    m = jnp.max(s, axis=-1, keepdims=True)
    m = jnp.where(jnp.isfinite(m), m, 0.0)
    e = jnp.exp(s - m)
    return e / jnp.maximum(jnp.sum(e, axis=-1, keepdims=True), 1e-30)


def alibi_slopes(n_heads):
    return jnp.exp2(-8.0 * jnp.arange(1, n_heads + 1, dtype=jnp.float32) / n_heads)


def compress_blocks(seq, pos, w1, b1, w2):
    B, G, S, Dh = seq.shape
    n_cmp = (S - CMP_BLOCK) // CMP_STRIDE + 1
    idx = jnp.arange(n_cmp)[:, None] * CMP_STRIDE + jnp.arange(CMP_BLOCK)[None, :]
    blocks = (seq[:, :, idx] + pos).reshape(B, G, n_cmp, CMP_BLOCK * Dh)
    return jax.nn.gelu(blocks @ w1 + b1) @ w2


def nsa_attention(xq, xkc, xvc, xks, xvs, xkw, xvw, xg,
                  k_pos, k_w1, k_b1, k_w2, v_pos, v_w1, v_b1, v_w2):
    B, S, _ = xq.shape
    G, R, Dh, QB = NSA_KV_HEADS, NSA_GROUP, NSA_HEAD_DIM, NSA_Q_BLOCK
    q = xq.reshape(B, S, G, R, Dh).transpose(0, 2, 3, 1, 4)
    kv = lambda a: a.reshape(B, S, G, Dh).transpose(0, 2, 1, 3)
    kc = compress_blocks(kv(xkc), k_pos, k_w1, k_b1, k_w2)
    vc = compress_blocks(kv(xvc), v_pos, v_w1, v_b1, v_w2)
    n_cmp = kc.shape[2]
    cmp_end = jnp.arange(n_cmp) * CMP_STRIDE + CMP_BLOCK - 1
    n_sel = S // SEL_BLOCK
    n_top = min(SEL_TOPN, n_sel)
    ks_blocks = kv(xks).reshape(B, G, n_sel, SEL_BLOCK, Dh)
    vs_blocks = kv(xvs).reshape(B, G, n_sel, SEL_BLOCK, Dh)
    pad_w = ((0, 0), (0, 0), (WINDOW, 0), (0, 0))
    kw_pad = jnp.pad(kv(xkw), pad_w)
    vw_pad = jnp.pad(kv(xvw), pad_w)
    gates = jax.nn.sigmoid(xg.astype(jnp.float32)).reshape(B, S, G, R, NSA_BRANCHES).transpose(0, 2, 3, 1, 4)
    slopes = alibi_slopes(NSA_HEADS).reshape(1, G, R, 1, 1)
    scale = Dh ** -0.5
    bi = jnp.arange(B)[:, None, None, None]
    gi = jnp.arange(G)[None, :, None, None]
    r_sel, r_cmp = SEL_BLOCK // CMP_STRIDE, CMP_BLOCK // CMP_STRIDE
    sel_start = jnp.arange(n_sel) * SEL_BLOCK
    blk_ids = jnp.arange(n_sel)[None, :]

    def block(q0):
        t = q0 + jnp.arange(QB)
        qb = lax.dynamic_slice_in_dim(q, q0, QB, axis=3)
        gb = lax.dynamic_slice_in_dim(gates, q0, QB, axis=3)
        s = jnp.einsum('bgrqd,bgnd->bgrqn', qb, kc).astype(jnp.float32) * scale
        dist = (t[:, None] - cmp_end[None, :]).astype(jnp.float32)
        p_cmp = masked_softmax(s - slopes * dist, cmp_end[None, :] <= t[:, None])
        o_cmp = jnp.einsum('bgrqn,bgnd->bgrqd', p_cmp.astype(vc.dtype), vc)
        imp = jnp.pad(jnp.sum(p_cmp, axis=2),
                      ((0, 0), (0, 0), (0, 0), (r_cmp - 1, r_sel * n_sel - n_cmp)))
        imp_sel = sum(imp[..., r_cmp - 1 + m - n: r_cmp - 1 + m - n + r_sel * n_sel: r_sel]
                      for m in range(r_sel) for n in range(r_cmp))
        cur = (t // SEL_BLOCK)[:, None]
        forced = (blk_ids == 0) | (blk_ids == cur) | (blk_ids == cur - 1)
        valid = sel_start[None, :] <= t[:, None]
        score = jnp.where(valid, jnp.where(forced, BIG, imp_sel), -BIG)
        _, idx = lax.top_k(score, n_top)
        k_sel = ks_blocks[bi, gi, idx].reshape(B, G, QB, n_top * SEL_BLOCK, Dh)
        v_sel = vs_blocks[bi, gi, idx].reshape(B, G, QB, n_top * SEL_BLOCK, Dh)
        pos = (idx[..., None] * SEL_BLOCK + jnp.arange(SEL_BLOCK)).reshape(B, G, QB, n_top * SEL_BLOCK)
        diff = (t[:, None] - pos)[:, :, None]
        s = jnp.einsum('bgrqd,bgqld->bgrql', qb, k_sel).astype(jnp.float32) * scale
        p_sel = masked_softmax(s - slopes * diff.astype(jnp.float32), diff >= 0)
        o_sel = jnp.einsum('bgrql,bgqld->bgrqd', p_sel.astype(v_sel.dtype), v_sel)
        kwb = lax.dynamic_slice_in_dim(kw_pad, q0, WINDOW + QB, axis=2)
        vwb = lax.dynamic_slice_in_dim(vw_pad, q0, WINDOW + QB, axis=2)
        s_pos = q0 - WINDOW + jnp.arange(WINDOW + QB)
        dw = t[:, None] - s_pos[None, :]
        mask_w = (dw >= 0) & (dw < WINDOW) & (s_pos[None, :] >= 0)
        s = jnp.einsum('bgrqd,bgkd->bgrqk', qb, kwb).astype(jnp.float32) * scale
        p_win = masked_softmax(s - slopes * dw.astype(jnp.float32), mask_w)
        o_win = jnp.einsum('bgrqk,bgkd->bgrqd', p_win.astype(vwb.dtype), vwb)
        o = gb[..., 0:1] * o_cmp + gb[..., 1:2] * o_sel + gb[..., 2:3] * o_win
        return o.astype(xq.dtype)

    out = lax.map(block, jnp.arange(S // QB) * QB)
    return out.transpose(1, 0, 4, 2, 3, 5).reshape(B, S, NSA_Q)


def gla_attention(xq, xk, xv, xr, xa, w_alpha, b_alpha, norm_g):
    B, S, _ = xq.shape
    H, dk, dv, C = GLA_HEADS, GLA_DK, GLA_DV, GLA_CHUNK
    n_chunk = S // C
    chunks = lambda a, d: a.reshape(B, n_chunk, C, H, d).transpose(1, 0, 3, 2, 4)
    q = chunks(xq.astype(jnp.float32), dk) * dk ** -0.5
    k = chunks(xk.astype(jnp.float32), dk)
    v = chunks(xv.astype(jnp.float32), dv)
    log_a = jax.nn.log_sigmoid((xa @ w_alpha + b_alpha).astype(jnp.float32)) / GLA_TAU
    la = chunks(log_a, dk)
    causal = jnp.tril(jnp.ones((C, C), dtype=bool))[None, None, :, :, None]

    def step(state, inp):
        qc, kc, vc, lac = inp
        b = jnp.cumsum(lac, axis=2)
        o_inter = jnp.einsum('bhtk,bhkv->bhtv', qc * jnp.exp(b), state)
        rel = jnp.where(causal, b[:, :, :, None, :] - b[:, :, None, :, :], -jnp.inf)
        attn = jnp.einsum('bhtk,bhsk,bhtsk->bhts', qc, kc, jnp.exp(rel))
        o = o_inter + jnp.einsum('bhts,bhsv->bhtv', attn, vc)
        b_last = b[:, :, -1:, :]
        state = (jnp.exp(b_last[:, :, 0, :])[..., None] * state
                 + jnp.einsum('bhsk,bhsv->bhkv', kc * jnp.exp(b_last - b), vc))
        return state, o

    state0 = jnp.zeros((B, H, dk, dv), jnp.float32)
    _, o = lax.scan(step, state0, (q, k, v, la))
    o = o.transpose(1, 0, 3, 2, 4).reshape(B, S, H, dv)
    mu = jnp.mean(o, axis=-1, keepdims=True)
    var = jnp.mean(jnp.square(o - mu), axis=-1, keepdims=True)
    o = ((o - mu) * lax.rsqrt(var + NORM_EPS)).reshape(B, S, GLA_V) * norm_g.astype(jnp.float32)
    o = o * jax.nn.silu(xr.astype(jnp.float32))
    return o.astype(xq.dtype)


def hier_moe(u, w_rg, b_rg, w_re, b_re, w_gate, w_up, w_down):
    B, S, D = u.shape
    T = B * S
    x = u.reshape(T, D)
    g_logits = (x @ w_rg + b_rg).astype(jnp.float32)
    g_prob = jax.nn.softmax(g_logits, axis=-1)
    g_star = jnp.argmax(g_logits, axis=-1)
    p_group = jnp.take_along_axis(g_prob, g_star[:, None], axis=1)
    e_logits = (x @ w_re + b_re).astype(jnp.float32).reshape(T, N_GROUPS, EXPERTS_PER_GROUP)
    e_logits = jnp.take_along_axis(e_logits, g_star[:, None, None], axis=1)[:, 0]
    top_p, top_i = lax.top_k(jax.nn.softmax(e_logits, axis=-1), EXPERT_TOPK)
    w = p_group * top_p / jnp.sum(top_p, axis=-1, keepdims=True)
    gid = g_star[:, None] * EXPERTS_PER_GROUP + top_i
    combine = jnp.einsum('tk,tke->te', w, jax.nn.one_hot(gid, N_EXPERTS, dtype=jnp.float32))
    y = jnp.zeros((T, D), jnp.float32)
    for e in range(N_EXPERTS):
        h = jax.nn.silu(x @ w_gate[e]) * (x @ w_up[e])
        y = y + combine[:, e:e + 1] * (h @ w_down[e]).astype(jnp.float32)
    return y.astype(u.dtype).reshape(B, S, D)


def setup_inputs(seed: int = 0) -> dict:
    key = jax.random.key(seed)
    keys = iter(jax.random.split(key, 32))
    L, D, Dh = DEPTH, D_MODEL, NSA_HEAD_DIM
    nrm = lambda shape, scale: jax.random.normal(next(keys), shape, jnp.float32) * scale
    gain = lambda shape: 1.0 + nrm(shape, 0.02)
    return {
        "x": nrm((BATCH, SEQ, D), 1.0),
        "norm_mix_g": gain((L, D)),
        "w_in": nrm((L, D, IN_COLS), D ** -0.5),
        "cmp_k_pos": nrm((L, CMP_BLOCK, Dh), 0.1),
        "cmp_k_w1": nrm((L, CMP_BLOCK * Dh, Dh), (CMP_BLOCK * Dh) ** -0.5),
        "cmp_k_b1": nrm((L, Dh), 0.01),
        "cmp_k_w2": nrm((L, Dh, Dh), Dh ** -0.5),
        "cmp_v_pos": nrm((L, CMP_BLOCK, Dh), 0.1),
        "cmp_v_w1": nrm((L, CMP_BLOCK * Dh, Dh), (CMP_BLOCK * Dh) ** -0.5),
        "cmp_v_b1": nrm((L, Dh), 0.01),
        "cmp_v_w2": nrm((L, Dh, Dh), Dh ** -0.5),
        "gla_w_alpha": nrm((L, GLA_GATE_RANK, GLA_QK), GLA_GATE_RANK ** -0.5),
        "gla_b_alpha": nrm((L, GLA_QK), 0.1),
        "gla_norm_g": gain((L, GLA_V)),
        "w_branch_nsa": nrm((L, NSA_Q, D), NSA_Q ** -0.5),
        "w_branch_gla": nrm((L, GLA_V, D), GLA_V ** -0.5),
        "w_out": nrm((L, D, D), D ** -0.5),
        "norm_ffn_g": gain((L, D)),
        "w_router_group": nrm((L, D, N_GROUPS), D ** -0.5),
        "b_router_group": nrm((L, N_GROUPS), 0.01),
        "w_router_expert": nrm((L, D, N_EXPERTS), D ** -0.5),
        "b_router_expert": nrm((L, N_EXPERTS), 0.01),
        "w_exp_gate": nrm((L, N_EXPERTS, D, D_EXPERT), D ** -0.5),
        "w_exp_up": nrm((L, N_EXPERTS, D, D_EXPERT), D ** -0.5),
        "w_exp_down": nrm((L, N_EXPERTS, D_EXPERT, D), D_EXPERT ** -0.5),
        "norm_final_g": gain((D,)),
    }


def reference(x, norm_mix_g, w_in, cmp_k_pos, cmp_k_w1, cmp_k_b1, cmp_k_w2,
              cmp_v_pos, cmp_v_w1, cmp_v_b1, cmp_v_w2, gla_w_alpha, gla_b_alpha, gla_norm_g,
              w_branch_nsa, w_branch_gla, w_out, norm_ffn_g, w_router_group, b_router_group,
              w_router_expert, b_router_expert, w_exp_gate, w_exp_up, w_exp_down, norm_final_g):
    split_points = np.cumsum(SPLIT_SIZES)[:-1].tolist()
    h = x
    for l in range(DEPTH):
        xn = rms_norm(h, norm_mix_g[l])
        proj = xn @ w_in[l]
        (q, kc, vc, ks, vs, kw, vw, ng, gq, gk, gv, gr, ga, ma, mb) = jnp.split(proj, split_points, axis=-1)
        o_nsa = nsa_attention(q, kc, vc, ks, vs, kw, vw, ng,
                              cmp_k_pos[l], cmp_k_w1[l], cmp_k_b1[l], cmp_k_w2[l],
                              cmp_v_pos[l], cmp_v_w1[l], cmp_v_b1[l], cmp_v_w2[l])
        o_gla = gla_attention(gq, gk, gv, gr, ga, gla_w_alpha[l], gla_b_alpha[l], gla_norm_g[l])
        mixed = (jax.nn.sigmoid(ma) * (o_nsa @ w_branch_nsa[l])
                 + jax.nn.sigmoid(mb) * (o_gla @ w_branch_gla[l]))
        h = h + mixed @ w_out[l]
        h = h + hier_moe(rms_norm(h, norm_ffn_g[l]), w_router_group[l], b_router_group[l],
                         w_router_expert[l], b_router_expert[l],
                         w_exp_gate[l], w_exp_up[l], w_exp_down[l])
    return rms_norm(h, norm_final_g)
```

```python
import functools

import jax
import jax.numpy as jnp
from jax import lax
from jax.experimental import pallas as pl
from jax.experimental.pallas import tpu as pltpu

F32 = jnp.float32
BF16 = jnp.bfloat16
I32 = jnp.int32

NSA_HEAD_DIM = 128
NSA_KV_HEADS = 4
NSA_GROUP = 4
NSA_HEADS = NSA_KV_HEADS * NSA_GROUP
CMP_BLOCK = 32
CMP_STRIDE = 16
SEL_BLOCK = 64
SEL_TOPN = 16
WINDOW = 512
BIG = 1e9
GLA_HEADS = 16
GLA_DK = 64
GLA_DV = 128
GLA_GATE_RANK = 16
GLA_TAU = 16.0
N_GROUPS = 4
EXPERTS_PER_GROUP = 8
N_EXPERTS = N_GROUPS * EXPERTS_PER_GROUP
NORM_EPS = 1e-6

LANES = 128
VMEM_LIMIT = 56 * 1024 * 1024
NEG = -1e30

NORM_ROWS = 512
PROJ_TM, PROJ_TN = 1024, 512
NSA_TQ, NSA_TK = 128, 256
GLA_TS, GLA_C, GLA_SUB = 512, 64, 16
MIX_TM, MIX_TN = 512, 1024
OUT_TM, OUT_TN = 512, 1024
ROUTE_TM = 256
MOE_TM = 512
MOE_CJ = 256
MOE_CN = 1024
GATHER_ROWS = 256
COMBINE_ROWS = 128


def _cparams(*sem):
    return pltpu.CompilerParams(dimension_semantics=sem, vmem_limit_bytes=VMEM_LIMIT)


def _dot(a, b, **kw):
    return jnp.dot(a, b, preferred_element_type=F32, **kw)


def _dot_nt(a, b, **kw):
    return lax.dot_general(a, b, (((1,), (1,)), ((), ())), preferred_element_type=F32, **kw)


def _dot_tn(a, b, **kw):
    return lax.dot_general(a, b, (((0,), (0,)), ((), ())), preferred_element_type=F32, **kw)


def _sigmoid(x):
    return 1.0 / (1.0 + jnp.exp(-x))


def _masked_softmax(s, mask):
    sm = jnp.where(mask, s, NEG)
    m = jnp.max(sm, axis=-1, keepdims=True)
    e = jnp.where(mask, jnp.exp(sm - m), 0.0)
    return e * (1.0 / jnp.maximum(jnp.sum(e, axis=-1, keepdims=True), 1e-30))


def _rmsnorm_kernel(x_ref, g_ref, o_ref):
    x = x_ref[...]
    ms = jnp.mean(x * x, axis=-1, keepdims=True)
    o_ref[...] = (x * lax.rsqrt(ms + NORM_EPS) * g_ref[...]).astype(o_ref.dtype)


def rmsnorm(x, g, out_dtype=BF16, rows=NORM_ROWS):
    T, D = x.shape
    rows = min(rows, T)
    return pl.pallas_call(
        _rmsnorm_kernel,
        out_shape=jax.ShapeDtypeStruct((T, D), out_dtype),
        grid=(T // rows,),
        in_specs=[pl.BlockSpec((rows, D), lambda i: (i, 0)),
                  pl.BlockSpec((1, D), lambda i: (0, 0))],
        out_specs=pl.BlockSpec((rows, D), lambda i: (i, 0)),
        compiler_params=_cparams("parallel"),
        name="rmsnorm",
    )(x, g.reshape(1, D))


def _mm_kernel(a_ref, b_ref, o_ref):
    o_ref[...] = _dot(a_ref[...], b_ref[...]).astype(o_ref.dtype)


def matmul(a, b, out_dtype, tm, tn, name):
    M, K = a.shape
    N = b.shape[1]
    tm, tn = min(tm, M), min(tn, N)
    return pl.pallas_call(
        _mm_kernel,
        out_shape=jax.ShapeDtypeStruct((M, N), out_dtype),
        grid=(M // tm, N // tn),
        in_specs=[pl.BlockSpec((tm, K), lambda i, j: (i, 0)),
                  pl.BlockSpec((K, tn), lambda i, j: (0, j))],
        out_specs=pl.BlockSpec((tm, tn), lambda i, j: (i, j)),
        compiler_params=_cparams("parallel", "arbitrary"),
        name=name,
    )(a, b)


def _compress_kernel(seq_ref, pos_ref, w1_ref, b1_ref, w2_ref, o_ref):
    x = seq_ref[0, 0, 0].astype(F32)
    nc = x.shape[0]
    half = x.shape[1]
    pos = pos_ref[0]
    w1 = w1_ref[0]
    u0 = _dot((x + pos[0:1, :]).astype(BF16), w1[:half, :])
    u1 = _dot((x + pos[1:2, :]).astype(BF16), w1[half:, :])
    pre = u0 + pltpu.roll(u1, nc - 1, 0) + b1_ref[0]
    h = 0.5 * pre * (1.0 + jnp.tanh(0.7978845608028654 * (pre + 0.044715 * pre * pre * pre)))
    o_ref[0, 0, 0] = _dot(h.astype(BF16), w2_ref[0]).astype(o_ref.dtype)


def compress(seq2, pos2, w1, b1, w2):
    _, B, G, NC, HW = seq2.shape
    Dh = w2.shape[-1]
    return pl.pallas_call(
        _compress_kernel,
        out_shape=jax.ShapeDtypeStruct((2, B, G, NC, Dh), BF16),
        grid=(2, B, G),
        in_specs=[pl.BlockSpec((1, 1, 1, NC, HW), lambda a, b, g: (a, b, g, 0, 0)),
                  pl.BlockSpec((1, 2, HW), lambda a, b, g: (a, 0, 0)),
                  pl.BlockSpec((1, 2 * HW, Dh), lambda a, b, g: (a, 0, 0)),
                  pl.BlockSpec((1, 1, Dh), lambda a, b, g: (a, 0, 0)),
                  pl.BlockSpec((1, Dh, Dh), lambda a, b, g: (a, 0, 0))],
        out_specs=pl.BlockSpec((1, 1, 1, NC, Dh), lambda a, b, g: (a, b, g, 0, 0)),
        compiler_params=_cparams("parallel", "parallel", "parallel"),
        name="nsa_compress",
    )(seq2, pos2, w1, b1, w2)


def _nsa_kernel(slopes_ref, q_ref, kc_ref, vc_ref, ks_ref, vs_ref, kw_ref, vw_ref, g_ref, o_ref,
                *, TQ, TK, S, NC, NSEL, NTOP):
    R, Dh = NSA_GROUP, NSA_HEAD_DIM
    g = pl.program_id(1)
    i = pl.program_id(2)
    q0 = i * TQ
    scale = Dh ** -0.5
    rows = R * TQ

    q4 = jnp.concatenate([q_ref[0, :, r * Dh:(r + 1) * Dh] for r in range(R)], axis=0)
    slope4 = jnp.concatenate([jnp.full((TQ, 1), slopes_ref[g * R + r], F32) for r in range(R)], axis=0)
    t1 = q0 + lax.broadcasted_iota(I32, (TQ, 1), 0)
    t4 = jnp.concatenate([t1] * R, axis=0)

    cmp_end = lax.broadcasted_iota(I32, (1, NC), 1) * CMP_STRIDE + (CMP_BLOCK - 1)
    s = _dot_nt(q4, kc_ref[0, 0, 0]) * scale
    s = s - slope4 * (t4 - cmp_end).astype(F32)
    p_cmp = _masked_softmax(s, cmp_end <= t4)
    o_cmp = _dot(p_cmp.astype(BF16), vc_ref[0, 0, 0])

    imp = p_cmp[0:TQ]
    for r in range(1, R):
        imp = imp + p_cmp[r * TQ:(r + 1) * TQ]
    d = lax.broadcasted_iota(I32, (NC, NSEL), 0) - 4 * lax.broadcasted_iota(I32, (NC, NSEL), 1)
    mmat = jnp.where((d == -1) | (d == 3), 1.0, jnp.where((d >= 0) & (d <= 2), 2.0, 0.0)).astype(F32)
    imp_sel = _dot(imp, mmat, precision=lax.Precision.HIGHEST)

    cur = t1 // SEL_BLOCK
    blk = lax.broadcasted_iota(I32, (1, NSEL), 1)
    forced = (blk == 0) | (blk == cur) | (blk == cur - 1)
    score = jnp.where(blk <= cur, jnp.where(forced, BIG, imp_sel), -BIG)
    cnt = jnp.zeros((TQ, NSEL), F32)
    for ii in range(NSEL):
        col = score[:, ii:ii + 1]
        beats = (col > score) | ((col == score) & (blk > ii))
        cnt = cnt + jnp.where(beats, 1.0, 0.0)
    sel = jnp.where(cnt < NTOP, 1.0, 0.0).astype(BF16)
    sel4 = jnp.concatenate([sel] * R, axis=0)

    blk_col = lax.broadcasted_iota(I32, (NSEL, TK), 0)
    key_in_tile = lax.broadcasted_iota(I32, (NSEL, TK), 1)
    pos_in_tile = lax.broadcasted_iota(I32, (1, TK), 1)

    def sel_body(kt, carry):
        m, l, acc = carry
        k0 = pl.multiple_of(kt * TK, TK)
        k = ks_ref[0, pl.ds(k0, TK), :]
        v = vs_ref[0, pl.ds(k0, TK), :]
        expand = jnp.where((key_in_tile + k0) // SEL_BLOCK == blk_col, 1.0, 0.0).astype(BF16)
        selk = _dot(sel4, expand)
        diff = t4 - (k0 + pos_in_tile)
        mask = (selk > 0.5) & (diff >= 0)
        sc = _dot_nt(q4, k) * scale - slope4 * diff.astype(F32)
        sc = jnp.where(mask, sc, NEG)
        m_new = jnp.maximum(m, jnp.max(sc, axis=-1, keepdims=True))
        alpha = jnp.exp(m - m_new)
        p = jnp.where(mask, jnp.exp(sc - m_new), 0.0)
        l = alpha * l + jnp.sum(p, axis=-1, keepdims=True)
        acc = alpha * acc + _dot(p.astype(BF16), v)
        return m_new, l, acc

    n_kt = (q0 + TQ + TK - 1) // TK
    init = (jnp.full((rows, 1), NEG, F32), jnp.zeros((rows, 1), F32), jnp.zeros((rows, Dh), F32))
    _, l_sel, acc_sel = lax.fori_loop(0, n_kt, sel_body, init)
    o_sel = acc_sel * (1.0 / jnp.maximum(l_sel, 1e-30))

    WK = WINDOW + TQ
    ws = pl.multiple_of(jnp.maximum(q0 - WINDOW, 0), TQ)
    kw = kw_ref[0, pl.ds(ws, WK), :]
    vw = vw_ref[0, pl.ds(ws, WK), :]
    dw = t4 - (ws + lax.broadcasted_iota(I32, (1, WK), 1))
    sw = _dot_nt(q4, kw) * scale - slope4 * dw.astype(F32)
    p_win = _masked_softmax(sw, (dw >= 0) & (dw < WINDOW))
    o_win = _dot(p_win.astype(BF16), vw)

    gt = _sigmoid(g_ref[0, 0])
    for r in range(R):
        rs = slice(r * TQ, (r + 1) * TQ)
        o = (gt[:, 3 * r:3 * r + 1] * o_cmp[rs] + gt[:, 3 * r + 1:3 * r + 2] * o_sel[rs]
             + gt[:, 3 * r + 2:3 * r + 3] * o_win[rs])
        o_ref[0, :, r * Dh:(r + 1) * Dh] = o.astype(o_ref.dtype)


def nsa_attention(proj, kvc, gates, slopes, *, col_q, col_ks, col_vs, col_kw, col_vw):
    B, S, _ = proj.shape
    G, R, Dh = NSA_KV_HEADS, NSA_GROUP, NSA_HEAD_DIM
    NC = kvc.shape[3]
    NSEL = S // SEL_BLOCK
    TQ, TK = min(NSA_TQ, S), min(NSA_TK, S)
    assert S % TQ == 0 and S % TK == 0 and S >= WINDOW + TQ and NC == S // CMP_STRIDE
    kern = functools.partial(_nsa_kernel, TQ=TQ, TK=TK, S=S, NC=NC, NSEL=NSEL, NTOP=min(SEL_TOPN, NSEL))
    kv_spec = lambda col: pl.BlockSpec((1, S, Dh), lambda b, g, i: (b, 0, col + g))
    return pl.pallas_call(
        kern,
        out_shape=jax.ShapeDtypeStruct((B, S, G * R * Dh), BF16),
        grid=(B, G, S // TQ),
        in_specs=[pl.BlockSpec(memory_space=pltpu.SMEM),
                  pl.BlockSpec((1, TQ, R * Dh), lambda b, g, i: (b, i, col_q // R + g)),
                  pl.BlockSpec((1, 1, 1, NC, Dh), lambda b, g, i: (0, b, g, 0, 0)),
                  pl.BlockSpec((1, 1, 1, NC, Dh), lambda b, g, i: (1, b, g, 0, 0)),
                  kv_spec(col_ks), kv_spec(col_vs), kv_spec(col_kw), kv_spec(col_vw),
                  pl.BlockSpec((1, 1, TQ, 3 * R), lambda b, g, i: (b, g, i, 0))],
        out_specs=pl.BlockSpec((1, TQ, R * Dh), lambda b, g, i: (b, i, g)),
        compiler_params=_cparams("parallel", "parallel", "arbitrary"),
        name="nsa_attention",
    )(slopes, proj, kvc, kvc, proj, proj, proj, proj, gates)


def _gla_kernel(q_ref, k_ref, v_ref, r_ref, xa_ref, wa_ref, ba_ref, ng_ref, o_ref, st_ref, *, TS, C):
    DK2, DV2 = 2 * GLA_DK, 2 * GLA_DV
    SUB = GLA_SUB
    NSUB = C // SUB
    HI = lax.Precision.HIGHEST

    @pl.when(pl.program_id(2) == 0)
    def _():
        st_ref[...] = jnp.zeros_like(st_ref)

    lane = lax.broadcasted_iota(I32, (1, DK2), 1)
    head_a = lane < GLA_DK
    row = lax.broadcasted_iota(I32, (C, 1), 0)
    col = lax.broadcasted_iota(I32, (1, C), 1)
    tri = jnp.where(col <= row, 1.0, 0.0).astype(F32)
    mask1 = (row // 32 == 1) & (col // 32 == 0)
    mask2 = (row // 32 == col // 32) & ((row // 16) % 2 == 1) & ((col // 16) % 2 == 0)
    kv_same = (lax.broadcasted_iota(I32, (DK2, DV2), 0) < GLA_DK) == (lax.broadcasted_iota(I32, (DK2, DV2), 1) < GLA_DV)
    hsel = jnp.where(kv_same, 1.0, 0.0).astype(BF16)
    vk_same = (lax.broadcasted_iota(I32, (DV2, DK2), 0) < GLA_DV) == (lax.broadcasted_iota(I32, (DV2, DK2), 1) < GLA_DK)
    t_in = row % SUB

    def chunk(c, carry):
        r0 = pl.multiple_of(c * C, C)
        q = q_ref[0, pl.ds(r0, C), :].astype(F32) * (GLA_DK ** -0.5)
        k = k_ref[0, pl.ds(r0, C), :].astype(F32)
        vb = v_ref[0, pl.ds(r0, C), :]
        v = vb.astype(F32)
        z = _dot(xa_ref[0, pl.ds(r0, C), :], wa_ref[...], precision=HI) + ba_ref[...]
        la = (jnp.minimum(z, 0.0) - jnp.log(1.0 + jnp.exp(-jnp.abs(z)))) * (1.0 / GLA_TAU)
        b = _dot(tri, la, precision=HI)

        r1 = b[31:32, :]
        r2 = jnp.where(row < 32, b[15:16, :], b[47:48, :])
        q1 = q * jnp.exp(jnp.minimum(b - r1, 0.0))
        k1 = (k * jnp.exp(jnp.minimum(r1 - b, 0.0))).astype(BF16)
        q2 = q * jnp.exp(jnp.minimum(b - r2, 0.0))
        k2 = (k * jnp.exp(jnp.minimum(r2 - b, 0.0))).astype(BF16)

        def scores(hm):
            a1 = _dot_nt(jnp.where(hm, q1, 0.0).astype(BF16), k1)
            a2 = _dot_nt(jnp.where(hm, q2, 0.0).astype(BF16), k2)
            return (jnp.where(mask1, a1, 0.0) + jnp.where(mask2, a2, 0.0)).astype(BF16)

        o = jnp.concatenate([_dot(scores(head_a), vb[:, :GLA_DV]),
                             _dot(scores(jnp.logical_not(head_a)), vb[:, GLA_DV:])], axis=1)

        b3 = b.reshape(NSUB, SUB, DK2)
        k3 = k.reshape(NSUB, SUB, DK2)
        v3 = v.reshape(NSUB, SUB, DV2)
        prods = []
        for sp in range(SUB):
            bs = jnp.broadcast_to(b3[:, sp:sp + 1, :], (NSUB, SUB, DK2)).reshape(C, DK2)
            ks = jnp.broadcast_to(k3[:, sp:sp + 1, :], (NSUB, SUB, DK2)).reshape(C, DK2)
            prods.append((q * ks * jnp.exp(jnp.minimum(b - bs, 0.0))).astype(BF16))
        ac = _dot(jnp.concatenate(prods, axis=0), hsel)
        for sp in range(SUB):
            vs = jnp.broadcast_to(v3[:, sp:sp + 1, :], (NSUB, SUB, DV2)).reshape(C, DV2)
            o = o + jnp.where(t_in >= sp, ac[sp * C:(sp + 1) * C] * vs, 0.0)

        st = st_ref[...]
        o = o + _dot_nt((q * jnp.exp(b)).astype(BF16), st.astype(BF16))
        bl = b[C - 1:C, :]
        kd = (k * jnp.exp(bl - b)).astype(BF16)
        ds = jnp.where(vk_same, _dot_tn(vb, kd), 0.0)
        st_ref[...] = st * jnp.exp(bl) + ds

        def ln(x):
            mu = jnp.mean(x, axis=-1, keepdims=True)
            xc = x - mu
            return xc * lax.rsqrt(jnp.mean(xc * xc, axis=-1, keepdims=True) + NORM_EPS)

        on = jnp.concatenate([ln(o[:, :GLA_DV]), ln(o[:, GLA_DV:])], axis=1) * ng_ref[...]
        rr = r_ref[0, pl.ds(r0, C), :].astype(F32)
        o_ref[0, pl.ds(r0, C), :] = (on * (rr * _sigmoid(rr))).astype(o_ref.dtype)
        return carry

    lax.fori_loop(0, TS // C, chunk, 0)


def gla_attention(proj, xa, w_alpha, b_alpha, norm_g, *, col_q, col_k, col_v, col_r):
    B, S, _ = proj.shape
    HP = GLA_HEADS // 2
    DK2, DV2 = 2 * GLA_DK, 2 * GLA_DV
    TS, C = min(GLA_TS, S), GLA_C
    assert S % TS == 0 and TS % C == 0 and C == 64
    kern = functools.partial(_gla_kernel, TS=TS, C=C)
    return pl.pallas_call(
        kern,
        out_shape=jax.ShapeDtypeStruct((B, S, GLA_HEADS * GLA_DV), BF16),
        grid=(B, HP, S // TS),
        in_specs=[pl.BlockSpec((1, TS, DK2), lambda b, p, s: (b, s, col_q + p)),
                  pl.BlockSpec((1, TS, DK2), lambda b, p, s: (b, s, col_k + p)),
                  pl.BlockSpec((1, TS, DV2), lambda b, p, s: (b, s, col_v // 2 + p)),
                  pl.BlockSpec((1, TS, DV2), lambda b, p, s: (b, s, col_r // 2 + p)),
                  pl.BlockSpec((1, TS, GLA_GATE_RANK), lambda b, p, s: (b, s, 0)),
                  pl.BlockSpec((GLA_GATE_RANK, DK2), lambda b, p, s: (0, p)),
                  pl.BlockSpec((1, DK2), lambda b, p, s: (0, p)),
                  pl.BlockSpec((1, DV2), lambda b, p, s: (0, p))],
        out_specs=pl.BlockSpec((1, TS, DV2), lambda b, p, s: (b, s, p)),
        scratch_shapes=[pltpu.VMEM((DV2, DK2), F32)],
        compiler_params=_cparams("parallel", "parallel", "arbitrary"),
        name="gla_attention",
    )(proj, proj, proj, proj, xa, w_alpha, b_alpha.reshape(1, -1), norm_g.reshape(1, -1))


def _mix_kernel(oa_ref, ob_ref, wa_ref, wb_ref, ma_ref, mb_ref, o_ref):
    ya = _dot(oa_ref[...], wa_ref[...])
    yb = _dot(ob_ref[...], wb_ref[...])
    o_ref[...] = (_sigmoid(ma_ref[...].astype(F32)) * ya + _sigmoid(mb_ref[...].astype(F32)) * yb).astype(o_ref.dtype)


def gated_mix(o_nsa, o_gla, wa, wb, proj2d, col_ma, col_mb):
    T, KA = o_nsa.shape
    KB = o_gla.shape[1]
    N = wa.shape[1]
    tm, tn = min(MIX_TM, T), min(MIX_TN, N)
    ca, cb = col_ma * LANES // tn, col_mb * LANES // tn
    assert (col_ma * LANES) % tn == 0 and (col_mb * LANES) % tn == 0
    return pl.pallas_call(
        _mix_kernel,
        out_shape=jax.ShapeDtypeStruct((T, N), BF16),
        grid=(T // tm, N // tn),
        in_specs=[pl.BlockSpec((tm, KA), lambda i, j: (i, 0)),
                  pl.BlockSpec((tm, KB), lambda i, j: (i, 0)),
                  pl.BlockSpec((KA, tn), lambda i, j: (0, j)),
                  pl.BlockSpec((KB, tn), lambda i, j: (0, j)),
                  pl.BlockSpec((tm, tn), lambda i, j: (i, ca + j)),
                  pl.BlockSpec((tm, tn), lambda i, j: (i, cb + j))],
        out_specs=pl.BlockSpec((tm, tn), lambda i, j: (i, j)),
        compiler_params=_cparams("parallel", "arbitrary"),
        name="gated_mix",
    )(o_nsa, o_gla, wa, wb, proj2d, proj2d)


def _out_kernel(a_ref, w_ref, x_ref, o_ref):
    o_ref[...] = x_ref[...] + _dot(a_ref[...], w_ref[...])


def out_proj(mixed, w, x):
    T, K = mixed.shape
    N = w.shape[1]
    tm, tn = min(OUT_TM, T), min(OUT_TN, N)
    return pl.pallas_call(
        _out_kernel,
        out_shape=jax.ShapeDtypeStruct((T, N), F32),
        grid=(T // tm, N // tn),
        in_specs=[pl.BlockSpec((tm, K), lambda i, j: (i, 0)),
                  pl.BlockSpec((K, tn), lambda i, j: (0, j)),
                  pl.BlockSpec((tm, tn), lambda i, j: (i, j))],
        out_specs=pl.BlockSpec((tm, tn), lambda i, j: (i, j)),
        compiler_params=_cparams("parallel", "arbitrary"),
        name="out_proj",
    )(mixed, w, x)


def _router_kernel(h_ref, g_ref, w_ref, b_ref, id_ref, wt_ref):
    x = h_ref[...]
    u = x * lax.rsqrt(jnp.mean(x * x, axis=-1, keepdims=True) + NORM_EPS) * g_ref[...]
    logit = _dot(u, w_ref[...], precision=lax.Precision.HIGHEST) + b_ref[...]
    lane = lax.broadcasted_iota(I32, logit.shape, 1)
    big = jnp.int32(1 << 20)
    gmask = lane < N_GROUPS
    gl = jnp.where(gmask, logit, NEG)
    gmax = jnp.max(gl, axis=-1, keepdims=True)
    g_star = jnp.min(jnp.where(gmask & (gl == gmax), lane, big), axis=-1, keepdims=True)
    p_group = 1.0 / jnp.sum(jnp.where(gmask, jnp.exp(gl - gmax), 0.0), axis=-1, keepdims=True)
    e_lo = N_GROUPS + EXPERTS_PER_GROUP * g_star
    emask = (lane >= e_lo) & (lane < e_lo + EXPERTS_PER_GROUP)
    el = jnp.where(emask, logit, NEG)
    m1 = jnp.max(el, axis=-1, keepdims=True)
    i1 = jnp.min(jnp.where(emask & (el == m1), lane, big), axis=-1, keepdims=True)
    emask2 = emask & (lane != i1)
    el2 = jnp.where(emask2, logit, NEG)
    m2 = jnp.max(el2, axis=-1, keepdims=True)
    i2 = jnp.min(jnp.where(emask2 & (el2 == m2), lane, big), axis=-1, keepdims=True)
    e2 = jnp.exp(m2 - m1)
    w1 = p_group / (1.0 + e2)
    w2 = p_group * e2 / (1.0 + e2)
    id_ref[...] = jnp.where(lane == 0, i1 - N_GROUPS, jnp.where(lane == 1, i2 - N_GROUPS, 0))
    wt_ref[...] = jnp.where(lane == 0, w1, jnp.where(lane == 1, w2, 0.0))


def router(h, g, w_r, b_r):
    T, D = h.shape
    tm = min(ROUTE_TM, T)
    return pl.pallas_call(
        _router_kernel,
        out_shape=(jax.ShapeDtypeStruct((T, LANES), I32), jax.ShapeDtypeStruct((T, LANES), F32)),
        grid=(T // tm,),
        in_specs=[pl.BlockSpec((tm, D), lambda i: (i, 0)),
                  pl.BlockSpec((1, D), lambda i: (0, 0)),
                  pl.BlockSpec((D, LANES), lambda i: (0, 0)),
                  pl.BlockSpec((1, LANES), lambda i: (0, 0))],
        out_specs=(pl.BlockSpec((tm, LANES), lambda i: (i, 0)), pl.BlockSpec((tm, LANES), lambda i: (i, 0))),
        compiler_params=_cparams("parallel"),
        name="moe_router",
    )(h, g.reshape(1, D), w_r, b_r)


def _row_copy(src_hbm, dst_vmem, sem, src_row, dst_row):
    return pltpu.make_async_copy(src_hbm.at[pl.ds(src_row, 1), :], dst_vmem.at[pl.ds(dst_row, 1), :], sem)


def _gather_norm_kernel(idx_ref, h_hbm, g_ref, o_ref, buf, sem, *, rows):
    def start(r, c):
        _row_copy(h_hbm, buf, sem, idx_ref[0, 0, r], r).start()
        return c

    def wait(r, c):
        _row_copy(h_hbm, buf, sem, 0, r).wait()
        return c

    lax.fori_loop(0, rows, start, 0)
    lax.fori_loop(0, rows, wait, 0)
    x = buf[...]
    o_ref[...] = (x * lax.rsqrt(jnp.mean(x * x, axis=-1, keepdims=True) + NORM_EPS) * g_ref[...]).astype(o_ref.dtype)


def gather_norm(h, row_token, g):
    T, D = h.shape
    NP = row_token.shape[0]
    rows = min(GATHER_ROWS, NP)
    assert NP % rows == 0
    return pl.pallas_call(
        functools.partial(_gather_norm_kernel, rows=rows),
        out_shape=jax.ShapeDtypeStruct((NP, D), BF16),
        grid=(NP // rows,),
        in_specs=[pl.BlockSpec((1, 1, rows), lambda i: (i, 0, 0), memory_space=pltpu.SMEM),
                  pl.BlockSpec(memory_space=pl.ANY),
                  pl.BlockSpec((1, D), lambda i: (0, 0))],
        out_specs=pl.BlockSpec((rows, D), lambda i: (i, 0)),
        scratch_shapes=[pltpu.VMEM((rows, D), F32), pltpu.SemaphoreType.DMA],
        compiler_params=_cparams("arbitrary"),
        name="moe_gather_norm",
    )(row_token.reshape(NP // rows, 1, rows), h, g.reshape(1, D))


def _ffn_up_kernel(te_ref, x_ref, wg_ref, wu_ref, o_ref):
    x = x_ref[...]
    a = _dot(x, wg_ref[0].astype(BF16))
    u = _dot(x, wu_ref[0].astype(BF16))
    o_ref[...] = (a * _sigmoid(a) * u).astype(o_ref.dtype)


def ffn_up(x_sorted, tile_expert, w_gate, w_up):
    NP, D = x_sorted.shape
    E, _, DE = w_gate.shape
    tm, cj = MOE_TM, min(MOE_CJ, DE)
    return pl.pallas_call(
        _ffn_up_kernel,
        out_shape=jax.ShapeDtypeStruct((NP, DE), BF16),
        grid_spec=pltpu.PrefetchScalarGridSpec(
            num_scalar_prefetch=1,
            grid=(NP // tm, DE // cj),
            in_specs=[pl.BlockSpec((tm, D), lambda i, j, te: (i, 0)),
                      pl.BlockSpec((1, D, cj), lambda i, j, te: (te[i], 0, j)),
                      pl.BlockSpec((1, D, cj), lambda i, j, te: (te[i], 0, j))],
            out_specs=pl.BlockSpec((tm, cj), lambda i, j, te: (i, j))),
        compiler_params=_cparams("parallel", "arbitrary"),
        name="moe_ffn_up",
    )(tile_expert, x_sorted, w_gate, w_up)


def _ffn_down_kernel(te_ref, h_ref, wd_ref, o_ref):
    o_ref[...] = _dot(h_ref[...], wd_ref[0].astype(BF16))


def ffn_down(h_sorted, tile_expert, w_down):
    NP, DE = h_sorted.shape
    E, _, D = w_down.shape
    tm, cn = MOE_TM, min(MOE_CN, D)
    return pl.pallas_call(
        _ffn_down_kernel,
        out_shape=jax.ShapeDtypeStruct((NP, D), F32),
        grid_spec=pltpu.PrefetchScalarGridSpec(
            num_scalar_prefetch=1,
            grid=(NP // tm, D // cn),
            in_specs=[pl.BlockSpec((tm, DE), lambda i, j, te: (i, 0)),
                      pl.BlockSpec((1, DE, cn), lambda i, j, te: (te[i], 0, j))],
            out_specs=pl.BlockSpec((tm, cn), lambda i, j, te: (i, j))),
        compiler_params=_cparams("parallel", "arbitrary"),
        name="moe_ffn_down",
    )(tile_expert, h_sorted, w_down)


def _combine_kernel(idx_ref, h_ref, wt_ref, y_hbm, g_ref, o_ref, buf, sem, *, rows):
    def start(r, c):
        _row_copy(y_hbm, buf.at[0], sem, idx_ref[0, 0, 2 * r], r).start()
        _row_copy(y_hbm, buf.at[1], sem, idx_ref[0, 0, 2 * r + 1], r).start()
        return c

    def wait(r, c):
        _row_copy(y_hbm, buf.at[0], sem, 0, r).wait()
        _row_copy(y_hbm, buf.at[1], sem, 0, r).wait()
        return c

    lax.fori_loop(0, rows, start, 0)
    lax.fori_loop(0, rows, wait, 0)
    wt = wt_ref[...]
    x = h_ref[...] + wt[:, 0:1] * buf[0] + wt[:, 1:2] * buf[1]
    o_ref[...] = x * lax.rsqrt(jnp.mean(x * x, axis=-1, keepdims=True) + NORM_EPS) * g_ref[...]


def combine_norm(h, wts, dest, y_sorted, g):
    T, D = h.shape
    rows = min(COMBINE_ROWS, T)
    return pl.pallas_call(
        functools.partial(_combine_kernel, rows=rows),
        out_shape=jax.ShapeDtypeStruct((T, D), F32),
        grid=(T // rows,),
        in_specs=[pl.BlockSpec((1, 1, 2 * rows), lambda i: (i, 0, 0), memory_space=pltpu.SMEM),
                  pl.BlockSpec((rows, D), lambda i: (i, 0)),
                  pl.BlockSpec((rows, LANES), lambda i: (i, 0)),
                  pl.BlockSpec(memory_space=pl.ANY),
                  pl.BlockSpec((1, D), lambda i: (0, 0))],
        out_specs=pl.BlockSpec((rows, D), lambda i: (i, 0)),
        scratch_shapes=[pltpu.VMEM((2, rows, D), F32), pltpu.SemaphoreType.DMA],
        compiler_params=_cparams("arbitrary"),
        name="moe_combine_norm",
    )(dest.reshape(T // rows, 1, 2 * rows), h, wts, y_sorted, g.reshape(1, D))


def _dispatch_tables(ids, tm):
    T = ids.shape[0]
    E = N_EXPERTS
    eid = ids.reshape(-1)
    onehot = (eid[:, None] == jnp.arange(E, dtype=I32)[None, :]).astype(I32)
    csum = jnp.cumsum(onehot, axis=0)
    rank = jnp.sum(csum * onehot, axis=1) - 1
    counts = csum[-1]
    padded = ((counts + tm - 1) // tm) * tm
    ends = jnp.cumsum(padded)
    dest = (ends - padded)[eid] + rank
    NP = 2 * T + E * tm
    row_token = jnp.zeros((NP,), I32).at[dest].set(jnp.arange(2 * T, dtype=I32) // 2)
    tile_start = jnp.arange(NP // tm, dtype=I32) * tm
    tile_expert = jnp.searchsorted(ends, tile_start, side="right").astype(I32)
    last_used = jnp.max(jnp.where(counts > 0, jnp.arange(E, dtype=I32), 0))
    tile_expert = jnp.minimum(tile_expert, last_used)
    return dest.astype(I32), row_token, tile_expert


def _forward(x, norm_mix_g, w_in, cmp_k_pos, cmp_k_w1, cmp_k_b1, cmp_k_w2, cmp_v_pos, cmp_v_w1, cmp_v_b1, cmp_v_w2,
             gla_w_alpha, gla_b_alpha, gla_norm_g, w_branch_nsa, w_branch_gla, w_out, norm_ffn_g,
             w_router_group, b_router_group, w_router_expert, b_router_expert, w_exp_gate, w_exp_up, w_exp_down,
             norm_final_g):
    B, S, D = x.shape
    T = B * S
    G, R, Dh = NSA_KV_HEADS, NSA_GROUP, NSA_HEAD_DIM
    NSA_Q, NSA_KV = NSA_HEADS * Dh, G * Dh
    GQK, GV = GLA_HEADS * GLA_DK, GLA_HEADS * GLA_DV
    h = x.reshape(T, D)
    assert w_in.shape[0] == 1, "the final norm is fused into the (single) layer's combine step"
    for l in range(1):
        w = w_in[l]
        o_ng = NSA_Q + 6 * NSA_KV
        o_gq = o_ng + 3 * NSA_HEADS
        o_ga = o_gq + 2 * GQK + 2 * GV
        o_ma = o_ga + GLA_GATE_RANK
        w_main = jnp.concatenate([w[:, :o_ng], w[:, o_gq:o_ga], w[:, o_ma:]], axis=1).astype(BF16)
        n_small = 3 * NSA_HEADS + GLA_GATE_RANK
        w_small = jnp.concatenate([w[:, o_ng:o_gq], w[:, o_ga:o_ma], jnp.zeros((D, LANES - n_small), F32)],
                                  axis=1).astype(BF16)
        c_q = 0
        c_kc = NSA_Q // LANES
        c_ks, c_vs, c_kw, c_vw = c_kc + 2 * G, c_kc + 3 * G, c_kc + 4 * G, c_kc + 5 * G
        c_gq = c_kc + 6 * G
        c_gk = c_gq + GQK // LANES
        c_gv = c_gk + GQK // LANES
        c_gr = c_gv + GV // LANES
        c_ma = c_gr + GV // LANES
        c_mb = c_ma + D // LANES

        xn = rmsnorm(h, norm_mix_g[l])
        proj = matmul(xn, w_main, BF16, PROJ_TM, PROJ_TN, "proj_main")
        small = matmul(xn, w_small, F32, PROJ_TM, LANES, "proj_small")
        proj3 = proj.reshape(B, S, -1)

        NC = S // CMP_STRIDE
        kvc_cols = proj3[:, :, c_kc * LANES:(c_kc + 2 * G) * LANES]
        seq2 = kvc_cols.reshape(B, NC, CMP_STRIDE, 2, G, Dh).transpose(3, 0, 4, 1, 2, 5).reshape(
            2, B, G, NC, CMP_STRIDE * Dh)
        pos2 = jnp.stack([cmp_k_pos[l], cmp_v_pos[l]]).reshape(2, 2, CMP_STRIDE * Dh)
        w1 = jnp.stack([cmp_k_w1[l], cmp_v_w1[l]]).astype(BF16)
        b1 = jnp.stack([cmp_k_b1[l], cmp_v_b1[l]]).reshape(2, 1, Dh)
        w2 = jnp.stack([cmp_k_w2[l], cmp_v_w2[l]]).astype(BF16)
        kvc = compress(seq2, pos2, w1, b1, w2)
        gates = small[:, :3 * NSA_HEADS].reshape(B, S, G, 3 * R).transpose(0, 2, 1, 3)
        slopes = jnp.exp2(-8.0 * jnp.arange(1, NSA_HEADS + 1, dtype=F32) / NSA_HEADS)
        o_nsa = nsa_attention(proj3, kvc, gates, slopes, col_q=c_q, col_ks=c_ks, col_vs=c_vs,
                              col_kw=c_kw, col_vw=c_vw)

        xa = small[:, 3 * NSA_HEADS:n_small].reshape(B, S, GLA_GATE_RANK)
        o_gla = gla_attention(proj3, xa, gla_w_alpha[l], gla_b_alpha[l], gla_norm_g[l],
                              col_q=c_gq, col_k=c_gk, col_v=c_gv, col_r=c_gr)

        mixed = gated_mix(o_nsa.reshape(T, -1), o_gla.reshape(T, -1), w_branch_nsa[l].astype(BF16),
                          w_branch_gla[l].astype(BF16), proj, c_ma, c_mb)
        h = out_proj(mixed, w_out[l].astype(BF16), h)

        n_r = N_GROUPS + N_EXPERTS
        w_r = jnp.concatenate([w_router_group[l], w_router_expert[l], jnp.zeros((D, LANES - n_r), F32)], axis=1)
        b_r = jnp.concatenate([b_router_group[l], b_router_expert[l], jnp.zeros((LANES - n_r,), F32)]).reshape(1, LANES)
        ids, wts = router(h, norm_ffn_g[l], w_r, b_r)
        dest, row_token, tile_expert = _dispatch_tables(ids[:, :2], MOE_TM)
        x_sorted = gather_norm(h, row_token, norm_ffn_g[l])
        h_sorted = ffn_up(x_sorted, tile_expert, w_exp_gate[l], w_exp_up[l])
        y_sorted = ffn_down(h_sorted, tile_expert, w_exp_down[l])
        h = combine_norm(h, wts, dest, y_sorted, norm_final_g)
    return h.reshape(B, S, D)


def kernel(x, norm_mix_g, w_in, cmp_k_pos, cmp_k_w1, cmp_k_b1, cmp_k_w2, cmp_v_pos, cmp_v_w1, cmp_v_b1, cmp_v_w2, gla_w_alpha, gla_b_alpha, gla_norm_g, w_branch_nsa, w_branch_gla, w_out, norm_ffn_g, w_router_group, b_router_group, w_router_expert, b_router_expert, w_exp_gate, w_exp_up, w_exp_down, norm_final_g):
    return _forward(x, norm_mix_g, w_in, cmp_k_pos, cmp_k_w1, cmp_k_b1, cmp_k_w2, cmp_v_pos, cmp_v_w1, cmp_v_b1,
                    cmp_v_w2, gla_w_alpha, gla_b_alpha, gla_norm_g, w_branch_nsa, w_branch_gla, w_out, norm_ffn_g,
                    w_router_group, b_router_group, w_router_expert, b_router_expert, w_exp_gate, w_exp_up,
                    w_exp_down, norm_final_g)
```

```python
import functools

import jax
import jax.numpy as jnp
from jax import lax
from jax.experimental import pallas as pl
from jax.experimental.pallas import tpu as pltpu

F32 = jnp.float32
BF16 = jnp.bfloat16
I32 = jnp.int32

NSA_HEAD_DIM = 128
NSA_KV_HEADS = 4
NSA_GROUP = 4
NSA_HEADS = NSA_KV_HEADS * NSA_GROUP
CMP_BLOCK = 32
CMP_STRIDE = 16
SEL_BLOCK = 64
SEL_TOPN = 16
WINDOW = 512
BIG = 1e9
GLA_HEADS = 16
GLA_DK = 64
GLA_DV = 128
GLA_GATE_RANK = 16
GLA_TAU = 16.0
N_GROUPS = 4
EXPERTS_PER_GROUP = 8
N_EXPERTS = N_GROUPS * EXPERTS_PER_GROUP
NORM_EPS = 1e-6

LANES = 128
VMEM_LIMIT = 56 * 1024 * 1024
NEG = -1e30

NORM_ROWS = 512
PROJ_TM, PROJ_TN = 1024, 512
NSA_TQ, NSA_TK = 128, 256
GLA_TS, GLA_C, GLA_SUB = 512, 64, 16
MIX_TM, MIX_TN = 512, 1024
OUT_TM, OUT_TN = 512, 1024
ROUTE_TM = 256
MOE_TM = 512
MOE_CJ = 512
MOE_CN = 2048
GATHER_PAIRS = 1024
COMBINE_ROWS = 128


def _cparams(*sem):
    return pltpu.CompilerParams(dimension_semantics=sem, vmem_limit_bytes=VMEM_LIMIT)


def _dot(a, b, **kw):
    return jnp.dot(a, b, preferred_element_type=F32, **kw)


def _dot_nt(a, b, **kw):
    return lax.dot_general(a, b, (((1,), (1,)), ((), ())), preferred_element_type=F32, **kw)


def _dot_tn(a, b, **kw):
    return lax.dot_general(a, b, (((0,), (0,)), ((), ())), preferred_element_type=F32, **kw)


def _sigmoid(x):
    return 1.0 / (1.0 + jnp.exp(-x))


def _masked_softmax(s, mask):
    sm = jnp.where(mask, s, NEG)
    m = jnp.max(sm, axis=-1, keepdims=True)
    e = jnp.where(mask, jnp.exp(sm - m), 0.0)
    return e * (1.0 / jnp.maximum(jnp.sum(e, axis=-1, keepdims=True), 1e-30))


def _rmsnorm_kernel(x_ref, g_ref, o_ref):
    x = x_ref[...]
    ms = jnp.mean(x * x, axis=-1, keepdims=True)
    o_ref[...] = (x * lax.rsqrt(ms + NORM_EPS) * g_ref[...]).astype(o_ref.dtype)


def rmsnorm(x, g, out_dtype=BF16, rows=NORM_ROWS):
    T, D = x.shape
    rows = min(rows, T)
    return pl.pallas_call(
        _rmsnorm_kernel,
        out_shape=jax.ShapeDtypeStruct((T, D), out_dtype),
        grid=(T // rows,),
        in_specs=[pl.BlockSpec((rows, D), lambda i: (i, 0)),
                  pl.BlockSpec((1, D), lambda i: (0, 0))],
        out_specs=pl.BlockSpec((rows, D), lambda i: (i, 0)),
        compiler_params=_cparams("parallel"),
        name="rmsnorm",
    )(x, g.reshape(1, D))


def _mm_kernel(a_ref, b_ref, o_ref):
    o_ref[...] = _dot(a_ref[...], b_ref[...]).astype(o_ref.dtype)


def matmul(a, b, out_dtype, tm, tn, name):
    M, K = a.shape
    N = b.shape[1]
    tm, tn = min(tm, M), min(tn, N)
    return pl.pallas_call(
        _mm_kernel,
        out_shape=jax.ShapeDtypeStruct((M, N), out_dtype),
        grid=(M // tm, N // tn),
        in_specs=[pl.BlockSpec((tm, K), lambda i, j: (i, 0)),
                  pl.BlockSpec((K, tn), lambda i, j: (0, j))],
        out_specs=pl.BlockSpec((tm, tn), lambda i, j: (i, j)),
        compiler_params=_cparams("parallel", "arbitrary"),
        name=name,
    )(a, b)


def _compress_kernel(seq_ref, pos_ref, w1_ref, b1_ref, w2_ref, o_ref):
    x = seq_ref[0, 0, 0].astype(F32)
    nc = x.shape[0]
    half = x.shape[1]
    pos = pos_ref[0]
    w1 = w1_ref[0]
    u0 = _dot((x + pos[0:1, :]).astype(BF16), w1[:half, :])
    u1 = _dot((x + pos[1:2, :]).astype(BF16), w1[half:, :])
    pre = u0 + pltpu.roll(u1, nc - 1, 0) + b1_ref[0]
    h = 0.5 * pre * (1.0 + jnp.tanh(0.7978845608028654 * (pre + 0.044715 * pre * pre * pre)))
    o_ref[0, 0, 0] = _dot(h.astype(BF16), w2_ref[0]).astype(o_ref.dtype)


def compress(seq2, pos2, w1, b1, w2):
    _, B, G, NC, HW = seq2.shape
    Dh = w2.shape[-1]
    return pl.pallas_call(
        _compress_kernel,
        out_shape=jax.ShapeDtypeStruct((2, B, G, NC, Dh), BF16),
        grid=(2, B, G),
        in_specs=[pl.BlockSpec((1, 1, 1, NC, HW), lambda a, b, g: (a, b, g, 0, 0)),
                  pl.BlockSpec((1, 2, HW), lambda a, b, g: (a, 0, 0)),
                  pl.BlockSpec((1, 2 * HW, Dh), lambda a, b, g: (a, 0, 0)),
                  pl.BlockSpec((1, 1, Dh), lambda a, b, g: (a, 0, 0)),
                  pl.BlockSpec((1, Dh, Dh), lambda a, b, g: (a, 0, 0))],
        out_specs=pl.BlockSpec((1, 1, 1, NC, Dh), lambda a, b, g: (a, b, g, 0, 0)),
        compiler_params=_cparams("parallel", "parallel", "parallel"),
        name="nsa_compress",
    )(seq2, pos2, w1, b1, w2)


MASK_BIG = 2.0 ** 100


def _nsa_kernel(slopes_ref, q_ref, kc_ref, vc_ref, ks_ref, vs_ref, kw_ref, vw_ref, g_ref,
                paux_ref, saux_ref, caux_ref, mmat_ref, o_ref,
                qx_ref, sc_ref, sw_ref, sa_ref, sb_ref, score_ref, m_ref, l_ref, acc_ref,
                *, TQ, TK, NC, NSEL, NTOP):
    R, Dh = NSA_GROUP, NSA_HEAD_DIM
    g = pl.program_id(1)
    q0 = pl.program_id(2) * TQ
    scale = Dh ** -0.5
    lane = lax.broadcasted_iota(I32, (1, LANES), 1)
    t1 = q0 + lax.broadcasted_iota(I32, (TQ, 1), 0)
    head = lambda r: slice(r * TQ, (r + 1) * TQ)

    def alibi_cols(r):
        hi, lo = slopes_ref[0, g * R + r], slopes_ref[1, g * R + r]
        c = jnp.where(lane == 0, 64.0 * hi, jnp.where(lane == 1, 64.0 * lo,
                                                      jnp.where(lane == 2, hi, jnp.where(lane == 3, lo, 0.0))))
        return jnp.broadcast_to(c, (TQ, LANES))

    for r in range(R):
        qx_ref[head(r), 0:Dh] = (q_ref[0, :, r * Dh:(r + 1) * Dh].astype(F32) * scale).astype(BF16)
        qx_ref[head(r), Dh:Dh + LANES] = alibi_cols(r).astype(BF16)

    def softmax_pv(s, v, row_ok=None):
        e = jnp.exp(s - jnp.max(s, axis=-1, keepdims=True))
        inv = 1.0 / jnp.maximum(jnp.sum(e, axis=-1, keepdims=True), 1e-30)
        if row_ok is not None:
            inv = jnp.where(row_ok, inv, 0.0)
        p = e * inv
        return p, _dot(p.astype(BF16), v)

    WK = WINDOW + TQ
    ws = pl.multiple_of(jnp.maximum(q0 - WINDOW, 0), TQ)
    sc_ref[...] = _dot_nt(qx_ref[...], jnp.concatenate([kc_ref[0, 0, 0], caux_ref[...]], axis=1))
    sw_ref[...] = _dot_nt(qx_ref[...], jnp.concatenate([kw_ref[0, pl.ds(ws, WK), :], paux_ref[pl.ds(ws, WK), :]],
                                                       axis=1))

    cmp_end = lax.broadcasted_iota(I32, (1, NC), 1) * CMP_STRIDE + (CMP_BLOCK - 1)
    cmp_bias = jnp.where(cmp_end <= t1, 0.0, NEG)
    vc = vc_ref[0, 0, 0]
    o_cmp = []
    imp = None
    for r in range(R):
        p, o = softmax_pv(sc_ref[head(r), :] + cmp_bias, vc, t1 >= CMP_BLOCK - 1)
        o_cmp.append(o)
        imp = p if imp is None else imp + p

    vw = vw_ref[0, pl.ds(ws, WK), :]
    dw = t1 - (ws + lax.broadcasted_iota(I32, (1, WK), 1))
    win_bias = jnp.where((dw >= 0) & (dw < WINDOW), 0.0, NEG)
    o_win = [softmax_pv(sw_ref[head(r), :] + win_bias, vw)[1] for r in range(R)]

    mm = mmat_ref[...]
    i1 = imp.astype(BF16)
    rem = imp - i1.astype(F32)
    i2 = rem.astype(BF16)
    i3 = (rem - i2.astype(F32)).astype(BF16)
    imp_sel = _dot_nt(mm, i1) + _dot_nt(mm, i2) + _dot_nt(mm, i3)

    cur = (q0 + lax.broadcasted_iota(I32, (1, TQ), 1)) // SEL_BLOCK
    blk = lax.broadcasted_iota(I32, (NSEL, 1), 0)
    forced = (blk == 0) | (blk == cur) | (blk == cur - 1)
    score = jnp.where(blk <= cur, jnp.where(forced, BIG, imp_sel), -BIG)
    score_ref[...] = score

    def rank_pair(i2, cnt):
        for d in range(2):
            ii = 2 * i2 + d
            other = score_ref[pl.ds(ii, 1), :]
            beats = (other > score) | ((other == score) & (blk > ii))
            cnt = cnt + jnp.where(beats, 1.0, 0.0)
        return cnt

    n_blk = (q0 + TQ) // SEL_BLOCK
    cnt = lax.fori_loop(0, n_blk // 2, rank_pair, jnp.zeros((NSEL, TQ), F32))
    unsel = jnp.where(cnt < NTOP, 0.0, -1.0).astype(BF16)
    place = jnp.where(lax.broadcasted_iota(I32, (NSEL, LANES), 1)
                      == lax.broadcasted_iota(I32, (NSEL, LANES), 0) + LANES // 2, 1.0, 0.0).astype(BF16)
    unsel_l = _dot_tn(unsel, place)
    for r in range(R):
        qx_ref[head(r), Dh:Dh + LANES] = (alibi_cols(r) + unsel_l).astype(BF16)

    m_ref[...] = jnp.full(m_ref.shape, NEG, F32)
    l_ref[...] = jnp.zeros(l_ref.shape, F32)
    acc_ref[...] = jnp.zeros(acc_ref.shape, F32)
    pos_in_tile = lax.broadcasted_iota(I32, (1, TK), 1)

    def issue_scores(kt, s_ref):
        k0 = pl.multiple_of(kt * TK, TK)
        kk = jnp.concatenate([ks_ref[0, pl.ds(k0, TK), :], saux_ref[pl.ds(k0, TK), :]], axis=1)
        s_ref[...] = _dot_nt(qx_ref[...], kk)

    def consume_scores(kt, s_ref, causal):
        k0 = pl.multiple_of(kt * TK, TK)
        vv = vs_ref[0, pl.ds(k0, TK), :]
        if causal:
            causal_bias = jnp.where(t1 >= k0 + pos_in_tile, 0.0, -MASK_BIG)
        for r in range(R):
            sc = s_ref[head(r), :]
            if causal:
                sc = sc + causal_bias
            m_old = m_ref[r]
            m_new = jnp.maximum(m_old, jnp.broadcast_to(jnp.max(sc, axis=-1, keepdims=True), (TQ, LANES)))
            alpha = jnp.exp(m_old - m_new)
            chunks = [jnp.exp(sc[:, c * LANES:(c + 1) * LANES] - m_new) for c in range(TK // LANES)]
            psum = chunks[0]
            for pc in chunks[1:]:
                psum = psum + pc
            m_ref[r] = m_new
            l_ref[r] = alpha * l_ref[r] + psum
            acc_ref[r] = alpha * acc_ref[r] + _dot(jnp.concatenate(chunks, axis=1).astype(BF16), vv)

    def sel_pair(k, carry):
        issue_scores(2 * k + 1, sb_ref)
        consume_scores(2 * k, sa_ref, False)
        issue_scores(2 * k + 2, sa_ref)
        consume_scores(2 * k + 1, sb_ref, False)
        return carry

    n_below = q0 // TK
    issue_scores(0, sa_ref)
    lax.fori_loop(0, n_below // 2, sel_pair, 0)

    @pl.when(n_below % 2 == 1)
    def _():
        issue_scores(n_below, sb_ref)
        consume_scores(n_below - 1, sa_ref, False)
        consume_scores(n_below, sb_ref, True)

    @pl.when(n_below % 2 == 0)
    def _():
        consume_scores(n_below, sa_ref, True)

    gt = _sigmoid(g_ref[0, 0])
    for r in range(R):
        l_sel = jnp.sum(l_ref[r], axis=-1, keepdims=True)
        o_sel = acc_ref[r] * (1.0 / jnp.maximum(l_sel, 1e-30))
        o = (gt[:, 3 * r:3 * r + 1] * o_cmp[r] + gt[:, 3 * r + 1:3 * r + 2] * o_sel
             + gt[:, 3 * r + 2:3 * r + 3] * o_win[r])
        o_ref[0, :, r * Dh:(r + 1) * Dh] = o.astype(o_ref.dtype)


def _nsa_constants(S, NC, NSEL):
    def pos_cols(pos):
        col = jnp.arange(LANES)[None, :]
        hi, lo = (pos // 64)[:, None], (pos % 64)[:, None]
        return jnp.where(col < 2, hi, jnp.where(col < 4, lo, 0)).astype(F32)

    pos = jnp.arange(S)
    paux = pos_cols(pos)
    onehot = (jnp.arange(LANES)[None, :] - LANES // 2 == (pos // SEL_BLOCK)[:, None]) & (jnp.arange(LANES)[None, :] >= LANES // 2)
    saux = paux + jnp.where(onehot, MASK_BIG, 0.0)
    caux = pos_cols(jnp.arange(NC) * CMP_STRIDE + (CMP_BLOCK - 1))
    d = jnp.arange(NC)[None, :] - 4 * jnp.arange(NSEL)[:, None]
    mmat = jnp.where((d == -1) | (d == 3), 1.0, jnp.where((d >= 0) & (d <= 2), 2.0, 0.0))
    return paux.astype(BF16), saux.astype(BF16), caux.astype(BF16), mmat.astype(BF16)


def nsa_attention(proj, kvc, gates, slopes, *, col_q, col_ks, col_vs, col_kw, col_vw):
    B, S, _ = proj.shape
    G, R, Dh = NSA_KV_HEADS, NSA_GROUP, NSA_HEAD_DIM
    NC = kvc.shape[3]
    NSEL = S // SEL_BLOCK
    TQ, TK = min(NSA_TQ, S), min(NSA_TK, S)
    assert S % TQ == 0 and TK % TQ == 0 and S % TK == 0 and S >= WINDOW + TQ and NC == S // CMP_STRIDE
    assert NSEL <= LANES // 2 and S // 64 <= 256 and TQ % (2 * SEL_BLOCK) == 0
    s_hi = slopes.astype(BF16).astype(F32)
    s_lo = (slopes - s_hi).astype(BF16).astype(F32)
    paux, saux, caux, mmat = _nsa_constants(S, NC, NSEL)
    kern = functools.partial(_nsa_kernel, TQ=TQ, TK=TK, NC=NC, NSEL=NSEL, NTOP=min(SEL_TOPN, NSEL))
    kv_spec = lambda col: pl.BlockSpec((1, S, Dh), lambda b, g, i: (b, 0, col + g))
    whole = lambda a: pl.BlockSpec(a.shape, lambda b, g, i: (0, 0))
    return pl.pallas_call(
        kern,
        out_shape=jax.ShapeDtypeStruct((B, S, G * R * Dh), BF16),
        grid=(B, G, S // TQ),
        in_specs=[pl.BlockSpec(memory_space=pltpu.SMEM),
                  pl.BlockSpec((1, TQ, R * Dh), lambda b, g, i: (b, i, col_q // R + g)),
                  pl.BlockSpec((1, 1, 1, NC, Dh), lambda b, g, i: (0, b, g, 0, 0)),
                  pl.BlockSpec((1, 1, 1, NC, Dh), lambda b, g, i: (1, b, g, 0, 0)),
                  kv_spec(col_ks), kv_spec(col_vs), kv_spec(col_kw), kv_spec(col_vw),
                  pl.BlockSpec((1, 1, TQ, 3 * R), lambda b, g, i: (b, g, i, 0)),
                  whole(paux), whole(saux), whole(caux), whole(mmat)],
        out_specs=pl.BlockSpec((1, TQ, R * Dh), lambda b, g, i: (b, i, g)),
        scratch_shapes=[pltpu.VMEM((R * TQ, Dh + LANES), BF16), pltpu.VMEM((R * TQ, NC), F32),
                        pltpu.VMEM((R * TQ, WINDOW + TQ), F32), pltpu.VMEM((R * TQ, TK), F32),
                        pltpu.VMEM((R * TQ, TK), F32),
                        pltpu.VMEM((NSEL, TQ), F32), pltpu.VMEM((R, TQ, LANES), F32),
                        pltpu.VMEM((R, TQ, LANES), F32), pltpu.VMEM((R, TQ, Dh), F32)],
        compiler_params=_cparams("parallel", "parallel", "arbitrary"),
        name="nsa_attention",
    )(jnp.stack([s_hi, s_lo]), proj, kvc, kvc, proj, proj, proj, proj, gates, paux, saux, caux, mmat)


def _gla_kernel(q_ref, k_ref, v_ref, r_ref, xa_ref, wa_ref, ba_ref, ng_ref, o_ref, st_ref, la_ref, *, TS, C):
    DK2, DV2 = 2 * GLA_DK, 2 * GLA_DV
    SUB = GLA_SUB
    NSUB = C // SUB

    @pl.when(pl.program_id(2) == 0)
    def _():
        st_ref[...] = jnp.zeros_like(st_ref)

    z = _dot(xa_ref[0], wa_ref[...], precision=lax.Precision.HIGHEST) + ba_ref[...]
    la_ref[...] = (jnp.minimum(z, 0.0) - jnp.log(1.0 + jnp.exp(-jnp.abs(z)))) * (1.0 / GLA_TAU)

    lane = lax.broadcasted_iota(I32, (1, DK2), 1)
    head_a = lane < GLA_DK
    row = lax.broadcasted_iota(I32, (C, 1), 0)
    col = lax.broadcasted_iota(I32, (1, C), 1)
    tri = jnp.where(col <= row, 1.0, 0.0).astype(BF16)
    mask1 = (row // 32 == 1) & (col // 32 == 0)
    mask2 = (row // 32 == col // 32) & ((row // 16) % 2 == 1) & ((col // 16) % 2 == 0)
    kv_same = (lax.broadcasted_iota(I32, (DK2, DV2), 0) < GLA_DK) == (lax.broadcasted_iota(I32, (DK2, DV2), 1) < GLA_DV)
    hsel = jnp.where(kv_same, 1.0, 0.0).astype(BF16)
    vk_same = (lax.broadcasted_iota(I32, (DV2, DK2), 0) < GLA_DV) == (lax.broadcasted_iota(I32, (DV2, DK2), 1) < GLA_DK)
    t_in = row % SUB

    def chunk(c, carry):
        r0 = pl.multiple_of(c * C, C)
        q = q_ref[0, pl.ds(r0, C), :].astype(F32) * (GLA_DK ** -0.5)
        k = k_ref[0, pl.ds(r0, C), :].astype(F32)
        vb = v_ref[0, pl.ds(r0, C), :]
        v = vb.astype(F32)
        la = la_ref[pl.ds(r0, C), :]
        l1 = la.astype(BF16)
        lr = la - l1.astype(F32)
        l2 = lr.astype(BF16)
        l3 = (lr - l2.astype(F32)).astype(BF16)
        b = _dot(tri, l1) + _dot(tri, l2) + _dot(tri, l3)

        r1 = b[31:32, :]
        r2 = jnp.where(row < 32, b[15:16, :], b[47:48, :])
        q1 = q * jnp.exp(jnp.minimum(b - r1, 0.0))
        k1 = (k * jnp.exp(jnp.minimum(r1 - b, 0.0))).astype(BF16)
        q2 = q * jnp.exp(jnp.minimum(b - r2, 0.0))
        k2 = (k * jnp.exp(jnp.minimum(r2 - b, 0.0))).astype(BF16)

        def scores(hm):
            a1 = _dot_nt(jnp.where(hm, q1, 0.0).astype(BF16), k1)
            a2 = _dot_nt(jnp.where(hm, q2, 0.0).astype(BF16), k2)
            return (jnp.where(mask1, a1, 0.0) + jnp.where(mask2, a2, 0.0)).astype(BF16)

        o = jnp.concatenate([_dot(scores(head_a), vb[:, :GLA_DV]),
                             _dot(scores(jnp.logical_not(head_a)), vb[:, GLA_DV:])], axis=1)

        b3 = b.reshape(NSUB, SUB, DK2)
        k3 = k.reshape(NSUB, SUB, DK2)
        v3 = v.reshape(NSUB, SUB, DV2)
        prods = []
        for sp in range(SUB):
            bs = jnp.broadcast_to(b3[:, sp:sp + 1, :], (NSUB, SUB, DK2)).reshape(C, DK2)
            ks = jnp.broadcast_to(k3[:, sp:sp + 1, :], (NSUB, SUB, DK2)).reshape(C, DK2)
            prods.append((q * ks * jnp.exp(jnp.minimum(b - bs, 0.0))).astype(BF16))
        ac = _dot(jnp.concatenate(prods, axis=0), hsel)
        for sp in range(SUB):
            vs = jnp.broadcast_to(v3[:, sp:sp + 1, :], (NSUB, SUB, DV2)).reshape(C, DV2)
            o = o + jnp.where(t_in >= sp, ac[sp * C:(sp + 1) * C] * vs, 0.0)

        st = st_ref[...]
        o = o + _dot_nt((q * jnp.exp(b)).astype(BF16), st.astype(BF16))
        bl = b[C - 1:C, :]
        kd = (k * jnp.exp(bl - b)).astype(BF16)
        ds = jnp.where(vk_same, _dot_tn(vb, kd), 0.0)
        st_ref[...] = st * jnp.exp(bl) + ds

        def ln(x):
            mu = jnp.mean(x, axis=-1, keepdims=True)
            xc = x - mu
            return xc * lax.rsqrt(jnp.mean(xc * xc, axis=-1, keepdims=True) + NORM_EPS)

        on = jnp.concatenate([ln(o[:, :GLA_DV]), ln(o[:, GLA_DV:])], axis=1) * ng_ref[...]
        rr = r_ref[0, pl.ds(r0, C), :].astype(F32)
        o_ref[0, pl.ds(r0, C), :] = (on * (rr * _sigmoid(rr))).astype(o_ref.dtype)
        return carry

    lax.fori_loop(0, TS // C, chunk, 0, unroll=8)


def gla_attention(proj, xa, w_alpha, b_alpha, norm_g, *, col_q, col_k, col_v, col_r):
    B, S, _ = proj.shape
    HP = GLA_HEADS // 2
    DK2, DV2 = 2 * GLA_DK, 2 * GLA_DV
    TS, C = min(GLA_TS, S), GLA_C
    assert S % TS == 0 and TS % C == 0 and C == 64
    kern = functools.partial(_gla_kernel, TS=TS, C=C)
    return pl.pallas_call(
        kern,
        out_shape=jax.ShapeDtypeStruct((B, S, GLA_HEADS * GLA_DV), BF16),
        grid=(B, HP, S // TS),
        in_specs=[pl.BlockSpec((1, TS, DK2), lambda b, p, s: (b, s, col_q + p)),
                  pl.BlockSpec((1, TS, DK2), lambda b, p, s: (b, s, col_k + p)),
                  pl.BlockSpec((1, TS, DV2), lambda b, p, s: (b, s, col_v // 2 + p)),
                  pl.BlockSpec((1, TS, DV2), lambda b, p, s: (b, s, col_r // 2 + p)),
                  pl.BlockSpec((1, TS, GLA_GATE_RANK), lambda b, p, s: (b, s, 0)),
                  pl.BlockSpec((GLA_GATE_RANK, DK2), lambda b, p, s: (0, p)),
                  pl.BlockSpec((1, DK2), lambda b, p, s: (0, p)),
                  pl.BlockSpec((1, DV2), lambda b, p, s: (0, p))],
        out_specs=pl.BlockSpec((1, TS, DV2), lambda b, p, s: (b, s, p)),
        scratch_shapes=[pltpu.VMEM((DV2, DK2), F32), pltpu.VMEM((TS, DK2), F32)],
        compiler_params=_cparams("parallel", "parallel", "arbitrary"),
        name="gla_attention",
    )(proj, proj, proj, proj, xa, w_alpha, b_alpha.reshape(1, -1), norm_g.reshape(1, -1))


def _mix_kernel(oa_ref, ob_ref, wa_ref, wb_ref, ma_ref, mb_ref, o_ref):
    ya = _dot(oa_ref[...], wa_ref[...])
    yb = _dot(ob_ref[...], wb_ref[...])
    o_ref[...] = (_sigmoid(ma_ref[...].astype(F32)) * ya + _sigmoid(mb_ref[...].astype(F32)) * yb).astype(o_ref.dtype)


def gated_mix(o_nsa, o_gla, wa, wb, proj2d, col_ma, col_mb):
    T, KA = o_nsa.shape
    KB = o_gla.shape[1]
    N = wa.shape[1]
    tm, tn = min(MIX_TM, T), min(MIX_TN, N)
    ca, cb = col_ma * LANES // tn, col_mb * LANES // tn
    assert (col_ma * LANES) % tn == 0 and (col_mb * LANES) % tn == 0
    return pl.pallas_call(
        _mix_kernel,
        out_shape=jax.ShapeDtypeStruct((T, N), BF16),
        grid=(T // tm, N // tn),
        in_specs=[pl.BlockSpec((tm, KA), lambda i, j: (i, 0)),
                  pl.BlockSpec((tm, KB), lambda i, j: (i, 0)),
                  pl.BlockSpec((KA, tn), lambda i, j: (0, j)),
                  pl.BlockSpec((KB, tn), lambda i, j: (0, j)),
                  pl.BlockSpec((tm, tn), lambda i, j: (i, ca + j)),
                  pl.BlockSpec((tm, tn), lambda i, j: (i, cb + j))],
        out_specs=pl.BlockSpec((tm, tn), lambda i, j: (i, j)),
        compiler_params=_cparams("parallel", "arbitrary"),
        name="gated_mix",
    )(o_nsa, o_gla, wa, wb, proj2d, proj2d)


def _out_kernel(a_ref, w_ref, x_ref, o_ref):
    o_ref[...] = x_ref[...] + _dot(a_ref[...], w_ref[...])


def out_proj(mixed, w, x):
    T, K = mixed.shape
    N = w.shape[1]
    tm, tn = min(OUT_TM, T), min(OUT_TN, N)
    return pl.pallas_call(
        _out_kernel,
        out_shape=jax.ShapeDtypeStruct((T, N), F32),
        grid=(T // tm, N // tn),
        in_specs=[pl.BlockSpec((tm, K), lambda i, j: (i, 0)),
                  pl.BlockSpec((K, tn), lambda i, j: (0, j)),
                  pl.BlockSpec((tm, tn), lambda i, j: (i, j))],
        out_specs=pl.BlockSpec((tm, tn), lambda i, j: (i, j)),
        compiler_params=_cparams("parallel", "arbitrary"),
        name="out_proj",
    )(mixed, w, x)


def _router_kernel(h_ref, g_ref, w_ref, b_ref, id_ref, wt_ref, u_ref):
    x = h_ref[...]
    u = x * lax.rsqrt(jnp.mean(x * x, axis=-1, keepdims=True) + NORM_EPS) * g_ref[...]
    u_ref[...] = u.astype(u_ref.dtype)
    logit = _dot(u, w_ref[...], precision=lax.Precision.HIGHEST) + b_ref[...]
    lane = lax.broadcasted_iota(I32, logit.shape, 1)
    big = jnp.int32(1 << 20)
    gmask = lane < N_GROUPS
    gl = jnp.where(gmask, logit, NEG)
    gmax = jnp.max(gl, axis=-1, keepdims=True)
    g_star = jnp.min(jnp.where(gmask & (gl == gmax), lane, big), axis=-1, keepdims=True)
    p_group = 1.0 / jnp.sum(jnp.where(gmask, jnp.exp(gl - gmax), 0.0), axis=-1, keepdims=True)
    e_lo = N_GROUPS + EXPERTS_PER_GROUP * g_star
    emask = (lane >= e_lo) & (lane < e_lo + EXPERTS_PER_GROUP)
    el = jnp.where(emask, logit, NEG)
    m1 = jnp.max(el, axis=-1, keepdims=True)
    i1 = jnp.min(jnp.where(emask & (el == m1), lane, big), axis=-1, keepdims=True)
    emask2 = emask & (lane != i1)
    el2 = jnp.where(emask2, logit, NEG)
    m2 = jnp.max(el2, axis=-1, keepdims=True)
    i2 = jnp.min(jnp.where(emask2 & (el2 == m2), lane, big), axis=-1, keepdims=True)
    e2 = jnp.exp(m2 - m1)
    w1 = p_group / (1.0 + e2)
    w2 = p_group * e2 / (1.0 + e2)
    id_ref[...] = jnp.where(lane == 0, i1 - N_GROUPS, jnp.where(lane == 1, i2 - N_GROUPS, 0))
    wt_ref[...] = jnp.where(lane == 0, w1, jnp.where(lane == 1, w2, 0.0))


def router(h, g, w_r, b_r):
    T, D = h.shape
    tm = min(ROUTE_TM, T)
    return pl.pallas_call(
        _router_kernel,
        out_shape=(jax.ShapeDtypeStruct((T, LANES), I32), jax.ShapeDtypeStruct((T, LANES), F32),
                   jax.ShapeDtypeStruct((T, D), BF16)),
        grid=(T // tm,),
        in_specs=[pl.BlockSpec((tm, D), lambda i: (i, 0)),
                  pl.BlockSpec((1, D), lambda i: (0, 0)),
                  pl.BlockSpec((D, LANES), lambda i: (0, 0)),
                  pl.BlockSpec((1, LANES), lambda i: (0, 0))],
        out_specs=(pl.BlockSpec((tm, LANES), lambda i: (i, 0)), pl.BlockSpec((tm, LANES), lambda i: (i, 0)),
                   pl.BlockSpec((tm, D), lambda i: (i, 0))),
        compiler_params=_cparams("parallel"),
        name="moe_router",
    )(h, g.reshape(1, D), w_r, b_r)


def _row_copy(src_hbm, dst_ref, sem, src_row, dst_row):
    return pltpu.make_async_copy(src_hbm.at[pl.ds(src_row, 1)], dst_ref.at[pl.ds(dst_row, 1)], sem)


def _dispatch_rows_kernel(dest_ref, src_hbm, init_hbm, out_hbm, sem, *, pairs):
    base = pl.program_id(0) * pairs

    def start(p, c):
        _row_copy(src_hbm, out_hbm, sem, (base + p) // 2, dest_ref[0, 0, p]).start()
        return c

    def wait(p, c):
        _row_copy(src_hbm, out_hbm, sem, 0, 0).wait()
        return c

    lax.fori_loop(0, pairs, start, 0)
    lax.fori_loop(0, pairs, wait, 0)


def dispatch_rows(u3, dest, n_rows):
    T = u3.shape[0]
    pairs = min(GATHER_PAIRS, 2 * T)
    assert (2 * T) % pairs == 0
    out_shape = jax.ShapeDtypeStruct((n_rows,) + u3.shape[1:], u3.dtype)
    return pl.pallas_call(
        functools.partial(_dispatch_rows_kernel, pairs=pairs),
        out_shape=out_shape,
        grid=(2 * T // pairs,),
        in_specs=[pl.BlockSpec((1, 1, pairs), lambda i: (i, 0, 0), memory_space=pltpu.SMEM),
                  pl.BlockSpec(memory_space=pl.ANY),
                  pl.BlockSpec(memory_space=pl.ANY)],
        out_specs=pl.BlockSpec(memory_space=pl.ANY),
        scratch_shapes=[pltpu.SemaphoreType.DMA],
        input_output_aliases={2: 0},
        compiler_params=_cparams("arbitrary"),
        name="moe_dispatch_rows",
    )(dest.reshape(2 * T // pairs, 1, pairs), u3, jnp.zeros(out_shape.shape, out_shape.dtype))


def _ffn_up_kernel(te_ref, new_ref, nv_ref, x_ref, wg_ref, wu_ref, o_ref, wg_bf, wu_bf):
    i = pl.program_id(1)

    @pl.when(new_ref[i] == 1)
    def _():
        wg_bf[...] = wg_ref[0].astype(BF16)
        wu_bf[...] = wu_ref[0].astype(BF16)

    @pl.when(i < nv_ref[0])
    def _():
        x = x_ref[...]
        a = _dot(x, wg_bf[...])
        u = _dot(x, wu_bf[...])
        o_ref[...] = (a * _sigmoid(a) * u).astype(o_ref.dtype)

    @pl.when(i >= nv_ref[0])
    def _():
        o_ref[...] = jnp.zeros_like(o_ref)


def ffn_up(x_sorted, tables, w_gate, w_up):
    NP, D = x_sorted.shape
    E, _, DE = w_gate.shape
    tm, cj = MOE_TM, min(MOE_CJ, DE)
    return pl.pallas_call(
        _ffn_up_kernel,
        out_shape=jax.ShapeDtypeStruct((NP, DE), BF16),
        grid_spec=pltpu.PrefetchScalarGridSpec(
            num_scalar_prefetch=3,
            grid=(DE // cj, NP // tm),
            in_specs=[pl.BlockSpec((tm, D), lambda j, i, te, nw, nv: (i, 0)),
                      pl.BlockSpec((1, D, cj), lambda j, i, te, nw, nv: (te[i], 0, j)),
                      pl.BlockSpec((1, D, cj), lambda j, i, te, nw, nv: (te[i], 0, j))],
            out_specs=pl.BlockSpec((tm, cj), lambda j, i, te, nw, nv: (i, j)),
            scratch_shapes=[pltpu.VMEM((D, cj), BF16), pltpu.VMEM((D, cj), BF16)]),
        compiler_params=_cparams("arbitrary", "arbitrary"),
        name="moe_ffn_up",
    )(*tables, x_sorted, w_gate, w_up)


def _ffn_down_kernel(te_ref, new_ref, nv_ref, h_ref, wd_ref, o_ref, wd_bf):
    i = pl.program_id(1)

    @pl.when(new_ref[i] == 1)
    def _():
        wd_bf[...] = wd_ref[0].astype(BF16)

    @pl.when(i < nv_ref[0])
    def _():
        o_ref[...] = _dot(h_ref[...], wd_bf[...])

    @pl.when(i >= nv_ref[0])
    def _():
        o_ref[...] = jnp.zeros_like(o_ref)


def ffn_down(h_sorted, tables, w_down):
    NP, DE = h_sorted.shape
    E, _, D = w_down.shape
    tm, cn = MOE_TM, min(MOE_CN, D)
    return pl.pallas_call(
        _ffn_down_kernel,
        out_shape=jax.ShapeDtypeStruct((NP, D), F32),
        grid_spec=pltpu.PrefetchScalarGridSpec(
            num_scalar_prefetch=3,
            grid=(D // cn, NP // tm),
            in_specs=[pl.BlockSpec((tm, DE), lambda j, i, te, nw, nv: (i, 0)),
                      pl.BlockSpec((1, DE, cn), lambda j, i, te, nw, nv: (te[i], 0, j))],
            out_specs=pl.BlockSpec((tm, cn), lambda j, i, te, nw, nv: (i, j)),
            scratch_shapes=[pltpu.VMEM((DE, cn), BF16)]),
        compiler_params=_cparams("arbitrary", "arbitrary"),
        name="moe_ffn_down",
    )(*tables, h_sorted, w_down)


def _combine_kernel(idx_ref, h_ref, wt_ref, y_hbm, g_ref, o_ref, buf, sem, *, rows):
    def start(r, c):
        _row_copy(y_hbm, buf.at[0], sem, idx_ref[0, 0, 2 * r], r).start()
        _row_copy(y_hbm, buf.at[1], sem, idx_ref[0, 0, 2 * r + 1], r).start()
        return c

    def wait(r, c):
        _row_copy(y_hbm, buf.at[0], sem, 0, r).wait()
        _row_copy(y_hbm, buf.at[1], sem, 0, r).wait()
        return c

    lax.fori_loop(0, rows, start, 0)
    lax.fori_loop(0, rows, wait, 0)
    wt = wt_ref[...]
    x = h_ref[...] + wt[:, 0:1] * buf[0] + wt[:, 1:2] * buf[1]
    o_ref[...] = x * lax.rsqrt(jnp.mean(x * x, axis=-1, keepdims=True) + NORM_EPS) * g_ref[...]


def combine_norm(h, wts, dest, y_sorted, g):
    T, D = h.shape
    rows = min(COMBINE_ROWS, T)
    return pl.pallas_call(
        functools.partial(_combine_kernel, rows=rows),
        out_shape=jax.ShapeDtypeStruct((T, D), F32),
        grid=(T // rows,),
        in_specs=[pl.BlockSpec((1, 1, 2 * rows), lambda i: (i, 0, 0), memory_space=pltpu.SMEM),
                  pl.BlockSpec((rows, D), lambda i: (i, 0)),
                  pl.BlockSpec((rows, LANES), lambda i: (i, 0)),
                  pl.BlockSpec(memory_space=pl.ANY),
                  pl.BlockSpec((1, D), lambda i: (0, 0))],
        out_specs=pl.BlockSpec((rows, D), lambda i: (i, 0)),
        scratch_shapes=[pltpu.VMEM((2, rows, D), F32), pltpu.SemaphoreType.DMA],
        compiler_params=_cparams("arbitrary"),
        name="moe_combine_norm",
    )(dest.reshape(T // rows, 1, 2 * rows), h, wts, y_sorted, g.reshape(1, D))


def _dispatch_tables(ids, tm):
    T = ids.shape[0]
    E = N_EXPERTS
    eid = ids.reshape(-1)
    onehot = (eid[:, None] == jnp.arange(E, dtype=I32)[None, :]).astype(I32)
    csum = jnp.cumsum(onehot, axis=0)
    rank = jnp.sum(csum * onehot, axis=1) - 1
    counts = csum[-1]
    padded = ((counts + tm - 1) // tm) * tm
    ends = jnp.cumsum(padded)
    dest = jnp.sum(onehot * (ends - padded)[None, :], axis=1) + rank
    n_rows = 2 * T + E * tm
    tile_start = jnp.arange(n_rows // tm, dtype=I32) * tm
    tile_expert = jnp.sum((ends[None, :] <= tile_start[:, None]).astype(I32), axis=1)
    last_used = jnp.max(jnp.where(counts > 0, jnp.arange(E, dtype=I32), 0))
    tile_expert = jnp.minimum(tile_expert, last_used)
    tile_new = jnp.concatenate([jnp.ones((1,), I32), (tile_expert[1:] != tile_expert[:-1]).astype(I32)])
    n_tiles_used = (ends[-1] // tm).reshape(1)
    return dest.astype(I32), n_rows, (tile_expert, tile_new, n_tiles_used.astype(I32))


def _forward(x, norm_mix_g, w_in, cmp_k_pos, cmp_k_w1, cmp_k_b1, cmp_k_w2, cmp_v_pos, cmp_v_w1, cmp_v_b1, cmp_v_w2,
             gla_w_alpha, gla_b_alpha, gla_norm_g, w_branch_nsa, w_branch_gla, w_out, norm_ffn_g,
             w_router_group, b_router_group, w_router_expert, b_router_expert, w_exp_gate, w_exp_up, w_exp_down,
             norm_final_g):
    B, S, D = x.shape
    T = B * S
    G, R, Dh = NSA_KV_HEADS, NSA_GROUP, NSA_HEAD_DIM
    NSA_Q, NSA_KV = NSA_HEADS * Dh, G * Dh
    GQK, GV = GLA_HEADS * GLA_DK, GLA_HEADS * GLA_DV
    h = x.reshape(T, D)
    assert w_in.shape[0] == 1, "the final norm is fused into the (single) layer's combine step"
    for l in range(1):
        w = w_in[l]
        o_ng = NSA_Q + 6 * NSA_KV
        o_gq = o_ng + 3 * NSA_HEADS
        o_ga = o_gq + 2 * GQK + 2 * GV
        o_ma = o_ga + GLA_GATE_RANK
        w_main = jnp.concatenate([w[:, :o_ng], w[:, o_gq:o_ga], w[:, o_ma:]], axis=1).astype(BF16)
        n_small = 3 * NSA_HEADS + GLA_GATE_RANK
        w_small = jnp.concatenate([w[:, o_ng:o_gq], w[:, o_ga:o_ma], jnp.zeros((D, LANES - n_small), F32)],
                                  axis=1).astype(BF16)
        c_q = 0
        c_kc = NSA_Q // LANES
        c_ks, c_vs, c_kw, c_vw = c_kc + 2 * G, c_kc + 3 * G, c_kc + 4 * G, c_kc + 5 * G
        c_gq = c_kc + 6 * G
        c_gk = c_gq + GQK // LANES
        c_gv = c_gk + GQK // LANES
        c_gr = c_gv + GV // LANES
        c_ma = c_gr + GV // LANES
        c_mb = c_ma + D // LANES

        xn = rmsnorm(h, norm_mix_g[l])
        proj = matmul(xn, w_main, BF16, PROJ_TM, PROJ_TN, "proj_main")
        small = matmul(xn, w_small, F32, PROJ_TM, LANES, "proj_small")
        proj3 = proj.reshape(B, S, -1)

        NC = S // CMP_STRIDE
        kvc_cols = proj3[:, :, c_kc * LANES:(c_kc + 2 * G) * LANES]
        seq2 = kvc_cols.reshape(B, NC, CMP_STRIDE, 2, G, Dh).transpose(3, 0, 4, 1, 2, 5).reshape(
            2, B, G, NC, CMP_STRIDE * Dh)
        pos2 = jnp.stack([cmp_k_pos[l], cmp_v_pos[l]]).reshape(2, 2, CMP_STRIDE * Dh)
        w1 = jnp.stack([cmp_k_w1[l], cmp_v_w1[l]]).astype(BF16)
        b1 = jnp.stack([cmp_k_b1[l], cmp_v_b1[l]]).reshape(2, 1, Dh)
        w2 = jnp.stack([cmp_k_w2[l], cmp_v_w2[l]]).astype(BF16)
        kvc = compress(seq2, pos2, w1, b1, w2)
        gates = small[:, :3 * NSA_HEADS].reshape(B, S, G, 3 * R).transpose(0, 2, 1, 3)
        slopes = jnp.exp2(-8.0 * jnp.arange(1, NSA_HEADS + 1, dtype=F32) / NSA_HEADS)
        o_nsa = nsa_attention(proj3, kvc, gates, slopes, col_q=c_q, col_ks=c_ks, col_vs=c_vs,
                              col_kw=c_kw, col_vw=c_vw)

        xa = small[:, 3 * NSA_HEADS:n_small].reshape(B, S, GLA_GATE_RANK)
        o_gla = gla_attention(proj3, xa, gla_w_alpha[l], gla_b_alpha[l], gla_norm_g[l],
                              col_q=c_gq, col_k=c_gk, col_v=c_gv, col_r=c_gr)

        mixed = gated_mix(o_nsa.reshape(T, -1), o_gla.reshape(T, -1), w_branch_nsa[l].astype(BF16),
                          w_branch_gla[l].astype(BF16), proj, c_ma, c_mb)
        h = out_proj(mixed, w_out[l].astype(BF16), h)

        n_r = N_GROUPS + N_EXPERTS
        w_r = jnp.concatenate([w_router_group[l], w_router_expert[l], jnp.zeros((D, LANES - n_r), F32)], axis=1)
        b_r = jnp.concatenate([b_router_group[l], b_router_expert[l], jnp.zeros((LANES - n_r,), F32)]).reshape(1, LANES)
        ids, wts, u = router(h, norm_ffn_g[l], w_r, b_r)
        dest, n_rows, tables = _dispatch_tables(ids[:, :2], MOE_TM)
        x_sorted = dispatch_rows(u.reshape(T, D // LANES, LANES), dest, n_rows).reshape(n_rows, D)
        h_sorted = ffn_up(x_sorted, tables, w_exp_gate[l], w_exp_up[l])
        y_sorted = ffn_down(h_sorted, tables, w_exp_down[l])
        h = combine_norm(h, wts, dest, y_sorted, norm_final_g)
    return h.reshape(B, S, D)


def kernel(x, norm_mix_g, w_in, cmp_k_pos, cmp_k_w1, cmp_k_b1, cmp_k_w2, cmp_v_pos, cmp_v_w1, cmp_v_b1, cmp_v_w2, gla_w_alpha, gla_b_alpha, gla_norm_g, w_branch_nsa, w_branch_gla, w_out, norm_ffn_g, w_router_group, b_router_group, w_router_expert, b_router_expert, w_exp_gate, w_exp_up, w_exp_down, norm_final_g):
    return _forward(x, norm_mix_g, w_in, cmp_k_pos, cmp_k_w1, cmp_k_b1, cmp_k_w2, cmp_v_pos, cmp_v_w1, cmp_v_b1,
                    cmp_v_w2, gla_w_alpha, gla_b_alpha, gla_norm_g, w_branch_nsa, w_branch_gla, w_out, norm_ffn_g,
                    w_router_group, b_router_group, w_router_expert, b_router_expert, w_exp_gate, w_exp_up,
                    w_exp_down, norm_final_g)
```

```python
import functools

import jax
import jax.numpy as jnp
from jax import lax
from jax.experimental import pallas as pl
from jax.experimental.pallas import tpu as pltpu

F32 = jnp.float32
BF16 = jnp.bfloat16
I32 = jnp.int32

NSA_HEAD_DIM = 128
NSA_KV_HEADS = 4
NSA_GROUP = 4
NSA_HEADS = NSA_KV_HEADS * NSA_GROUP
CMP_BLOCK = 32
CMP_STRIDE = 16
SEL_BLOCK = 64
SEL_TOPN = 16
WINDOW = 512
BIG = 1e9
GLA_HEADS = 16
GLA_DK = 64
GLA_DV = 128
GLA_GATE_RANK = 16
GLA_TAU = 16.0
N_GROUPS = 4
EXPERTS_PER_GROUP = 8
N_EXPERTS = N_GROUPS * EXPERTS_PER_GROUP
NORM_EPS = 1e-6

LANES = 128
VMEM_LIMIT = 56 * 1024 * 1024
NEG = -1e30

NORM_ROWS = 512
PROJ_TM, PROJ_TN = 1024, 512
NSA_TQ, NSA_TK = 128, 256
GLA_TS, GLA_C, GLA_SUB = 512, 64, 16
MIX_TM, MIX_TN = 512, 1024
OUT_TM, OUT_TN = 512, 1024
ROUTE_TM = 256
MOE_TM = 512
MOE_CJ = 512
MOE_CN = 2048
GATHER_ROWS = 512
COMBINE_ROWS = 256


def _cparams(*sem):
    return pltpu.CompilerParams(dimension_semantics=sem, vmem_limit_bytes=VMEM_LIMIT)


def _dot(a, b, **kw):
    return jnp.dot(a, b, preferred_element_type=F32, **kw)


def _dot_nt(a, b, **kw):
    return lax.dot_general(a, b, (((1,), (1,)), ((), ())), preferred_element_type=F32, **kw)


def _dot_tn(a, b, **kw):
    return lax.dot_general(a, b, (((0,), (0,)), ((), ())), preferred_element_type=F32, **kw)


def _sigmoid(x):
    return 1.0 / (1.0 + jnp.exp(-x))


def _masked_softmax(s, mask):
    sm = jnp.where(mask, s, NEG)
    m = jnp.max(sm, axis=-1, keepdims=True)
    e = jnp.where(mask, jnp.exp(sm - m), 0.0)
    return e * (1.0 / jnp.maximum(jnp.sum(e, axis=-1, keepdims=True), 1e-30))


def _rmsnorm_kernel(x_ref, g_ref, o_ref):
    x = x_ref[...]
    ms = jnp.mean(x * x, axis=-1, keepdims=True)
    o_ref[...] = (x * lax.rsqrt(ms + NORM_EPS) * g_ref[...]).astype(o_ref.dtype)


def rmsnorm(x, g, out_dtype=BF16, rows=NORM_ROWS):
    T, D = x.shape
    rows = min(rows, T)
    return pl.pallas_call(
        _rmsnorm_kernel,
        out_shape=jax.ShapeDtypeStruct((T, D), out_dtype),
        grid=(T // rows,),
        in_specs=[pl.BlockSpec((rows, D), lambda i: (i, 0)),
                  pl.BlockSpec((1, D), lambda i: (0, 0))],
        out_specs=pl.BlockSpec((rows, D), lambda i: (i, 0)),
        compiler_params=_cparams("parallel"),
        name="rmsnorm",
    )(x, g.reshape(1, D))


def _mm_kernel(a_ref, b_ref, o_ref):
    o_ref[...] = _dot(a_ref[...], b_ref[...]).astype(o_ref.dtype)


def matmul(a, b, out_dtype, tm, tn, name):
    M, K = a.shape
    N = b.shape[1]
    tm, tn = min(tm, M), min(tn, N)
    return pl.pallas_call(
        _mm_kernel,
        out_shape=jax.ShapeDtypeStruct((M, N), out_dtype),
        grid=(M // tm, N // tn),
        in_specs=[pl.BlockSpec((tm, K), lambda i, j: (i, 0)),
                  pl.BlockSpec((K, tn), lambda i, j: (0, j))],
        out_specs=pl.BlockSpec((tm, tn), lambda i, j: (i, j)),
        compiler_params=_cparams("parallel", "arbitrary"),
        name=name,
    )(a, b)


def _compress_kernel(seq_ref, pos_ref, w1_ref, b1_ref, w2_ref, o_ref):
    x = seq_ref[0, 0, 0].astype(F32)
    nc = x.shape[0]
    half = x.shape[1]
    pos = pos_ref[0]
    w1 = w1_ref[0]
    u0 = _dot((x + pos[0:1, :]).astype(BF16), w1[:half, :])
    u1 = _dot((x + pos[1:2, :]).astype(BF16), w1[half:, :])
    pre = u0 + pltpu.roll(u1, nc - 1, 0) + b1_ref[0]
    h = 0.5 * pre * (1.0 + jnp.tanh(0.7978845608028654 * (pre + 0.044715 * pre * pre * pre)))
    o_ref[0, 0, 0] = _dot(h.astype(BF16), w2_ref[0]).astype(o_ref.dtype)


def compress(seq2, pos2, w1, b1, w2):
    _, B, G, NC, HW = seq2.shape
    Dh = w2.shape[-1]
    return pl.pallas_call(
        _compress_kernel,
        out_shape=jax.ShapeDtypeStruct((2, B, G, NC, Dh), BF16),
        grid=(2, B, G),
        in_specs=[pl.BlockSpec((1, 1, 1, NC, HW), lambda a, b, g: (a, b, g, 0, 0)),
                  pl.BlockSpec((1, 2, HW), lambda a, b, g: (a, 0, 0)),
                  pl.BlockSpec((1, 2 * HW, Dh), lambda a, b, g: (a, 0, 0)),
                  pl.BlockSpec((1, 1, Dh), lambda a, b, g: (a, 0, 0)),
                  pl.BlockSpec((1, Dh, Dh), lambda a, b, g: (a, 0, 0))],
        out_specs=pl.BlockSpec((1, 1, 1, NC, Dh), lambda a, b, g: (a, b, g, 0, 0)),
        compiler_params=_cparams("parallel", "parallel", "parallel"),
        name="nsa_compress",
    )(seq2, pos2, w1, b1, w2)


MASK_BIG = 2.0 ** 100


def _nsa_kernel(slopes_ref, q_ref, kc_ref, vc_ref, ks_ref, vs_ref, kw_ref, vw_ref, g_ref,
                paux_ref, saux_ref, caux_ref, mmat_ref, o_ref,
                qx_ref, sc_ref, sw_ref, sa_ref, sb_ref, score_ref, m_ref, acc_ref,
                *, TQ, TK, NC, NSEL, NTOP):
    R, Dh = NSA_GROUP, NSA_HEAD_DIM
    g = pl.program_id(1)
    q0 = pl.program_id(2) * TQ
    scale = Dh ** -0.5
    lane = lax.broadcasted_iota(I32, (1, LANES), 1)
    t1 = q0 + lax.broadcasted_iota(I32, (TQ, 1), 0)
    head = lambda r: slice(r * TQ, (r + 1) * TQ)

    def alibi_cols(r):
        hi, lo = slopes_ref[0, g * R + r], slopes_ref[1, g * R + r]
        c = jnp.where(lane == 0, 64.0 * hi, jnp.where(lane == 1, 64.0 * lo,
                                                      jnp.where(lane == 2, hi, jnp.where(lane == 3, lo, 0.0))))
        return jnp.broadcast_to(c, (TQ, LANES))

    for r in range(R):
        qx_ref[head(r), 0:Dh] = (q_ref[0, :, r * Dh:(r + 1) * Dh].astype(F32) * scale).astype(BF16)
        qx_ref[head(r), Dh:Dh + LANES] = alibi_cols(r).astype(BF16)

    def with_ones(v):
        return jnp.concatenate([v, jnp.ones(v.shape, v.dtype)], axis=1)

    def exp_pv(s, v1):
        m = jnp.broadcast_to(jnp.max(s, axis=-1, keepdims=True), (TQ, LANES))
        e = [jnp.exp(s[:, c * LANES:(c + 1) * LANES] - m) for c in range(s.shape[1] // LANES)]
        return e, _dot(jnp.concatenate(e, axis=1).astype(BF16), v1)

    WK = WINDOW + TQ
    ws = pl.multiple_of(jnp.maximum(q0 - WINDOW, 0), TQ)
    sc_ref[...] = _dot_nt(qx_ref[...], jnp.concatenate([kc_ref[0, 0, 0], caux_ref[...]], axis=1))
    sw_ref[...] = _dot_nt(qx_ref[...], jnp.concatenate([kw_ref[0, pl.ds(ws, WK), :], paux_ref[pl.ds(ws, WK), :]],
                                                       axis=1))

    cmp_end = lax.broadcasted_iota(I32, (1, NC), 1) * CMP_STRIDE + (CMP_BLOCK - 1)
    cmp_bias = jnp.where(cmp_end <= t1, 0.0, NEG)
    row_ok = q0 + lax.broadcasted_iota(I32, (TQ, LANES), 0) >= CMP_BLOCK - 1
    vc1 = with_ones(vc_ref[0, 0, 0])
    o_cmp = []
    imp = None
    for r in range(R):
        e, o2 = exp_pv(sc_ref[head(r), :] + cmp_bias, vc1)
        inv = jnp.where(row_ok, 1.0 / jnp.maximum(o2[:, Dh:], 1e-30), 0.0)
        o_cmp.append(o2[:, :Dh] * inv)
        p = jnp.concatenate([ec * inv for ec in e], axis=1)
        imp = p if imp is None else imp + p

    vw1 = with_ones(vw_ref[0, pl.ds(ws, WK), :])
    dw = t1 - (ws + lax.broadcasted_iota(I32, (1, WK), 1))
    win_bias = jnp.where((dw >= 0) & (dw < WINDOW), 0.0, NEG)
    o_win = []
    for r in range(R):
        _, o2 = exp_pv(sw_ref[head(r), :] + win_bias, vw1)
        o_win.append(o2[:, :Dh] * (1.0 / jnp.maximum(o2[:, Dh:], 1e-30)))

    mm = mmat_ref[...]
    i1 = imp.astype(BF16)
    rem = imp - i1.astype(F32)
    i2 = rem.astype(BF16)
    i3 = (rem - i2.astype(F32)).astype(BF16)
    imp_sel = _dot_nt(mm, i1) + _dot_nt(mm, i2) + _dot_nt(mm, i3)

    cur = (q0 + lax.broadcasted_iota(I32, (1, TQ), 1)) // SEL_BLOCK
    blk = lax.broadcasted_iota(I32, (NSEL, 1), 0)
    forced = (blk == 0) | (blk == cur) | (blk == cur - 1)
    score = jnp.where(blk <= cur, jnp.where(forced, BIG, imp_sel), -BIG)
    score_ref[...] = score

    def rank_pair(i2, cnt):
        for d in range(2):
            ii = 2 * i2 + d
            other = score_ref[pl.ds(ii, 1), :]
            beats = (other > score) | ((other == score) & (blk > ii))
            cnt = cnt + jnp.where(beats, 1.0, 0.0)
        return cnt

    n_blk = (q0 + TQ) // SEL_BLOCK
    cnt = lax.fori_loop(0, n_blk // 2, rank_pair, jnp.zeros((NSEL, TQ), F32))
    unsel = jnp.where(cnt < NTOP, 0.0, -1.0).astype(BF16)
    place = jnp.where(lax.broadcasted_iota(I32, (NSEL, LANES), 1)
                      == lax.broadcasted_iota(I32, (NSEL, LANES), 0) + LANES // 2, 1.0, 0.0).astype(BF16)
    unsel_l = _dot_tn(unsel, place)
    for r in range(R):
        qx_ref[head(r), Dh:Dh + LANES] = (alibi_cols(r) + unsel_l).astype(BF16)

    m_ref[...] = jnp.full(m_ref.shape, NEG, F32)
    acc_ref[...] = jnp.zeros(acc_ref.shape, F32)
    pos_in_tile = lax.broadcasted_iota(I32, (1, TK), 1)

    def issue_scores(kt, s_ref):
        k0 = pl.multiple_of(kt * TK, TK)
        kk = jnp.concatenate([ks_ref[0, pl.ds(k0, TK), :], saux_ref[pl.ds(k0, TK), :]], axis=1)
        s_ref[...] = _dot_nt(qx_ref[...], kk)

    def consume_scores(kt, s_ref, causal):
        k0 = pl.multiple_of(kt * TK, TK)
        vv1 = with_ones(vs_ref[0, pl.ds(k0, TK), :])
        if causal:
            causal_bias = jnp.where(t1 >= k0 + pos_in_tile, 0.0, -MASK_BIG)
        for r in range(R):
            sc = s_ref[head(r), :]
            if causal:
                sc = sc + causal_bias
            m_old = m_ref[r]
            m_new = jnp.maximum(m_old, jnp.broadcast_to(jnp.max(sc, axis=-1, keepdims=True), (TQ, LANES)))
            alpha = jnp.exp(m_old - m_new)
            p = jnp.concatenate([jnp.exp(sc[:, c * LANES:(c + 1) * LANES] - m_new) for c in range(TK // LANES)],
                                axis=1).astype(BF16)
            m_ref[r] = m_new
            acc_ref[r] = jnp.concatenate([alpha, alpha], axis=1) * acc_ref[r] + _dot(p, vv1)

    def sel_pair(k, carry):
        issue_scores(2 * k + 1, sb_ref)
        consume_scores(2 * k, sa_ref, False)
        issue_scores(2 * k + 2, sa_ref)
        consume_scores(2 * k + 1, sb_ref, False)
        return carry

    n_below = q0 // TK
    issue_scores(0, sa_ref)
    lax.fori_loop(0, n_below // 2, sel_pair, 0)

    @pl.when(n_below % 2 == 1)
    def _():
        issue_scores(n_below, sb_ref)
        consume_scores(n_below - 1, sa_ref, False)
        consume_scores(n_below, sb_ref, True)

    @pl.when(n_below % 2 == 0)
    def _():
        consume_scores(n_below, sa_ref, True)

    gt = _sigmoid(g_ref[0, 0])
    for r in range(R):
        o_sel = acc_ref[r, :, 0:Dh] * (1.0 / jnp.maximum(acc_ref[r, :, Dh:2 * Dh], 1e-30))
        o = (gt[:, 3 * r:3 * r + 1] * o_cmp[r] + gt[:, 3 * r + 1:3 * r + 2] * o_sel
             + gt[:, 3 * r + 2:3 * r + 3] * o_win[r])
        o_ref[0, :, r * Dh:(r + 1) * Dh] = o.astype(o_ref.dtype)


def _nsa_constants(S, NC, NSEL):
    def pos_cols(pos):
        col = jnp.arange(LANES)[None, :]
        hi, lo = (pos // 64)[:, None], (pos % 64)[:, None]
        return jnp.where(col < 2, hi, jnp.where(col < 4, lo, 0)).astype(F32)

    pos = jnp.arange(S)
    paux = pos_cols(pos)
    onehot = (jnp.arange(LANES)[None, :] - LANES // 2 == (pos // SEL_BLOCK)[:, None]) & (jnp.arange(LANES)[None, :] >= LANES // 2)
    saux = paux + jnp.where(onehot, MASK_BIG, 0.0)
    caux = pos_cols(jnp.arange(NC) * CMP_STRIDE + (CMP_BLOCK - 1))
    d = jnp.arange(NC)[None, :] - 4 * jnp.arange(NSEL)[:, None]
    mmat = jnp.where((d == -1) | (d == 3), 1.0, jnp.where((d >= 0) & (d <= 2), 2.0, 0.0))
    return paux.astype(BF16), saux.astype(BF16), caux.astype(BF16), mmat.astype(BF16)


def nsa_attention(proj, kvc, gates, slopes, *, col_q, col_ks, col_vs, col_kw, col_vw):
    B, S, _ = proj.shape
    G, R, Dh = NSA_KV_HEADS, NSA_GROUP, NSA_HEAD_DIM
    NC = kvc.shape[3]
    NSEL = S // SEL_BLOCK
    TQ, TK = min(NSA_TQ, S), min(NSA_TK, S)
    assert S % TQ == 0 and TK % TQ == 0 and S % TK == 0 and S >= WINDOW + TQ and NC == S // CMP_STRIDE
    assert NSEL <= LANES // 2 and S // 64 <= 256 and TQ % (2 * SEL_BLOCK) == 0
    s_hi = slopes.astype(BF16).astype(F32)
    s_lo = (slopes - s_hi).astype(BF16).astype(F32)
    paux, saux, caux, mmat = _nsa_constants(S, NC, NSEL)
    kern = functools.partial(_nsa_kernel, TQ=TQ, TK=TK, NC=NC, NSEL=NSEL, NTOP=min(SEL_TOPN, NSEL))
    kv_spec = lambda col: pl.BlockSpec((1, S, Dh), lambda b, g, i: (b, 0, col + g))
    whole = lambda a: pl.BlockSpec(a.shape, lambda b, g, i: (0, 0))
    return pl.pallas_call(
        kern,
        out_shape=jax.ShapeDtypeStruct((B, S, G * R * Dh), BF16),
        grid=(B, G, S // TQ),
        in_specs=[pl.BlockSpec(memory_space=pltpu.SMEM),
                  pl.BlockSpec((1, TQ, R * Dh), lambda b, g, i: (b, i, col_q // R + g)),
                  pl.BlockSpec((1, 1, 1, NC, Dh), lambda b, g, i: (0, b, g, 0, 0)),
                  pl.BlockSpec((1, 1, 1, NC, Dh), lambda b, g, i: (1, b, g, 0, 0)),
                  kv_spec(col_ks), kv_spec(col_vs), kv_spec(col_kw), kv_spec(col_vw),
                  pl.BlockSpec((1, 1, TQ, 3 * R), lambda b, g, i: (b, g, i, 0)),
                  whole(paux), whole(saux), whole(caux), whole(mmat)],
        out_specs=pl.BlockSpec((1, TQ, R * Dh), lambda b, g, i: (b, i, g)),
        scratch_shapes=[pltpu.VMEM((R * TQ, Dh + LANES), BF16), pltpu.VMEM((R * TQ, NC), F32),
                        pltpu.VMEM((R * TQ, WINDOW + TQ), F32), pltpu.VMEM((R * TQ, TK), F32),
                        pltpu.VMEM((R * TQ, TK), F32),
                        pltpu.VMEM((NSEL, TQ), F32), pltpu.VMEM((R, TQ, LANES), F32),
                        pltpu.VMEM((R, TQ, 2 * Dh), F32)],
        compiler_params=_cparams("parallel", "parallel", "arbitrary"),
        name="nsa_attention",
    )(jnp.stack([s_hi, s_lo]), proj, kvc, kvc, proj, proj, proj, proj, gates, paux, saux, caux, mmat)


def _gla_kernel(q_ref, k_ref, v_ref, r_ref, xa_ref, wa_ref, ba_ref, ng_ref, o_ref, st_ref, la_ref, *, TS, C):
    DK2, DV2 = 2 * GLA_DK, 2 * GLA_DV
    SUB = GLA_SUB
    NSUB = C // SUB

    @pl.when(pl.program_id(2) == 0)
    def _():
        st_ref[...] = jnp.zeros_like(st_ref)

    z = _dot(xa_ref[0], wa_ref[...], precision=lax.Precision.HIGHEST) + ba_ref[...]
    la_ref[...] = (jnp.minimum(z, 0.0) - jnp.log(1.0 + jnp.exp(-jnp.abs(z)))) * (1.0 / GLA_TAU)

    lane = lax.broadcasted_iota(I32, (1, DK2), 1)
    head_a = lane < GLA_DK
    row = lax.broadcasted_iota(I32, (C, 1), 0)
    col = lax.broadcasted_iota(I32, (1, C), 1)
    tri = jnp.where(col <= row, 1.0, 0.0).astype(BF16)
    mask1 = (row // 32 == 1) & (col // 32 == 0)
    mask2 = (row // 32 == col // 32) & ((row // 16) % 2 == 1) & ((col // 16) % 2 == 0)
    kv_same = (lax.broadcasted_iota(I32, (DK2, DV2), 0) < GLA_DK) == (lax.broadcasted_iota(I32, (DK2, DV2), 1) < GLA_DV)
    hsel = jnp.where(kv_same, 1.0, 0.0).astype(BF16)
    vk_same = (lax.broadcasted_iota(I32, (DV2, DK2), 0) < GLA_DV) == (lax.broadcasted_iota(I32, (DV2, DK2), 1) < GLA_DK)
    t_in = row % SUB

    def chunk(c, carry):
        r0 = pl.multiple_of(c * C, C)
        q = q_ref[0, pl.ds(r0, C), :].astype(F32) * (GLA_DK ** -0.5)
        k = k_ref[0, pl.ds(r0, C), :].astype(F32)
        vb = v_ref[0, pl.ds(r0, C), :]
        v = vb.astype(F32)
        la = la_ref[pl.ds(r0, C), :]
        l1 = la.astype(BF16)
        lr = la - l1.astype(F32)
        l2 = lr.astype(BF16)
        l3 = (lr - l2.astype(F32)).astype(BF16)
        b = _dot(tri, l1) + _dot(tri, l2) + _dot(tri, l3)

        r1 = b[31:32, :]
        r2 = jnp.where(row < 32, b[15:16, :], b[47:48, :])
        q1 = q * jnp.exp(jnp.minimum(b - r1, 0.0))
        k1 = (k * jnp.exp(jnp.minimum(r1 - b, 0.0))).astype(BF16)
        q2 = q * jnp.exp(jnp.minimum(b - r2, 0.0))
        k2 = (k * jnp.exp(jnp.minimum(r2 - b, 0.0))).astype(BF16)

        def scores(hm):
            a1 = _dot_nt(jnp.where(hm, q1, 0.0).astype(BF16), k1)
            a2 = _dot_nt(jnp.where(hm, q2, 0.0).astype(BF16), k2)
            return (jnp.where(mask1, a1, 0.0) + jnp.where(mask2, a2, 0.0)).astype(BF16)

        o = jnp.concatenate([_dot(scores(head_a), vb[:, :GLA_DV]),
                             _dot(scores(jnp.logical_not(head_a)), vb[:, GLA_DV:])], axis=1)

        b3 = b.reshape(NSUB, SUB, DK2)
        k3 = k.reshape(NSUB, SUB, DK2)
        v3 = v.reshape(NSUB, SUB, DV2)
        prods = []
        for sp in range(SUB):
            bs = jnp.broadcast_to(b3[:, sp:sp + 1, :], (NSUB, SUB, DK2)).reshape(C, DK2)
            ks = jnp.broadcast_to(k3[:, sp:sp + 1, :], (NSUB, SUB, DK2)).reshape(C, DK2)
            prods.append((q * ks * jnp.exp(jnp.minimum(b - bs, 0.0))).astype(BF16))
        ac = _dot(jnp.concatenate(prods, axis=0), hsel)
        for sp in range(SUB):
            vs = jnp.broadcast_to(v3[:, sp:sp + 1, :], (NSUB, SUB, DV2)).reshape(C, DV2)
            o = o + jnp.where(t_in >= sp, ac[sp * C:(sp + 1) * C] * vs, 0.0)

        st = st_ref[...]
        o = o + _dot_nt((q * jnp.exp(b)).astype(BF16), st.astype(BF16))
        bl = b[C - 1:C, :]
        kd = (k * jnp.exp(bl - b)).astype(BF16)
        ds = jnp.where(vk_same, _dot_tn(vb, kd), 0.0)
        st_ref[...] = st * jnp.exp(bl) + ds

        def ln(x):
            mu = jnp.mean(x, axis=-1, keepdims=True)
            xc = x - mu
            return xc * lax.rsqrt(jnp.mean(xc * xc, axis=-1, keepdims=True) + NORM_EPS)

        on = jnp.concatenate([ln(o[:, :GLA_DV]), ln(o[:, GLA_DV:])], axis=1) * ng_ref[...]
        rr = r_ref[0, pl.ds(r0, C), :].astype(F32)
        o_ref[0, pl.ds(r0, C), :] = (on * (rr * _sigmoid(rr))).astype(o_ref.dtype)
        return carry

    lax.fori_loop(0, TS // C, chunk, 0, unroll=8)


def gla_attention(proj, xa, w_alpha, b_alpha, norm_g, *, col_q, col_k, col_v, col_r):
    B, S, _ = proj.shape
    HP = GLA_HEADS // 2
    DK2, DV2 = 2 * GLA_DK, 2 * GLA_DV
    TS, C = min(GLA_TS, S), GLA_C
    assert S % TS == 0 and TS % C == 0 and C == 64
    kern = functools.partial(_gla_kernel, TS=TS, C=C)
    return pl.pallas_call(
        kern,
        out_shape=jax.ShapeDtypeStruct((B, S, GLA_HEADS * GLA_DV), BF16),
        grid=(B, HP, S // TS),
        in_specs=[pl.BlockSpec((1, TS, DK2), lambda b, p, s: (b, s, col_q + p)),
                  pl.BlockSpec((1, TS, DK2), lambda b, p, s: (b, s, col_k + p)),
                  pl.BlockSpec((1, TS, DV2), lambda b, p, s: (b, s, col_v // 2 + p)),
                  pl.BlockSpec((1, TS, DV2), lambda b, p, s: (b, s, col_r // 2 + p)),
                  pl.BlockSpec((1, TS, GLA_GATE_RANK), lambda b, p, s: (b, s, 0)),
                  pl.BlockSpec((GLA_GATE_RANK, DK2), lambda b, p, s: (0, p)),
                  pl.BlockSpec((1, DK2), lambda b, p, s: (0, p)),
                  pl.BlockSpec((1, DV2), lambda b, p, s: (0, p))],
        out_specs=pl.BlockSpec((1, TS, DV2), lambda b, p, s: (b, s, p)),
        scratch_shapes=[pltpu.VMEM((DV2, DK2), F32), pltpu.VMEM((TS, DK2), F32)],
        compiler_params=_cparams("parallel", "parallel", "arbitrary"),
        name="gla_attention",
    )(proj, proj, proj, proj, xa, w_alpha, b_alpha.reshape(1, -1), norm_g.reshape(1, -1))


def _mix_kernel(oa_ref, ob_ref, wa_ref, wb_ref, ma_ref, mb_ref, o_ref):
    ya = _dot(oa_ref[...], wa_ref[...])
    yb = _dot(ob_ref[...], wb_ref[...])
    o_ref[...] = (_sigmoid(ma_ref[...].astype(F32)) * ya + _sigmoid(mb_ref[...].astype(F32)) * yb).astype(o_ref.dtype)


def gated_mix(o_nsa, o_gla, wa, wb, proj2d, col_ma, col_mb):
    T, KA = o_nsa.shape
    KB = o_gla.shape[1]
    N = wa.shape[1]
    tm, tn = min(MIX_TM, T), min(MIX_TN, N)
    ca, cb = col_ma * LANES // tn, col_mb * LANES // tn
    assert (col_ma * LANES) % tn == 0 and (col_mb * LANES) % tn == 0
    return pl.pallas_call(
        _mix_kernel,
        out_shape=jax.ShapeDtypeStruct((T, N), BF16),
        grid=(T // tm, N // tn),
        in_specs=[pl.BlockSpec((tm, KA), lambda i, j: (i, 0)),
                  pl.BlockSpec((tm, KB), lambda i, j: (i, 0)),
                  pl.BlockSpec((KA, tn), lambda i, j: (0, j)),
                  pl.BlockSpec((KB, tn), lambda i, j: (0, j)),
                  pl.BlockSpec((tm, tn), lambda i, j: (i, ca + j)),
                  pl.BlockSpec((tm, tn), lambda i, j: (i, cb + j))],
        out_specs=pl.BlockSpec((tm, tn), lambda i, j: (i, j)),
        compiler_params=_cparams("parallel", "arbitrary"),
        name="gated_mix",
    )(o_nsa, o_gla, wa, wb, proj2d, proj2d)


def _out_kernel(a_ref, w_ref, x_ref, o_ref):
    o_ref[...] = x_ref[...] + _dot(a_ref[...], w_ref[...])


def out_proj(mixed, w, x):
    T, K = mixed.shape
    N = w.shape[1]
    tm, tn = min(OUT_TM, T), min(OUT_TN, N)
    return pl.pallas_call(
        _out_kernel,
        out_shape=jax.ShapeDtypeStruct((T, N), F32),
        grid=(T // tm, N // tn),
        in_specs=[pl.BlockSpec((tm, K), lambda i, j: (i, 0)),
                  pl.BlockSpec((K, tn), lambda i, j: (0, j)),
                  pl.BlockSpec((tm, tn), lambda i, j: (i, j))],
        out_specs=pl.BlockSpec((tm, tn), lambda i, j: (i, j)),
        compiler_params=_cparams("parallel", "arbitrary"),
        name="out_proj",
    )(mixed, w, x)


def _router_kernel(h_ref, g_ref, w_ref, b_ref, id_ref, wt_ref, u_ref):
    x = h_ref[...]
    u = x * lax.rsqrt(jnp.mean(x * x, axis=-1, keepdims=True) + NORM_EPS) * g_ref[...]
    u_ref[...] = u.astype(u_ref.dtype).reshape(u_ref.shape)
    logit = _dot(u, w_ref[...], precision=lax.Precision.HIGHEST) + b_ref[...]
    lane = lax.broadcasted_iota(I32, logit.shape, 1)
    big = jnp.int32(1 << 20)
    gmask = lane < N_GROUPS
    gl = jnp.where(gmask, logit, NEG)
    gmax = jnp.max(gl, axis=-1, keepdims=True)
    g_star = jnp.min(jnp.where(gmask & (gl == gmax), lane, big), axis=-1, keepdims=True)
    p_group = 1.0 / jnp.sum(jnp.where(gmask, jnp.exp(gl - gmax), 0.0), axis=-1, keepdims=True)
    e_lo = N_GROUPS + EXPERTS_PER_GROUP * g_star
    emask = (lane >= e_lo) & (lane < e_lo + EXPERTS_PER_GROUP)
    el = jnp.where(emask, logit, NEG)
    m1 = jnp.max(el, axis=-1, keepdims=True)
    i1 = jnp.min(jnp.where(emask & (el == m1), lane, big), axis=-1, keepdims=True)
    emask2 = emask & (lane != i1)
    el2 = jnp.where(emask2, logit, NEG)
    m2 = jnp.max(el2, axis=-1, keepdims=True)
    i2 = jnp.min(jnp.where(emask2 & (el2 == m2), lane, big), axis=-1, keepdims=True)
    e2 = jnp.exp(m2 - m1)
    w1 = p_group / (1.0 + e2)
    w2 = p_group * e2 / (1.0 + e2)
    id_ref[...] = jnp.where(lane == 0, i1 - N_GROUPS, jnp.where(lane == 1, i2 - N_GROUPS, 0))
    wt_ref[...] = jnp.where(lane == 0, w1, jnp.where(lane == 1, w2, 0.0))


def router(h, g, w_r, b_r):
    T, D = h.shape
    tm = min(ROUTE_TM, T)
    return pl.pallas_call(
        _router_kernel,
        out_shape=(jax.ShapeDtypeStruct((T, LANES), I32), jax.ShapeDtypeStruct((T, LANES), F32),
                   jax.ShapeDtypeStruct((T, D // LANES, LANES), BF16)),
        grid=(T // tm,),
        in_specs=[pl.BlockSpec((tm, D), lambda i: (i, 0)),
                  pl.BlockSpec((1, D), lambda i: (0, 0)),
                  pl.BlockSpec((D, LANES), lambda i: (0, 0)),
                  pl.BlockSpec((1, LANES), lambda i: (0, 0))],
        out_specs=(pl.BlockSpec((tm, LANES), lambda i: (i, 0)), pl.BlockSpec((tm, LANES), lambda i: (i, 0)),
                   pl.BlockSpec((tm, D // LANES, LANES), lambda i: (i, 0, 0))),
        compiler_params=_cparams("parallel"),
        name="moe_router",
    )(h, g.reshape(1, D), w_r, b_r)


def _row_copy(src_hbm, dst_ref, sem, src_row, dst_row):
    return pltpu.make_async_copy(src_hbm.at[pl.ds(src_row, 1)], dst_ref.at[pl.ds(dst_row, 1)], sem)


def _dispatch_rows_kernel(tok_ref, src_hbm, o_ref, buf, sem, *, rows):
    def start(r, c):
        _row_copy(src_hbm, buf, sem, tok_ref[0, 0, r], r).start()
        return c

    def wait(r, c):
        _row_copy(src_hbm, buf, sem, 0, r).wait()
        return c

    lax.fori_loop(0, rows, start, 0)
    lax.fori_loop(0, rows, wait, 0)
    o_ref[...] = buf[...].reshape(o_ref.shape)


def dispatch_rows(u3, row_token):
    n_rows = row_token.shape[0]
    D = u3.shape[1] * u3.shape[2]
    rows = min(GATHER_ROWS, n_rows)
    assert n_rows % rows == 0
    return pl.pallas_call(
        functools.partial(_dispatch_rows_kernel, rows=rows),
        out_shape=jax.ShapeDtypeStruct((n_rows, D), u3.dtype),
        grid=(n_rows // rows,),
        in_specs=[pl.BlockSpec((1, 1, rows), lambda i: (i, 0, 0), memory_space=pltpu.SMEM),
                  pl.BlockSpec(memory_space=pl.ANY)],
        out_specs=pl.BlockSpec((rows, D), lambda i: (i, 0)),
        scratch_shapes=[pltpu.VMEM((rows,) + u3.shape[1:], u3.dtype), pltpu.SemaphoreType.DMA],
        compiler_params=_cparams("arbitrary"),
        name="moe_dispatch_rows",
    )(row_token.reshape(n_rows // rows, 1, rows), u3)


def _ffn_up_kernel(te_ref, new_ref, nv_ref, x_ref, wg_ref, wu_ref, o_ref, wg_bf, wu_bf):
    i = pl.program_id(1)

    @pl.when(new_ref[i] == 1)
    def _():
        wg_bf[...] = wg_ref[0].astype(BF16)
        wu_bf[...] = wu_ref[0].astype(BF16)

    @pl.when(i < nv_ref[0])
    def _():
        x = x_ref[...]
        a = _dot(x, wg_bf[...])
        u = _dot(x, wu_bf[...])
        o_ref[...] = (a * _sigmoid(a) * u).astype(o_ref.dtype)

    @pl.when(i >= nv_ref[0])
    def _():
        o_ref[...] = jnp.zeros_like(o_ref)


def ffn_up(x_sorted, tables, w_gate, w_up):
    NP, D = x_sorted.shape
    E, _, DE = w_gate.shape
    tm, cj = MOE_TM, min(MOE_CJ, DE)
    return pl.pallas_call(
        _ffn_up_kernel,
        out_shape=jax.ShapeDtypeStruct((NP, DE), BF16),
        grid_spec=pltpu.PrefetchScalarGridSpec(
            num_scalar_prefetch=3,
            grid=(DE // cj, NP // tm),
            in_specs=[pl.BlockSpec((tm, D), lambda j, i, te, nw, nv: (i, 0)),
                      pl.BlockSpec((1, D, cj), lambda j, i, te, nw, nv: (te[i], 0, j)),
                      pl.BlockSpec((1, D, cj), lambda j, i, te, nw, nv: (te[i], 0, j))],
            out_specs=pl.BlockSpec((tm, cj), lambda j, i, te, nw, nv: (i, j)),
            scratch_shapes=[pltpu.VMEM((D, cj), BF16), pltpu.VMEM((D, cj), BF16)]),
        compiler_params=_cparams("arbitrary", "arbitrary"),
        name="moe_ffn_up",
    )(*tables, x_sorted, w_gate, w_up)


def _ffn_down_kernel(te_ref, new_ref, nv_ref, h_ref, wd_ref, o_ref, wd_bf):
    i = pl.program_id(1)

    @pl.when(new_ref[i] == 1)
    def _():
        wd_bf[...] = wd_ref[0].astype(BF16)

    @pl.when(i < nv_ref[0])
    def _():
        o_ref[...] = _dot(h_ref[...], wd_bf[...]).astype(o_ref.dtype).reshape(o_ref.shape)

    @pl.when(i >= nv_ref[0])
    def _():
        o_ref[...] = jnp.zeros_like(o_ref)


def ffn_down(h_sorted, tables, w_down):
    NP, DE = h_sorted.shape
    E, _, D = w_down.shape
    tm, cn = MOE_TM, min(MOE_CN, D)
    return pl.pallas_call(
        _ffn_down_kernel,
        out_shape=jax.ShapeDtypeStruct((NP, D // LANES, LANES), BF16),
        grid_spec=pltpu.PrefetchScalarGridSpec(
            num_scalar_prefetch=3,
            grid=(D // cn, NP // tm),
            in_specs=[pl.BlockSpec((tm, DE), lambda j, i, te, nw, nv: (i, 0)),
                      pl.BlockSpec((1, DE, cn), lambda j, i, te, nw, nv: (te[i], 0, j))],
            out_specs=pl.BlockSpec((tm, cn // LANES, LANES), lambda j, i, te, nw, nv: (i, j, 0)),
            scratch_shapes=[pltpu.VMEM((DE, cn), BF16)]),
        compiler_params=_cparams("arbitrary", "arbitrary"),
        name="moe_ffn_down",
    )(*tables, h_sorted, w_down)


def _combine_kernel(idx_ref, nxt_ref, h_ref, wt_ref, y_hbm, g_ref, o_ref, buf, sem, *, rows):
    i = pl.program_id(0)
    slot = i % 2

    def issue(table, s):
        def body(r, c):
            _row_copy(y_hbm, buf.at[s, 0], sem.at[s], table[0, 0, 2 * r], r).start()
            _row_copy(y_hbm, buf.at[s, 1], sem.at[s], table[0, 0, 2 * r + 1], r).start()
            return c
        lax.fori_loop(0, rows, body, 0)

    @pl.when(i == 0)
    def _():
        issue(idx_ref, 0)

    @pl.when(i + 1 < pl.num_programs(0))
    def _():
        issue(nxt_ref, 1 - slot)

    def wait(r, c):
        _row_copy(y_hbm, buf.at[slot, 0], sem.at[slot], 0, r).wait()
        _row_copy(y_hbm, buf.at[slot, 1], sem.at[slot], 0, r).wait()
        return c

    lax.fori_loop(0, rows, wait, 0)
    wt = wt_ref[...]
    y0 = buf[slot, 0].reshape(h_ref.shape).astype(F32)
    y1 = buf[slot, 1].reshape(h_ref.shape).astype(F32)
    x = h_ref[...] + wt[:, 0:1] * y0 + wt[:, 1:2] * y1
    o_ref[...] = x * lax.rsqrt(jnp.mean(x * x, axis=-1, keepdims=True) + NORM_EPS) * g_ref[...]


def combine_norm(h, wts, dest, y3, g):
    T, D = h.shape
    rows = min(COMBINE_ROWS, T)
    n = T // rows
    table = dest.reshape(n, 1, 2 * rows)
    return pl.pallas_call(
        functools.partial(_combine_kernel, rows=rows),
        out_shape=jax.ShapeDtypeStruct((T, D), F32),
        grid=(n,),
        in_specs=[pl.BlockSpec((1, 1, 2 * rows), lambda i: (i, 0, 0), memory_space=pltpu.SMEM),
                  pl.BlockSpec((1, 1, 2 * rows), lambda i: (jnp.minimum(i + 1, n - 1), 0, 0),
                               memory_space=pltpu.SMEM),
                  pl.BlockSpec((rows, D), lambda i: (i, 0)),
                  pl.BlockSpec((rows, LANES), lambda i: (i, 0)),
                  pl.BlockSpec(memory_space=pl.ANY),
                  pl.BlockSpec((1, D), lambda i: (0, 0))],
        out_specs=pl.BlockSpec((rows, D), lambda i: (i, 0)),
        scratch_shapes=[pltpu.VMEM((2, 2, rows) + y3.shape[1:], y3.dtype), pltpu.SemaphoreType.DMA((2,))],
        compiler_params=_cparams("arbitrary"),
        name="moe_combine_norm",
    )(table, table, h, wts, y3, g.reshape(1, D))


def _dispatch_tables(ids, tm):
    T = ids.shape[0]
    E = N_EXPERTS
    eid = ids.reshape(-1)
    onehot = (eid[:, None] == jnp.arange(E, dtype=I32)[None, :]).astype(I32)
    csum = jnp.cumsum(onehot, axis=0)
    rank = jnp.sum(csum * onehot, axis=1) - 1
    counts = csum[-1]
    padded = ((counts + tm - 1) // tm) * tm
    ends = jnp.cumsum(padded)
    dest = jnp.sum(onehot * (ends - padded)[None, :], axis=1) + rank
    n_rows = 2 * T + E * tm
    tile_start = jnp.arange(n_rows // tm, dtype=I32) * tm
    tile_expert = jnp.sum((ends[None, :] <= tile_start[:, None]).astype(I32), axis=1)
    last_used = jnp.max(jnp.where(counts > 0, jnp.arange(E, dtype=I32), 0))
    tile_expert = jnp.minimum(tile_expert, last_used)
    tile_new = jnp.concatenate([jnp.ones((1,), I32), (tile_expert[1:] != tile_expert[:-1]).astype(I32)])
    n_tiles_used = (ends[-1] // tm).reshape(1)
    row_token = jnp.zeros((n_rows,), I32).at[dest].set(jnp.arange(2 * T, dtype=I32) // 2)
    return dest.astype(I32), row_token, (tile_expert, tile_new, n_tiles_used.astype(I32))


def _forward(x, norm_mix_g, w_in, cmp_k_pos, cmp_k_w1, cmp_k_b1, cmp_k_w2, cmp_v_pos, cmp_v_w1, cmp_v_b1, cmp_v_w2,
             gla_w_alpha, gla_b_alpha, gla_norm_g, w_branch_nsa, w_branch_gla, w_out, norm_ffn_g,
             w_router_group, b_router_group, w_router_expert, b_router_expert, w_exp_gate, w_exp_up, w_exp_down,
             norm_final_g):
    B, S, D = x.shape
    T = B * S
    G, R, Dh = NSA_KV_HEADS, NSA_GROUP, NSA_HEAD_DIM
    NSA_Q, NSA_KV = NSA_HEADS * Dh, G * Dh
    GQK, GV = GLA_HEADS * GLA_DK, GLA_HEADS * GLA_DV
    h = x.reshape(T, D)
    assert w_in.shape[0] == 1, "the final norm is fused into the (single) layer's combine step"
    for l in range(1):
        w = w_in[l]
        o_ng = NSA_Q + 6 * NSA_KV
        o_gq = o_ng + 3 * NSA_HEADS
        o_ga = o_gq + 2 * GQK + 2 * GV
        o_ma = o_ga + GLA_GATE_RANK
        w_main = jnp.concatenate([w[:, :o_ng], w[:, o_gq:o_ga], w[:, o_ma:]], axis=1).astype(BF16)
        n_small = 3 * NSA_HEADS + GLA_GATE_RANK
        w_small = jnp.concatenate([w[:, o_ng:o_gq], w[:, o_ga:o_ma], jnp.zeros((D, LANES - n_small), F32)],
                                  axis=1).astype(BF16)
        c_q = 0
        c_kc = NSA_Q // LANES
        c_ks, c_vs, c_kw, c_vw = c_kc + 2 * G, c_kc + 3 * G, c_kc + 4 * G, c_kc + 5 * G
        c_gq = c_kc + 6 * G
        c_gk = c_gq + GQK // LANES
        c_gv = c_gk + GQK // LANES
        c_gr = c_gv + GV // LANES
        c_ma = c_gr + GV // LANES
        c_mb = c_ma + D // LANES

        xn = rmsnorm(h, norm_mix_g[l])
        proj = matmul(xn, w_main, BF16, PROJ_TM, PROJ_TN, "proj_main")
        small = matmul(xn, w_small, F32, PROJ_TM, LANES, "proj_small")
        proj3 = proj.reshape(B, S, -1)

        NC = S // CMP_STRIDE
        kvc_cols = proj3[:, :, c_kc * LANES:(c_kc + 2 * G) * LANES]
        seq2 = kvc_cols.reshape(B, NC, CMP_STRIDE, 2, G, Dh).transpose(3, 0, 4, 1, 2, 5).reshape(
            2, B, G, NC, CMP_STRIDE * Dh)
        pos2 = jnp.stack([cmp_k_pos[l], cmp_v_pos[l]]).reshape(2, 2, CMP_STRIDE * Dh)
        w1 = jnp.stack([cmp_k_w1[l], cmp_v_w1[l]]).astype(BF16)
        b1 = jnp.stack([cmp_k_b1[l], cmp_v_b1[l]]).reshape(2, 1, Dh)
        w2 = jnp.stack([cmp_k_w2[l], cmp_v_w2[l]]).astype(BF16)
        kvc = compress(seq2, pos2, w1, b1, w2)
        gates = small[:, :3 * NSA_HEADS].reshape(B, S, G, 3 * R).transpose(0, 2, 1, 3)
        slopes = jnp.exp2(-8.0 * jnp.arange(1, NSA_HEADS + 1, dtype=F32) / NSA_HEADS)
        o_nsa = nsa_attention(proj3, kvc, gates, slopes, col_q=c_q, col_ks=c_ks, col_vs=c_vs,
                              col_kw=c_kw, col_vw=c_vw)

        xa = small[:, 3 * NSA_HEADS:n_small].reshape(B, S, GLA_GATE_RANK)
        o_gla = gla_attention(proj3, xa, gla_w_alpha[l], gla_b_alpha[l], gla_norm_g[l],
                              col_q=c_gq, col_k=c_gk, col_v=c_gv, col_r=c_gr)

        mixed = gated_mix(o_nsa.reshape(T, -1), o_gla.reshape(T, -1), w_branch_nsa[l].astype(BF16),
                          w_branch_gla[l].astype(BF16), proj, c_ma, c_mb)
        h = out_proj(mixed, w_out[l].astype(BF16), h)

        n_r = N_GROUPS + N_EXPERTS
        w_r = jnp.concatenate([w_router_group[l], w_router_expert[l], jnp.zeros((D, LANES - n_r), F32)], axis=1)
        b_r = jnp.concatenate([b_router_group[l], b_router_expert[l], jnp.zeros((LANES - n_r,), F32)]).reshape(1, LANES)
        ids, wts, u = router(h, norm_ffn_g[l], w_r, b_r)
        dest, row_token, tables = _dispatch_tables(ids[:, :2], MOE_TM)
        x_sorted = dispatch_rows(u, row_token)
        h_sorted = ffn_up(x_sorted, tables, w_exp_gate[l], w_exp_up[l])
        y_sorted = ffn_down(h_sorted, tables, w_exp_down[l])
        h = combine_norm(h, wts, dest, y_sorted, norm_final_g)
    return h.reshape(B, S, D)


def kernel(x, norm_mix_g, w_in, cmp_k_pos, cmp_k_w1, cmp_k_b1, cmp_k_w2, cmp_v_pos, cmp_v_w1, cmp_v_b1, cmp_v_w2, gla_w_alpha, gla_b_alpha, gla_norm_g, w_branch_nsa, w_branch_gla, w_out, norm_ffn_g, w_router_group, b_router_group, w_router_expert, b_router_expert, w_exp_gate, w_exp_up, w_exp_down, norm_final_g):
    return _forward(x, norm_mix_g, w_in, cmp_k_pos, cmp_k_w1, cmp_k_b1, cmp_k_w2, cmp_v_pos, cmp_v_w1, cmp_v_b1,
                    cmp_v_w2, gla_w_alpha, gla_b_alpha, gla_norm_g, w_branch_nsa, w_branch_gla, w_out, norm_ffn_g,
                    w_router_group, b_router_group, w_router_expert, b_router_expert, w_exp_gate, w_exp_up,
                    w_exp_down, norm_final_g)
```

```python
import functools

import jax
import jax.numpy as jnp
from jax import lax
from jax.experimental import pallas as pl
from jax.experimental.pallas import tpu as pltpu

F32 = jnp.float32
BF16 = jnp.bfloat16
I32 = jnp.int32

NSA_HEAD_DIM = 128
NSA_KV_HEADS = 4
NSA_GROUP = 4
NSA_HEADS = NSA_KV_HEADS * NSA_GROUP
CMP_BLOCK = 32
CMP_STRIDE = 16
SEL_BLOCK = 64
SEL_TOPN = 16
WINDOW = 512
BIG = 1e9
GLA_HEADS = 16
GLA_DK = 64
GLA_DV = 128
GLA_GATE_RANK = 16
GLA_TAU = 16.0
N_GROUPS = 4
EXPERTS_PER_GROUP = 8
N_EXPERTS = N_GROUPS * EXPERTS_PER_GROUP
NORM_EPS = 1e-6

LANES = 128
VMEM_LIMIT = 56 * 1024 * 1024
NEG = -1e30

NORM_ROWS = 512
PROJ_TM, PROJ_TN = 1024, 512
NSA_TQ, NSA_TK = 128, 256
GLA_TS, GLA_C, GLA_SUB = 512, 64, 16
MIX_TM, MIX_TN = 512, 1024
OUT_TM, OUT_TN = 512, 1024
ROUTE_TM = 256
MOE_TM = 512
MOE_CJ = 512
MOE_CN = 2048
GATHER_ROWS = 512
COMBINE_ROWS = 256


def _cparams(*sem):
    return pltpu.CompilerParams(dimension_semantics=sem, vmem_limit_bytes=VMEM_LIMIT)


def _dot(a, b, **kw):
    return jnp.dot(a, b, preferred_element_type=F32, **kw)


def _dot_nt(a, b, **kw):
    return lax.dot_general(a, b, (((1,), (1,)), ((), ())), preferred_element_type=F32, **kw)


def _dot_tn(a, b, **kw):
    return lax.dot_general(a, b, (((0,), (0,)), ((), ())), preferred_element_type=F32, **kw)


def _sigmoid(x):
    return 1.0 / (1.0 + jnp.exp(-x))


def _masked_softmax(s, mask):
    sm = jnp.where(mask, s, NEG)
    m = jnp.max(sm, axis=-1, keepdims=True)
    e = jnp.where(mask, jnp.exp(sm - m), 0.0)
    return e * (1.0 / jnp.maximum(jnp.sum(e, axis=-1, keepdims=True), 1e-30))


def _rmsnorm_kernel(x_ref, g_ref, o_ref):
    x = x_ref[...]
    ms = jnp.mean(x * x, axis=-1, keepdims=True)
    o_ref[...] = (x * lax.rsqrt(ms + NORM_EPS) * g_ref[...]).astype(o_ref.dtype)


def rmsnorm(x, g, out_dtype=BF16, rows=NORM_ROWS):
    T, D = x.shape
    rows = min(rows, T)
    return pl.pallas_call(
        _rmsnorm_kernel,
        out_shape=jax.ShapeDtypeStruct((T, D), out_dtype),
        grid=(T // rows,),
        in_specs=[pl.BlockSpec((rows, D), lambda i: (i, 0)),
                  pl.BlockSpec((1, D), lambda i: (0, 0))],
        out_specs=pl.BlockSpec((rows, D), lambda i: (i, 0)),
        compiler_params=_cparams("parallel"),
        name="rmsnorm",
    )(x, g.reshape(1, D))


def _mm_kernel(a_ref, b_ref, o_ref):
    o_ref[...] = _dot(a_ref[...], b_ref[...]).astype(o_ref.dtype)


def matmul(a, b, out_dtype, tm, tn, name):
    M, K = a.shape
    N = b.shape[1]
    tm, tn = min(tm, M), min(tn, N)
    return pl.pallas_call(
        _mm_kernel,
        out_shape=jax.ShapeDtypeStruct((M, N), out_dtype),
        grid=(M // tm, N // tn),
        in_specs=[pl.BlockSpec((tm, K), lambda i, j: (i, 0)),
                  pl.BlockSpec((K, tn), lambda i, j: (0, j))],
        out_specs=pl.BlockSpec((tm, tn), lambda i, j: (i, j)),
        compiler_params=_cparams("parallel", "arbitrary"),
        name=name,
    )(a, b)


def _compress_kernel(seq_ref, pos_ref, w1_ref, b1_ref, w2_ref, o_ref):
    seq = seq_ref[0]
    nc = seq.shape[0] // CMP_STRIDE
    x = seq.reshape(nc, CMP_STRIDE * seq.shape[1]).astype(F32)
    half = x.shape[1]
    pos = pos_ref[0]
    w1 = w1_ref[0]
    u0 = _dot((x + pos[0:1, :]).astype(BF16), w1[:half, :])
    u1 = _dot((x + pos[1:2, :]).astype(BF16), w1[half:, :])
    pre = u0 + pltpu.roll(u1, nc - 1, 0) + b1_ref[0]
    h = 0.5 * pre * (1.0 + jnp.tanh(0.7978845608028654 * (pre + 0.044715 * pre * pre * pre)))
    o_ref[0, 0, 0] = _dot(h.astype(BF16), w2_ref[0]).astype(o_ref.dtype)


def compress(proj, col_kc, pos2, w1, b1, w2):
    B, S, _ = proj.shape
    G = NSA_KV_HEADS
    Dh = w2.shape[-1]
    NC, HW = S // CMP_STRIDE, CMP_STRIDE * Dh
    return pl.pallas_call(
        _compress_kernel,
        out_shape=jax.ShapeDtypeStruct((2, B, G, NC, Dh), BF16),
        grid=(2, B, G),
        in_specs=[pl.BlockSpec((1, S, Dh), lambda a, b, g: (b, 0, col_kc + a * G + g)),
                  pl.BlockSpec((1, 2, HW), lambda a, b, g: (a, 0, 0)),
                  pl.BlockSpec((1, 2 * HW, Dh), lambda a, b, g: (a, 0, 0)),
                  pl.BlockSpec((1, 1, Dh), lambda a, b, g: (a, 0, 0)),
                  pl.BlockSpec((1, Dh, Dh), lambda a, b, g: (a, 0, 0))],
        out_specs=pl.BlockSpec((1, 1, 1, NC, Dh), lambda a, b, g: (a, b, g, 0, 0)),
        compiler_params=_cparams("parallel", "parallel", "parallel"),
        name="nsa_compress",
    )(proj, pos2, w1, b1, w2)


MASK_BIG = 2.0 ** 100


def _nsa_kernel(slopes_ref, q_ref, kc_ref, vc_ref, ks_ref, vs_ref, kw_ref, vw_ref, g_ref,
                paux_ref, saux_ref, caux_ref, mmat_ref, o_ref,
                qx_ref, sc_ref, sw_ref, sa_ref, sb_ref, score_ref, m_ref, acc_ref,
                *, TQ, TK, NC, NSEL, NTOP):
    R, Dh = NSA_GROUP, NSA_HEAD_DIM
    g = pl.program_id(1)
    q0 = pl.program_id(2) * TQ
    scale = Dh ** -0.5
    lane = lax.broadcasted_iota(I32, (1, LANES), 1)
    t1 = q0 + lax.broadcasted_iota(I32, (TQ, 1), 0)
    head = lambda r: slice(r * TQ, (r + 1) * TQ)

    def alibi_cols(r):
        hi, lo = slopes_ref[0, g * R + r], slopes_ref[1, g * R + r]
        c = jnp.where(lane == 0, 64.0 * hi, jnp.where(lane == 1, 64.0 * lo,
                                                      jnp.where(lane == 2, hi, jnp.where(lane == 3, lo, 0.0))))
        return jnp.broadcast_to(c, (TQ, LANES))

    for r in range(R):
        qx_ref[head(r), 0:Dh] = (q_ref[0, :, r * Dh:(r + 1) * Dh].astype(F32) * scale).astype(BF16)
        qx_ref[head(r), Dh:Dh + LANES] = alibi_cols(r).astype(BF16)

    def with_ones(v):
        return jnp.concatenate([v, jnp.ones(v.shape, v.dtype)], axis=1)

    def exp_pv(s, v1):
        m = jnp.broadcast_to(jnp.max(s, axis=-1, keepdims=True), (TQ, LANES))
        e = [jnp.exp(s[:, c * LANES:(c + 1) * LANES] - m) for c in range(s.shape[1] // LANES)]
        return e, _dot(jnp.concatenate(e, axis=1).astype(BF16), v1)

    WK = WINDOW + TQ
    ws = pl.multiple_of(jnp.maximum(q0 - WINDOW, 0), TQ)
    sc_ref[...] = _dot_nt(qx_ref[...], jnp.concatenate([kc_ref[0, 0, 0], caux_ref[...]], axis=1))
    sw_ref[...] = _dot_nt(qx_ref[...], jnp.concatenate([kw_ref[0, pl.ds(ws, WK), :], paux_ref[pl.ds(ws, WK), :]],
                                                       axis=1))

    cmp_end = lax.broadcasted_iota(I32, (1, NC), 1) * CMP_STRIDE + (CMP_BLOCK - 1)
    cmp_bias = jnp.where(cmp_end <= t1, 0.0, NEG)
    row_ok = q0 + lax.broadcasted_iota(I32, (TQ, LANES), 0) >= CMP_BLOCK - 1
    vc1 = with_ones(vc_ref[0, 0, 0])
    o_cmp = []
    imp = None
    for r in range(R):
        e, o2 = exp_pv(sc_ref[head(r), :] + cmp_bias, vc1)
        inv = jnp.where(row_ok, 1.0 / jnp.maximum(o2[:, Dh:], 1e-30), 0.0)
        o_cmp.append(o2[:, :Dh] * inv)
        p = jnp.concatenate([ec * inv for ec in e], axis=1)
        imp = p if imp is None else imp + p

    vw1 = with_ones(vw_ref[0, pl.ds(ws, WK), :])
    dw = t1 - (ws + lax.broadcasted_iota(I32, (1, WK), 1))
    win_bias = jnp.where((dw >= 0) & (dw < WINDOW), 0.0, NEG)
    o_win = []
    for r in range(R):
        _, o2 = exp_pv(sw_ref[head(r), :] + win_bias, vw1)
        o_win.append(o2[:, :Dh] * (1.0 / jnp.maximum(o2[:, Dh:], 1e-30)))

    mm = mmat_ref[...]
    i1 = imp.astype(BF16)
    rem = imp - i1.astype(F32)
    i2 = rem.astype(BF16)
    i3 = (rem - i2.astype(F32)).astype(BF16)
    imp_sel = _dot_nt(mm, i1) + _dot_nt(mm, i2) + _dot_nt(mm, i3)

    cur = (q0 + lax.broadcasted_iota(I32, (1, TQ), 1)) // SEL_BLOCK
    blk = lax.broadcasted_iota(I32, (NSEL, 1), 0)
    forced = (blk == 0) | (blk == cur) | (blk == cur - 1)
    score = jnp.where(blk <= cur, jnp.where(forced, BIG, imp_sel), -BIG)
    score_ref[...] = score

    def rank_pair(i2, cnt):
        for d in range(2):
            ii = 2 * i2 + d
            other = score_ref[pl.ds(ii, 1), :]
            beats = (other > score) | ((other == score) & (blk > ii))
            cnt = cnt + jnp.where(beats, 1.0, 0.0)
        return cnt

    n_blk = (q0 + TQ) // SEL_BLOCK
    cnt = lax.fori_loop(0, n_blk // 2, rank_pair, jnp.zeros((NSEL, TQ), F32))
    unsel = jnp.where(cnt < NTOP, 0.0, -1.0).astype(BF16)
    place = jnp.where(lax.broadcasted_iota(I32, (NSEL, LANES), 1)
                      == lax.broadcasted_iota(I32, (NSEL, LANES), 0) + LANES // 2, 1.0, 0.0).astype(BF16)
    unsel_l = _dot_tn(unsel, place)
    for r in range(R):
        qx_ref[head(r), Dh:Dh + LANES] = (alibi_cols(r) + unsel_l).astype(BF16)

    m_ref[...] = jnp.full(m_ref.shape, NEG, F32)
    acc_ref[...] = jnp.zeros(acc_ref.shape, F32)
    pos_in_tile = lax.broadcasted_iota(I32, (1, TK), 1)

    def issue_scores(kt, s_ref):
        k0 = pl.multiple_of(kt * TK, TK)
        kk = jnp.concatenate([ks_ref[0, pl.ds(k0, TK), :], saux_ref[pl.ds(k0, TK), :]], axis=1)
        s_ref[...] = _dot_nt(qx_ref[...], kk)

    def consume_scores(kt, s_ref, causal):
        k0 = pl.multiple_of(kt * TK, TK)
        vv1 = with_ones(vs_ref[0, pl.ds(k0, TK), :])
        if causal:
            causal_bias = jnp.where(t1 >= k0 + pos_in_tile, 0.0, -MASK_BIG)
        for r in range(R):
            sc = s_ref[head(r), :]
            if causal:
                sc = sc + causal_bias
            m_old = m_ref[r]
            m_new = jnp.maximum(m_old, jnp.broadcast_to(jnp.max(sc, axis=-1, keepdims=True), (TQ, LANES)))
            alpha = jnp.exp(m_old - m_new)
            p = jnp.concatenate([jnp.exp(sc[:, c * LANES:(c + 1) * LANES] - m_new) for c in range(TK // LANES)],
                                axis=1).astype(BF16)
            m_ref[r] = m_new
            acc_ref[r] = jnp.concatenate([alpha, alpha], axis=1) * acc_ref[r] + _dot(p, vv1)

    def sel_pair(k, carry):
        issue_scores(2 * k + 1, sb_ref)
        consume_scores(2 * k, sa_ref, False)
        issue_scores(2 * k + 2, sa_ref)
        consume_scores(2 * k + 1, sb_ref, False)
        return carry

    n_below = q0 // TK
    issue_scores(0, sa_ref)
    lax.fori_loop(0, n_below // 2, sel_pair, 0)

    @pl.when(n_below % 2 == 1)
    def _():
        issue_scores(n_below, sb_ref)
        consume_scores(n_below - 1, sa_ref, False)
        consume_scores(n_below, sb_ref, True)

    @pl.when(n_below % 2 == 0)
    def _():
        consume_scores(n_below, sa_ref, True)

    gt = _sigmoid(g_ref[0, 0])
    for r in range(R):
        o_sel = acc_ref[r, :, 0:Dh] * (1.0 / jnp.maximum(acc_ref[r, :, Dh:2 * Dh], 1e-30))
        o = (gt[:, 3 * r:3 * r + 1] * o_cmp[r] + gt[:, 3 * r + 1:3 * r + 2] * o_sel
             + gt[:, 3 * r + 2:3 * r + 3] * o_win[r])
        o_ref[0, :, r * Dh:(r + 1) * Dh] = o.astype(o_ref.dtype)


def _nsa_constants(S, NC, NSEL):
    def pos_cols(pos):
        col = jnp.arange(LANES)[None, :]
        hi, lo = (pos // 64)[:, None], (pos % 64)[:, None]
        return jnp.where(col < 2, hi, jnp.where(col < 4, lo, 0)).astype(F32)

    pos = jnp.arange(S)
    paux = pos_cols(pos)
    onehot = (jnp.arange(LANES)[None, :] - LANES // 2 == (pos // SEL_BLOCK)[:, None]) & (jnp.arange(LANES)[None, :] >= LANES // 2)
    saux = paux + jnp.where(onehot, MASK_BIG, 0.0)
    caux = pos_cols(jnp.arange(NC) * CMP_STRIDE + (CMP_BLOCK - 1))
    d = jnp.arange(NC)[None, :] - 4 * jnp.arange(NSEL)[:, None]
    mmat = jnp.where((d == -1) | (d == 3), 1.0, jnp.where((d >= 0) & (d <= 2), 2.0, 0.0))
    return paux.astype(BF16), saux.astype(BF16), caux.astype(BF16), mmat.astype(BF16)


def nsa_attention(proj, kvc, gates, slopes, *, col_q, col_ks, col_vs, col_kw, col_vw):
    B, S, _ = proj.shape
    G, R, Dh = NSA_KV_HEADS, NSA_GROUP, NSA_HEAD_DIM
    NC = kvc.shape[3]
    NSEL = S // SEL_BLOCK
    TQ, TK = min(NSA_TQ, S), min(NSA_TK, S)
    assert S % TQ == 0 and TK % TQ == 0 and S % TK == 0 and S >= WINDOW + TQ and NC == S // CMP_STRIDE
    assert NSEL <= LANES // 2 and S // 64 <= 256 and TQ % (2 * SEL_BLOCK) == 0
    s_hi = slopes.astype(BF16).astype(F32)
    s_lo = (slopes - s_hi).astype(BF16).astype(F32)
    paux, saux, caux, mmat = _nsa_constants(S, NC, NSEL)
    kern = functools.partial(_nsa_kernel, TQ=TQ, TK=TK, NC=NC, NSEL=NSEL, NTOP=min(SEL_TOPN, NSEL))
    kv_spec = lambda col: pl.BlockSpec((1, S, Dh), lambda b, g, i: (b, 0, col + g))
    whole = lambda a: pl.BlockSpec(a.shape, lambda b, g, i: (0, 0))
    return pl.pallas_call(
        kern,
        out_shape=jax.ShapeDtypeStruct((B, S, G * R * Dh), BF16),
        grid=(B, G, S // TQ),
        in_specs=[pl.BlockSpec(memory_space=pltpu.SMEM),
                  pl.BlockSpec((1, TQ, R * Dh), lambda b, g, i: (b, i, col_q // R + g)),
                  pl.BlockSpec((1, 1, 1, NC, Dh), lambda b, g, i: (0, b, g, 0, 0)),
                  pl.BlockSpec((1, 1, 1, NC, Dh), lambda b, g, i: (1, b, g, 0, 0)),
                  kv_spec(col_ks), kv_spec(col_vs), kv_spec(col_kw), kv_spec(col_vw),
                  pl.BlockSpec((1, 1, TQ, 3 * R), lambda b, g, i: (b, g, i, 0)),
                  whole(paux), whole(saux), whole(caux), whole(mmat)],
        out_specs=pl.BlockSpec((1, TQ, R * Dh), lambda b, g, i: (b, i, g)),
        scratch_shapes=[pltpu.VMEM((R * TQ, Dh + LANES), BF16), pltpu.VMEM((R * TQ, NC), F32),
                        pltpu.VMEM((R * TQ, WINDOW + TQ), F32), pltpu.VMEM((R * TQ, TK), F32),
                        pltpu.VMEM((R * TQ, TK), F32),
                        pltpu.VMEM((NSEL, TQ), F32), pltpu.VMEM((R, TQ, LANES), F32),
                        pltpu.VMEM((R, TQ, 2 * Dh), F32)],
        compiler_params=_cparams("parallel", "parallel", "arbitrary"),
        name="nsa_attention",
    )(jnp.stack([s_hi, s_lo]), proj, kvc, kvc, proj, proj, proj, proj, gates, paux, saux, caux, mmat)


def _gla_kernel(q_ref, k_ref, v_ref, r_ref, xa_ref, wa_ref, ba_ref, ng_ref, o_ref, st_ref, la_ref, *, TS, C):
    DK2, DV2 = 2 * GLA_DK, 2 * GLA_DV
    SUB = GLA_SUB
    NSUB = C // SUB

    @pl.when(pl.program_id(2) == 0)
    def _():
        st_ref[...] = jnp.zeros_like(st_ref)

    z = _dot(xa_ref[0], wa_ref[...], precision=lax.Precision.HIGHEST) + ba_ref[...]
    la_ref[...] = (jnp.minimum(z, 0.0) - jnp.log(1.0 + jnp.exp(-jnp.abs(z)))) * (1.0 / GLA_TAU)

    lane = lax.broadcasted_iota(I32, (1, DK2), 1)
    head_a = lane < GLA_DK
    row = lax.broadcasted_iota(I32, (C, 1), 0)
    col = lax.broadcasted_iota(I32, (1, C), 1)
    tri = jnp.where(col <= row, 1.0, 0.0).astype(BF16)
    mask1 = (row // 32 == 1) & (col // 32 == 0)
    mask2 = (row // 32 == col // 32) & ((row // 16) % 2 == 1) & ((col // 16) % 2 == 0)
    kv_same = (lax.broadcasted_iota(I32, (DK2, DV2), 0) < GLA_DK) == (lax.broadcasted_iota(I32, (DK2, DV2), 1) < GLA_DV)
    hsel = jnp.where(kv_same, 1.0, 0.0).astype(BF16)
    vk_same = (lax.broadcasted_iota(I32, (DV2, DK2), 0) < GLA_DV) == (lax.broadcasted_iota(I32, (DV2, DK2), 1) < GLA_DK)
    t_in = row % SUB

    def chunk(c, carry):
        r0 = pl.multiple_of(c * C, C)
        q = q_ref[0, pl.ds(r0, C), :].astype(F32) * (GLA_DK ** -0.5)
        k = k_ref[0, pl.ds(r0, C), :].astype(F32)
        vb = v_ref[0, pl.ds(r0, C), :]
        v = vb.astype(F32)
        la = la_ref[pl.ds(r0, C), :]
        l1 = la.astype(BF16)
        lr = la - l1.astype(F32)
        l2 = lr.astype(BF16)
        l3 = (lr - l2.astype(F32)).astype(BF16)
        b = _dot(tri, l1) + _dot(tri, l2) + _dot(tri, l3)

        r1 = b[31:32, :]
        r2 = jnp.where(row < 32, b[15:16, :], b[47:48, :])
        q1 = q * jnp.exp(jnp.minimum(b - r1, 0.0))
        k1 = (k * jnp.exp(jnp.minimum(r1 - b, 0.0))).astype(BF16)
        q2 = q * jnp.exp(jnp.minimum(b - r2, 0.0))
        k2 = (k * jnp.exp(jnp.minimum(r2 - b, 0.0))).astype(BF16)

        def scores(hm):
            a1 = _dot_nt(jnp.where(hm, q1, 0.0).astype(BF16), k1)
            a2 = _dot_nt(jnp.where(hm, q2, 0.0).astype(BF16), k2)
            return (jnp.where(mask1, a1, 0.0) + jnp.where(mask2, a2, 0.0)).astype(BF16)

        o = jnp.concatenate([_dot(scores(head_a), vb[:, :GLA_DV]),
                             _dot(scores(jnp.logical_not(head_a)), vb[:, GLA_DV:])], axis=1)

        b3 = b.reshape(NSUB, SUB, DK2)
        k3 = k.reshape(NSUB, SUB, DK2)
        v3 = v.reshape(NSUB, SUB, DV2)
        prods = []
        for sp in range(SUB):
            bs = jnp.broadcast_to(b3[:, sp:sp + 1, :], (NSUB, SUB, DK2)).reshape(C, DK2)
            ks = jnp.broadcast_to(k3[:, sp:sp + 1, :], (NSUB, SUB, DK2)).reshape(C, DK2)
            prods.append((q * ks * jnp.exp(jnp.minimum(b - bs, 0.0))).astype(BF16))
        ac = _dot(jnp.concatenate(prods, axis=0), hsel)
        for sp in range(SUB):
            vs = jnp.broadcast_to(v3[:, sp:sp + 1, :], (NSUB, SUB, DV2)).reshape(C, DV2)
            o = o + jnp.where(t_in >= sp, ac[sp * C:(sp + 1) * C] * vs, 0.0)

        st = st_ref[...]
        o = o + _dot_nt((q * jnp.exp(b)).astype(BF16), st.astype(BF16))
        bl = b[C - 1:C, :]
        kd = (k * jnp.exp(bl - b)).astype(BF16)
        ds = jnp.where(vk_same, _dot_tn(vb, kd), 0.0)
        st_ref[...] = st * jnp.exp(bl) + ds

        def ln(x):
            mu = jnp.mean(x, axis=-1, keepdims=True)
            xc = x - mu
            return xc * lax.rsqrt(jnp.mean(xc * xc, axis=-1, keepdims=True) + NORM_EPS)

        on = jnp.concatenate([ln(o[:, :GLA_DV]), ln(o[:, GLA_DV:])], axis=1) * ng_ref[...]
        rr = r_ref[0, pl.ds(r0, C), :].astype(F32)
        o_ref[0, pl.ds(r0, C), :] = (on * (rr * _sigmoid(rr))).astype(o_ref.dtype)
        return carry

    lax.fori_loop(0, TS // C, chunk, 0, unroll=8)


def gla_attention(proj, xa, w_alpha, b_alpha, norm_g, *, col_q, col_k, col_v, col_r):
    B, S, _ = proj.shape
    HP = GLA_HEADS // 2
    DK2, DV2 = 2 * GLA_DK, 2 * GLA_DV
    TS, C = min(GLA_TS, S), GLA_C
    assert S % TS == 0 and TS % C == 0 and C == 64
    kern = functools.partial(_gla_kernel, TS=TS, C=C)
    return pl.pallas_call(
        kern,
        out_shape=jax.ShapeDtypeStruct((B, S, GLA_HEADS * GLA_DV), BF16),
        grid=(B, HP, S // TS),
        in_specs=[pl.BlockSpec((1, TS, DK2), lambda b, p, s: (b, s, col_q + p)),
                  pl.BlockSpec((1, TS, DK2), lambda b, p, s: (b, s, col_k + p)),
                  pl.BlockSpec((1, TS, DV2), lambda b, p, s: (b, s, col_v // 2 + p)),
                  pl.BlockSpec((1, TS, DV2), lambda b, p, s: (b, s, col_r // 2 + p)),
                  pl.BlockSpec((1, TS, GLA_GATE_RANK), lambda b, p, s: (b, s, 0)),
                  pl.BlockSpec((GLA_GATE_RANK, DK2), lambda b, p, s: (0, p)),
                  pl.BlockSpec((1, DK2), lambda b, p, s: (0, p)),
                  pl.BlockSpec((1, DV2), lambda b, p, s: (0, p))],
        out_specs=pl.BlockSpec((1, TS, DV2), lambda b, p, s: (b, s, p)),
        scratch_shapes=[pltpu.VMEM((DV2, DK2), F32), pltpu.VMEM((TS, DK2), F32)],
        compiler_params=_cparams("parallel", "parallel", "arbitrary"),
        name="gla_attention",
    )(proj, proj, proj, proj, xa, w_alpha, b_alpha.reshape(1, -1), norm_g.reshape(1, -1))


def _mix_kernel(oa_ref, ob_ref, wa_ref, wb_ref, ma_ref, mb_ref, o_ref):
    ya = _dot(oa_ref[...], wa_ref[...])
    yb = _dot(ob_ref[...], wb_ref[...])
    o_ref[...] = (_sigmoid(ma_ref[...].astype(F32)) * ya + _sigmoid(mb_ref[...].astype(F32)) * yb).astype(o_ref.dtype)


def gated_mix(o_nsa, o_gla, wa, wb, proj2d, col_ma, col_mb):
    T, KA = o_nsa.shape
    KB = o_gla.shape[1]
    N = wa.shape[1]
    tm, tn = min(MIX_TM, T), min(MIX_TN, N)
    ca, cb = col_ma * LANES // tn, col_mb * LANES // tn
    assert (col_ma * LANES) % tn == 0 and (col_mb * LANES) % tn == 0
    return pl.pallas_call(
        _mix_kernel,
        out_shape=jax.ShapeDtypeStruct((T, N), BF16),
        grid=(T // tm, N // tn),
        in_specs=[pl.BlockSpec((tm, KA), lambda i, j: (i, 0)),
                  pl.BlockSpec((tm, KB), lambda i, j: (i, 0)),
                  pl.BlockSpec((KA, tn), lambda i, j: (0, j)),
                  pl.BlockSpec((KB, tn), lambda i, j: (0, j)),
                  pl.BlockSpec((tm, tn), lambda i, j: (i, ca + j)),
                  pl.BlockSpec((tm, tn), lambda i, j: (i, cb + j))],
        out_specs=pl.BlockSpec((tm, tn), lambda i, j: (i, j)),
        compiler_params=_cparams("parallel", "arbitrary"),
        name="gated_mix",
    )(o_nsa, o_gla, wa, wb, proj2d, proj2d)


def _out_kernel(a_ref, w_ref, x_ref, o_ref):
    o_ref[...] = x_ref[...] + _dot(a_ref[...], w_ref[...])


def out_proj(mixed, w, x):
    T, K = mixed.shape
    N = w.shape[1]
    tm, tn = min(OUT_TM, T), min(OUT_TN, N)
    return pl.pallas_call(
        _out_kernel,
        out_shape=jax.ShapeDtypeStruct((T, N), F32),
        grid=(T // tm, N // tn),
        in_specs=[pl.BlockSpec((tm, K), lambda i, j: (i, 0)),
                  pl.BlockSpec((K, tn), lambda i, j: (0, j)),
                  pl.BlockSpec((tm, tn), lambda i, j: (i, j))],
        out_specs=pl.BlockSpec((tm, tn), lambda i, j: (i, j)),
        compiler_params=_cparams("parallel", "arbitrary"),
        name="out_proj",
    )(mixed, w, x)


def _router_kernel(h_ref, g_ref, w_ref, b_ref, id_ref, wt_ref, u_ref):
    x = h_ref[...]
    u = x * lax.rsqrt(jnp.mean(x * x, axis=-1, keepdims=True) + NORM_EPS) * g_ref[...]
    u_ref[...] = u.astype(u_ref.dtype).reshape(u_ref.shape)
    logit = _dot(u, w_ref[...], precision=lax.Precision.HIGHEST) + b_ref[...]
    lane = lax.broadcasted_iota(I32, logit.shape, 1)
    big = jnp.int32(1 << 20)
    gmask = lane < N_GROUPS
    gl = jnp.where(gmask, logit, NEG)
    gmax = jnp.max(gl, axis=-1, keepdims=True)
    g_star = jnp.min(jnp.where(gmask & (gl == gmax), lane, big), axis=-1, keepdims=True)
    p_group = 1.0 / jnp.sum(jnp.where(gmask, jnp.exp(gl - gmax), 0.0), axis=-1, keepdims=True)
    e_lo = N_GROUPS + EXPERTS_PER_GROUP * g_star
    emask = (lane >= e_lo) & (lane < e_lo + EXPERTS_PER_GROUP)
    el = jnp.where(emask, logit, NEG)
    m1 = jnp.max(el, axis=-1, keepdims=True)
    i1 = jnp.min(jnp.where(emask & (el == m1), lane, big), axis=-1, keepdims=True)
    emask2 = emask & (lane != i1)
    el2 = jnp.where(emask2, logit, NEG)
    m2 = jnp.max(el2, axis=-1, keepdims=True)
    i2 = jnp.min(jnp.where(emask2 & (el2 == m2), lane, big), axis=-1, keepdims=True)
    e2 = jnp.exp(m2 - m1)
    w1 = p_group / (1.0 + e2)
    w2 = p_group * e2 / (1.0 + e2)
    id_ref[...] = jnp.where(lane == 0, i1 - N_GROUPS, jnp.where(lane == 1, i2 - N_GROUPS, 0))
    wt_ref[...] = jnp.where(lane == 0, w1, jnp.where(lane == 1, w2, 0.0))


def router(h, g, w_r, b_r):
    T, D = h.shape
    tm = min(ROUTE_TM, T)
    return pl.pallas_call(
        _router_kernel,
        out_shape=(jax.ShapeDtypeStruct((T, LANES), I32), jax.ShapeDtypeStruct((T, LANES), F32),
                   jax.ShapeDtypeStruct((T, D // LANES, LANES), BF16)),
        grid=(T // tm,),
        in_specs=[pl.BlockSpec((tm, D), lambda i: (i, 0)),
                  pl.BlockSpec((1, D), lambda i: (0, 0)),
                  pl.BlockSpec((D, LANES), lambda i: (0, 0)),
                  pl.BlockSpec((1, LANES), lambda i: (0, 0))],
        out_specs=(pl.BlockSpec((tm, LANES), lambda i: (i, 0)), pl.BlockSpec((tm, LANES), lambda i: (i, 0)),
                   pl.BlockSpec((tm, D // LANES, LANES), lambda i: (i, 0, 0))),
        compiler_params=_cparams("parallel"),
        name="moe_router",
    )(h, g.reshape(1, D), w_r, b_r)


def _row_copy(src_hbm, dst_ref, sem, src_row, dst_row):
    return pltpu.make_async_copy(src_hbm.at[pl.ds(src_row, 1)], dst_ref.at[pl.ds(dst_row, 1)], sem)


def _dispatch_rows_kernel(tok_ref, src_hbm, o_ref, buf, sem, *, rows):
    def start(r2, c):
        for d in range(2):
            r = 2 * r2 + d
            _row_copy(src_hbm, buf, sem, tok_ref[0, 0, r], r).start(priority=d)
        return c

    def wait(r, c):
        _row_copy(src_hbm, buf, sem, 0, r).wait()
        return c

    lax.fori_loop(0, rows // 2, start, 0)
    lax.fori_loop(0, rows, wait, 0)
    o_ref[...] = buf[...].reshape(o_ref.shape)


def dispatch_rows(u3, row_token):
    n_rows = row_token.shape[0]
    D = u3.shape[1] * u3.shape[2]
    rows = min(GATHER_ROWS, n_rows)
    assert n_rows % rows == 0
    return pl.pallas_call(
        functools.partial(_dispatch_rows_kernel, rows=rows),
        out_shape=jax.ShapeDtypeStruct((n_rows, D), u3.dtype),
        grid=(n_rows // rows,),
        in_specs=[pl.BlockSpec((1, 1, rows), lambda i: (i, 0, 0), memory_space=pltpu.SMEM),
                  pl.BlockSpec(memory_space=pl.ANY)],
        out_specs=pl.BlockSpec((rows, D), lambda i: (i, 0)),
        scratch_shapes=[pltpu.VMEM((rows,) + u3.shape[1:], u3.dtype), pltpu.SemaphoreType.DMA],
        compiler_params=_cparams("arbitrary"),
        name="moe_dispatch_rows",
    )(row_token.reshape(n_rows // rows, 1, rows), u3)


def _ffn_up_kernel(te_ref, new_ref, nv_ref, x_ref, wg_ref, wu_ref, o_ref, wg_bf, wu_bf):
    i = pl.program_id(1)

    @pl.when(new_ref[i] == 1)
    def _():
        wg_bf[...] = wg_ref[0].astype(BF16)
        wu_bf[...] = wu_ref[0].astype(BF16)

    @pl.when(i < nv_ref[0])
    def _():
        x = x_ref[...]
        a = _dot(x, wg_bf[...])
        u = _dot(x, wu_bf[...])
        o_ref[...] = (a * _sigmoid(a) * u).astype(o_ref.dtype)

    @pl.when(i >= nv_ref[0])
    def _():
        o_ref[...] = jnp.zeros_like(o_ref)


def ffn_up(x_sorted, tables, w_gate, w_up):
    NP, D = x_sorted.shape
    E, _, DE = w_gate.shape
    tm, cj = MOE_TM, min(MOE_CJ, DE)
    return pl.pallas_call(
        _ffn_up_kernel,
        out_shape=jax.ShapeDtypeStruct((NP, DE), BF16),
        grid_spec=pltpu.PrefetchScalarGridSpec(
            num_scalar_prefetch=3,
            grid=(DE // cj, NP // tm),
            in_specs=[pl.BlockSpec((tm, D), lambda j, i, te, nw, nv: (i, 0)),
                      pl.BlockSpec((1, D, cj), lambda j, i, te, nw, nv: (te[i], 0, j)),
                      pl.BlockSpec((1, D, cj), lambda j, i, te, nw, nv: (te[i], 0, j))],
            out_specs=pl.BlockSpec((tm, cj), lambda j, i, te, nw, nv: (i, j)),
            scratch_shapes=[pltpu.VMEM((D, cj), BF16), pltpu.VMEM((D, cj), BF16)]),
        compiler_params=_cparams("arbitrary", "arbitrary"),
        name="moe_ffn_up",
    )(*tables, x_sorted, w_gate, w_up)


def _ffn_down_kernel(te_ref, new_ref, nv_ref, h_ref, wd_ref, o_ref, wd_bf):
    i = pl.program_id(1)

    @pl.when(new_ref[i] == 1)
    def _():
        wd_bf[...] = wd_ref[0].astype(BF16)

    @pl.when(i < nv_ref[0])
    def _():
        o_ref[...] = _dot(h_ref[...], wd_bf[...]).astype(o_ref.dtype).reshape(o_ref.shape)

    @pl.when(i >= nv_ref[0])
    def _():
        o_ref[...] = jnp.zeros_like(o_ref)


def ffn_down(h_sorted, tables, w_down):
    NP, DE = h_sorted.shape
    E, _, D = w_down.shape
    tm, cn = MOE_TM, min(MOE_CN, D)
    return pl.pallas_call(
        _ffn_down_kernel,
        out_shape=jax.ShapeDtypeStruct((NP, D // LANES, LANES), BF16),
        grid_spec=pltpu.PrefetchScalarGridSpec(
            num_scalar_prefetch=3,
            grid=(D // cn, NP // tm),
            in_specs=[pl.BlockSpec((tm, DE), lambda j, i, te, nw, nv: (i, 0)),
                      pl.BlockSpec((1, DE, cn), lambda j, i, te, nw, nv: (te[i], 0, j))],
            out_specs=pl.BlockSpec((tm, cn // LANES, LANES), lambda j, i, te, nw, nv: (i, j, 0)),
            scratch_shapes=[pltpu.VMEM((DE, cn), BF16)]),
        compiler_params=_cparams("arbitrary", "arbitrary"),
        name="moe_ffn_down",
    )(*tables, h_sorted, w_down)


def _combine_kernel(idx_ref, nxt_ref, h_ref, wt_ref, y_hbm, g_ref, o_ref, buf, sem, *, rows):
    i = pl.program_id(0)
    slot = i % 2

    def issue(table, s):
        def body(r, c):
            _row_copy(y_hbm, buf.at[s, 0], sem.at[s], table[0, 0, 2 * r], r).start(priority=0)
            _row_copy(y_hbm, buf.at[s, 1], sem.at[s], table[0, 0, 2 * r + 1], r).start(priority=1)
            return c
        lax.fori_loop(0, rows, body, 0)

    @pl.when(i == 0)
    def _():
        issue(idx_ref, 0)

    @pl.when(i + 1 < pl.num_programs(0))
    def _():
        issue(nxt_ref, 1 - slot)

    def wait(r, c):
        _row_copy(y_hbm, buf.at[slot, 0], sem.at[slot], 0, r).wait()
        _row_copy(y_hbm, buf.at[slot, 1], sem.at[slot], 0, r).wait()
        return c

    lax.fori_loop(0, rows, wait, 0)
    wt = wt_ref[...]
    y0 = buf[slot, 0].reshape(h_ref.shape).astype(F32)
    y1 = buf[slot, 1].reshape(h_ref.shape).astype(F32)
    x = h_ref[...] + wt[:, 0:1] * y0 + wt[:, 1:2] * y1
    o_ref[...] = x * lax.rsqrt(jnp.mean(x * x, axis=-1, keepdims=True) + NORM_EPS) * g_ref[...]


def combine_norm(h, wts, dest, y3, g):
    T, D = h.shape
    rows = min(COMBINE_ROWS, T)
    n = T // rows
    table = dest.reshape(n, 1, 2 * rows)
    return pl.pallas_call(
        functools.partial(_combine_kernel, rows=rows),
        out_shape=jax.ShapeDtypeStruct((T, D), F32),
        grid=(n,),
        in_specs=[pl.BlockSpec((1, 1, 2 * rows), lambda i: (i, 0, 0), memory_space=pltpu.SMEM),
                  pl.BlockSpec((1, 1, 2 * rows), lambda i: (jnp.minimum(i + 1, n - 1), 0, 0),
                               memory_space=pltpu.SMEM),
                  pl.BlockSpec((rows, D), lambda i: (i, 0)),
                  pl.BlockSpec((rows, LANES), lambda i: (i, 0)),
                  pl.BlockSpec(memory_space=pl.ANY),
                  pl.BlockSpec((1, D), lambda i: (0, 0))],
        out_specs=pl.BlockSpec((rows, D), lambda i: (i, 0)),
        scratch_shapes=[pltpu.VMEM((2, 2, rows) + y3.shape[1:], y3.dtype), pltpu.SemaphoreType.DMA((2,))],
        compiler_params=_cparams("arbitrary"),
        name="moe_combine_norm",
    )(table, table, h, wts, y3, g.reshape(1, D))


def _dispatch_tables(ids, tm):
    T = ids.shape[0]
    E = N_EXPERTS
    eid = ids.reshape(-1)
    onehot = (eid[:, None] == jnp.arange(E, dtype=I32)[None, :]).astype(I32)
    csum = jnp.cumsum(onehot, axis=0)
    rank = jnp.sum(csum * onehot, axis=1) - 1
    counts = csum[-1]
    padded = ((counts + tm - 1) // tm) * tm
    ends = jnp.cumsum(padded)
    dest = jnp.sum(onehot * (ends - padded)[None, :], axis=1) + rank
    n_rows = 2 * T + E * tm
    tile_start = jnp.arange(n_rows // tm, dtype=I32) * tm
    tile_expert = jnp.sum((ends[None, :] <= tile_start[:, None]).astype(I32), axis=1)
    last_used = jnp.max(jnp.where(counts > 0, jnp.arange(E, dtype=I32), 0))
    tile_expert = jnp.minimum(tile_expert, last_used)
    tile_new = jnp.concatenate([jnp.ones((1,), I32), (tile_expert[1:] != tile_expert[:-1]).astype(I32)])
    n_tiles_used = (ends[-1] // tm).reshape(1)
    row_token = jnp.zeros((n_rows,), I32).at[dest].set(jnp.arange(2 * T, dtype=I32) // 2)
    return dest.astype(I32), row_token, (tile_expert, tile_new, n_tiles_used.astype(I32))


def _forward(x, norm_mix_g, w_in, cmp_k_pos, cmp_k_w1, cmp_k_b1, cmp_k_w2, cmp_v_pos, cmp_v_w1, cmp_v_b1, cmp_v_w2,
             gla_w_alpha, gla_b_alpha, gla_norm_g, w_branch_nsa, w_branch_gla, w_out, norm_ffn_g,
             w_router_group, b_router_group, w_router_expert, b_router_expert, w_exp_gate, w_exp_up, w_exp_down,
             norm_final_g):
    B, S, D = x.shape
    T = B * S
    G, R, Dh = NSA_KV_HEADS, NSA_GROUP, NSA_HEAD_DIM
    NSA_Q, NSA_KV = NSA_HEADS * Dh, G * Dh
    GQK, GV = GLA_HEADS * GLA_DK, GLA_HEADS * GLA_DV
    h = x.reshape(T, D)
    assert w_in.shape[0] == 1, "the final norm is fused into the (single) layer's combine step"
    for l in range(1):
        w = w_in[l]
        o_ng = NSA_Q + 6 * NSA_KV
        o_gq = o_ng + 3 * NSA_HEADS
        o_ga = o_gq + 2 * GQK + 2 * GV
        o_ma = o_ga + GLA_GATE_RANK
        n_small = 3 * NSA_HEADS + GLA_GATE_RANK
        w_small = jnp.concatenate([w[:, o_ng:o_gq], w[:, o_ga:o_ma], jnp.zeros((D, LANES - n_small), F32)],
                                  axis=1).astype(BF16)
        c_kc = NSA_Q // LANES
        c_ks, c_vs, c_kw, c_vw = c_kc + 2 * G, c_kc + 3 * G, c_kc + 4 * G, c_kc + 5 * G
        c_gk = GQK // LANES
        c_gv = c_gk + GQK // LANES
        c_gr = c_gv + GV // LANES

        xn = rmsnorm(h, norm_mix_g[l])
        proj_nsa = matmul(xn, w[:, :o_ng].astype(BF16), BF16, PROJ_TM, PROJ_TN, "proj_nsa").reshape(B, S, -1)
        proj_gla = matmul(xn, w[:, o_gq:o_ga].astype(BF16), BF16, PROJ_TM, PROJ_TN, "proj_gla").reshape(B, S, -1)
        proj_mix = matmul(xn, w[:, o_ma:].astype(BF16), BF16, PROJ_TM, PROJ_TN, "proj_mix")
        small = matmul(xn, w_small, F32, PROJ_TM, LANES, "proj_small")

        pos2 = jnp.stack([cmp_k_pos[l], cmp_v_pos[l]]).reshape(2, 2, CMP_STRIDE * Dh)
        w1 = jnp.stack([cmp_k_w1[l], cmp_v_w1[l]]).astype(BF16)
        b1 = jnp.stack([cmp_k_b1[l], cmp_v_b1[l]]).reshape(2, 1, Dh)
        w2 = jnp.stack([cmp_k_w2[l], cmp_v_w2[l]]).astype(BF16)
        kvc = compress(proj_nsa, c_kc, pos2, w1, b1, w2)
        gates = small[:, :3 * NSA_HEADS].reshape(B, S, G, 3 * R).transpose(0, 2, 1, 3)
        slopes = jnp.exp2(-8.0 * jnp.arange(1, NSA_HEADS + 1, dtype=F32) / NSA_HEADS)
        o_nsa = nsa_attention(proj_nsa, kvc, gates, slopes, col_q=0, col_ks=c_ks, col_vs=c_vs,
                              col_kw=c_kw, col_vw=c_vw)

        xa = small[:, 3 * NSA_HEADS:n_small].reshape(B, S, GLA_GATE_RANK)
        o_gla = gla_attention(proj_gla, xa, gla_w_alpha[l], gla_b_alpha[l], gla_norm_g[l],
                              col_q=0, col_k=c_gk, col_v=c_gv, col_r=c_gr)

        mixed = gated_mix(o_nsa.reshape(T, -1), o_gla.reshape(T, -1), w_branch_nsa[l].astype(BF16),
                          w_branch_gla[l].astype(BF16), proj_mix, 0, D // LANES)
        h = out_proj(mixed, w_out[l].astype(BF16), h)

        n_r = N_GROUPS + N_EXPERTS
        w_r = jnp.concatenate([w_router_group[l], w_router_expert[l], jnp.zeros((D, LANES - n_r), F32)], axis=1)
        b_r = jnp.concatenate([b_router_group[l], b_router_expert[l], jnp.zeros((LANES - n_r,), F32)]).reshape(1, LANES)
        ids, wts, u = router(h, norm_ffn_g[l], w_r, b_r)
        dest, row_token, tables = _dispatch_tables(ids[:, :2], MOE_TM)
        x_sorted = dispatch_rows(u, row_token)
        h_sorted = ffn_up(x_sorted, tables, w_exp_gate[l], w_exp_up[l])
        y_sorted = ffn_down(h_sorted, tables, w_exp_down[l])
        h = combine_norm(h, wts, dest, y_sorted, norm_final_g)
    return h.reshape(B, S, D)


def kernel(x, norm_mix_g, w_in, cmp_k_pos, cmp_k_w1, cmp_k_b1, cmp_k_w2, cmp_v_pos, cmp_v_w1, cmp_v_b1, cmp_v_w2, gla_w_alpha, gla_b_alpha, gla_norm_g, w_branch_nsa, w_branch_gla, w_out, norm_ffn_g, w_router_group, b_router_group, w_router_expert, b_router_expert, w_exp_gate, w_exp_up, w_exp_down, norm_final_g):
    return _forward(x, norm_mix_g, w_in, cmp_k_pos, cmp_k_w1, cmp_k_b1, cmp_k_w2, cmp_v_pos, cmp_v_w1, cmp_v_b1,
                    cmp_v_w2, gla_w_alpha, gla_b_alpha, gla_norm_g, w_branch_nsa, w_branch_gla, w_out, norm_ffn_g,
                    w_router_group, b_router_group, w_router_expert, b_router_expert, w_exp_gate, w_exp_up,
                    w_exp_down, norm_final_g)
```

```python
import functools

import jax
import jax.numpy as jnp
from jax import lax
from jax.experimental import pallas as pl
from jax.experimental.pallas import tpu as pltpu

F32 = jnp.float32
BF16 = jnp.bfloat16
I32 = jnp.int32

NSA_HEAD_DIM = 128
NSA_KV_HEADS = 4
NSA_GROUP = 4
NSA_HEADS = NSA_KV_HEADS * NSA_GROUP
CMP_BLOCK = 32
CMP_STRIDE = 16
SEL_BLOCK = 64
SEL_TOPN = 16
WINDOW = 512
BIG = 1e9
GLA_HEADS = 16
GLA_DK = 64
GLA_DV = 128
GLA_GATE_RANK = 16
GLA_TAU = 16.0
N_GROUPS = 4
EXPERTS_PER_GROUP = 8
N_EXPERTS = N_GROUPS * EXPERTS_PER_GROUP
NORM_EPS = 1e-6

LANES = 128
VMEM_LIMIT = 56 * 1024 * 1024
NEG = -1e30

NORM_ROWS = 512
PROJ_TM, PROJ_TN = 1024, 512
NSA_TQ, NSA_TK = 128, 256
GLA_TS, GLA_C, GLA_SUB = 512, 64, 16
MIX_TM, MIX_TN = 1024, 1024
OUT_TM, OUT_TN = 1024, 1024
ROUTE_TM = 256
MOE_TM = 512
MOE_CJ = 512
MOE_CN = 2048
DISPATCH_TOKENS = 512
COMBINE_ROWS = 256


def _cparams(*sem):
    return pltpu.CompilerParams(dimension_semantics=sem, vmem_limit_bytes=VMEM_LIMIT)


def _dot(a, b, **kw):
    return jnp.dot(a, b, preferred_element_type=F32, **kw)


def _dot_nt(a, b, **kw):
    return lax.dot_general(a, b, (((1,), (1,)), ((), ())), preferred_element_type=F32, **kw)


def _dot_tn(a, b, **kw):
    return lax.dot_general(a, b, (((0,), (0,)), ((), ())), preferred_element_type=F32, **kw)


def _sigmoid(x):
    return 1.0 / (1.0 + jnp.exp(-x))


def _masked_softmax(s, mask):
    sm = jnp.where(mask, s, NEG)
    m = jnp.max(sm, axis=-1, keepdims=True)
    e = jnp.where(mask, jnp.exp(sm - m), 0.0)
    return e * (1.0 / jnp.maximum(jnp.sum(e, axis=-1, keepdims=True), 1e-30))


def _rmsnorm_kernel(x_ref, g_ref, o_ref):
    x = x_ref[...]
    ms = jnp.mean(x * x, axis=-1, keepdims=True)
    o_ref[...] = (x * lax.rsqrt(ms + NORM_EPS) * g_ref[...]).astype(o_ref.dtype)


def rmsnorm(x, g, out_dtype=BF16, rows=NORM_ROWS):
    T, D = x.shape
    rows = min(rows, T)
    return pl.pallas_call(
        _rmsnorm_kernel,
        out_shape=jax.ShapeDtypeStruct((T, D), out_dtype),
        grid=(T // rows,),
        in_specs=[pl.BlockSpec((rows, D), lambda i: (i, 0)),
                  pl.BlockSpec((1, D), lambda i: (0, 0))],
        out_specs=pl.BlockSpec((rows, D), lambda i: (i, 0)),
        compiler_params=_cparams("parallel"),
        name="rmsnorm",
    )(x, g.reshape(1, D))


def _mm_kernel(a_ref, b_ref, o_ref):
    o_ref[...] = _dot(a_ref[...], b_ref[...]).astype(o_ref.dtype)


def matmul(a, b, out_dtype, tm, tn, name):
    M, K = a.shape
    N = b.shape[1]
    tm, tn = min(tm, M), min(tn, N)
    return pl.pallas_call(
        _mm_kernel,
        out_shape=jax.ShapeDtypeStruct((M, N), out_dtype),
        grid=(M // tm, N // tn),
        in_specs=[pl.BlockSpec((tm, K), lambda i, j: (i, 0)),
                  pl.BlockSpec((K, tn), lambda i, j: (0, j))],
        out_specs=pl.BlockSpec((tm, tn), lambda i, j: (i, j)),
        compiler_params=_cparams("parallel", "arbitrary"),
        name=name,
    )(a, b)


def _compress_kernel(seq_ref, pos_ref, w1_ref, b1_ref, w2_ref, o_ref):
    seq = seq_ref[0]
    nc = seq.shape[0] // CMP_STRIDE
    x = seq.reshape(nc, CMP_STRIDE * seq.shape[1]).astype(F32)
    half = x.shape[1]
    pos = pos_ref[0]
    w1 = w1_ref[0]
    u0 = _dot((x + pos[0:1, :]).astype(BF16), w1[:half, :])
    u1 = _dot((x + pos[1:2, :]).astype(BF16), w1[half:, :])
    pre = u0 + pltpu.roll(u1, nc - 1, 0) + b1_ref[0]
    h = 0.5 * pre * (1.0 + jnp.tanh(0.7978845608028654 * (pre + 0.044715 * pre * pre * pre)))
    o_ref[0, 0, 0] = _dot(h.astype(BF16), w2_ref[0]).astype(o_ref.dtype)


def compress(proj, col_kc, pos2, w1, b1, w2):
    B, S, _ = proj.shape
    G = NSA_KV_HEADS
    Dh = w2.shape[-1]
    NC, HW = S // CMP_STRIDE, CMP_STRIDE * Dh
    return pl.pallas_call(
        _compress_kernel,
        out_shape=jax.ShapeDtypeStruct((2, B, G, NC, Dh), BF16),
        grid=(2, B, G),
        in_specs=[pl.BlockSpec((1, S, Dh), lambda a, b, g: (b, 0, col_kc + a * G + g)),
                  pl.BlockSpec((1, 2, HW), lambda a, b, g: (a, 0, 0)),
                  pl.BlockSpec((1, 2 * HW, Dh), lambda a, b, g: (a, 0, 0)),
                  pl.BlockSpec((1, 1, Dh), lambda a, b, g: (a, 0, 0)),
                  pl.BlockSpec((1, Dh, Dh), lambda a, b, g: (a, 0, 0))],
        out_specs=pl.BlockSpec((1, 1, 1, NC, Dh), lambda a, b, g: (a, b, g, 0, 0)),
        compiler_params=_cparams("parallel", "parallel", "parallel"),
        name="nsa_compress",
    )(proj, pos2, w1, b1, w2)


MASK_BIG = 2.0 ** 100


def _nsa_kernel(slopes_ref, q_ref, kc_ref, vc_ref, ks_ref, vs_ref, kw_ref, vw_ref, g_ref,
                paux_ref, saux_ref, caux_ref, mmat_ref, o_ref,
                qx_ref, sc_ref, sw_ref, sa_ref, sb_ref, score_ref, m_ref, acc_ref,
                *, TQ, TK, NC, NSEL, NTOP):
    R, Dh = NSA_GROUP, NSA_HEAD_DIM
    g = pl.program_id(1)
    q0 = pl.program_id(2) * TQ
    scale = Dh ** -0.5
    lane = lax.broadcasted_iota(I32, (1, LANES), 1)
    t1 = q0 + lax.broadcasted_iota(I32, (TQ, 1), 0)
    head = lambda r: slice(r * TQ, (r + 1) * TQ)

    def alibi_cols(r):
        hi, lo = slopes_ref[0, g * R + r], slopes_ref[1, g * R + r]
        c = jnp.where(lane == 0, 64.0 * hi, jnp.where(lane == 1, 64.0 * lo,
                                                      jnp.where(lane == 2, hi, jnp.where(lane == 3, lo, 0.0))))
        return jnp.broadcast_to(c, (TQ, LANES))

    for r in range(R):
        qx_ref[head(r), 0:Dh] = (q_ref[0, :, r * Dh:(r + 1) * Dh].astype(F32) * scale).astype(BF16)
        qx_ref[head(r), Dh:Dh + LANES] = alibi_cols(r).astype(BF16)

    def with_ones(v):
        return jnp.concatenate([v, jnp.ones(v.shape, v.dtype)], axis=1)

    def exp_pv(s, v1):
        m = jnp.broadcast_to(jnp.max(s, axis=-1, keepdims=True), (TQ, LANES))
        e = [jnp.exp(s[:, c * LANES:(c + 1) * LANES] - m) for c in range(s.shape[1] // LANES)]
        return e, _dot(jnp.concatenate(e, axis=1).astype(BF16), v1)

    WK = WINDOW + TQ
    ws = pl.multiple_of(jnp.maximum(q0 - WINDOW, 0), TQ)
    sc_ref[...] = _dot_nt(qx_ref[...], jnp.concatenate([kc_ref[0, 0, 0], caux_ref[...]], axis=1))
    sw_ref[...] = _dot_nt(qx_ref[...], jnp.concatenate([kw_ref[0, pl.ds(ws, WK), :], paux_ref[pl.ds(ws, WK), :]],
                                                       axis=1))

    cmp_end = lax.broadcasted_iota(I32, (1, NC), 1) * CMP_STRIDE + (CMP_BLOCK - 1)
    cmp_bias = jnp.where(cmp_end <= t1, 0.0, NEG)
    row_ok = q0 + lax.broadcasted_iota(I32, (TQ, LANES), 0) >= CMP_BLOCK - 1
    vc1 = with_ones(vc_ref[0, 0, 0])
    o_cmp = []
    imp = None
    for r in range(R):
        e, o2 = exp_pv(sc_ref[head(r), :] + cmp_bias, vc1)
        inv = jnp.where(row_ok, 1.0 / jnp.maximum(o2[:, Dh:], 1e-30), 0.0)
        o_cmp.append(o2[:, :Dh] * inv)
        p = jnp.concatenate([ec * inv for ec in e], axis=1)
        imp = p if imp is None else imp + p

    vw1 = with_ones(vw_ref[0, pl.ds(ws, WK), :])
    dw = t1 - (ws + lax.broadcasted_iota(I32, (1, WK), 1))
    win_bias = jnp.where((dw >= 0) & (dw < WINDOW), 0.0, NEG)
    o_win = []
    for r in range(R):
        _, o2 = exp_pv(sw_ref[head(r), :] + win_bias, vw1)
        o_win.append(o2[:, :Dh] * (1.0 / jnp.maximum(o2[:, Dh:], 1e-30)))

    mm = mmat_ref[...]
    i1 = imp.astype(BF16)
    rem = imp - i1.astype(F32)
    i2 = rem.astype(BF16)
    i3 = (rem - i2.astype(F32)).astype(BF16)
    imp_sel = _dot_nt(mm, i1) + _dot_nt(mm, i2) + _dot_nt(mm, i3)

    cur = (q0 + lax.broadcasted_iota(I32, (1, TQ), 1)) // SEL_BLOCK
    blk = lax.broadcasted_iota(I32, (NSEL, 1), 0)
    forced = (blk == 0) | (blk == cur) | (blk == cur - 1)
    score = jnp.where(blk <= cur, jnp.where(forced, BIG, imp_sel), -BIG)
    score_ref[...] = score

    def rank_pair(i2, cnt):
        for d in range(2):
            ii = 2 * i2 + d
            other = score_ref[pl.ds(ii, 1), :]
            beats = (other > score) | ((other == score) & (blk > ii))
            cnt = cnt + jnp.where(beats, 1.0, 0.0)
        return cnt

    n_blk = (q0 + TQ) // SEL_BLOCK
    cnt = lax.fori_loop(0, n_blk // 2, rank_pair, jnp.zeros((NSEL, TQ), F32))
    unsel = jnp.where(cnt < NTOP, 0.0, -1.0).astype(BF16)
    place = jnp.where(lax.broadcasted_iota(I32, (NSEL, LANES), 1)
                      == lax.broadcasted_iota(I32, (NSEL, LANES), 0) + LANES // 2, 1.0, 0.0).astype(BF16)
    unsel_l = _dot_tn(unsel, place)
    for r in range(R):
        qx_ref[head(r), Dh:Dh + LANES] = (alibi_cols(r) + unsel_l).astype(BF16)

    m_ref[...] = jnp.full(m_ref.shape, NEG, F32)
    acc_ref[...] = jnp.zeros(acc_ref.shape, F32)
    pos_in_tile = lax.broadcasted_iota(I32, (1, TK), 1)

    def issue_scores(kt, s_ref):
        k0 = pl.multiple_of(kt * TK, TK)
        kk = jnp.concatenate([ks_ref[0, pl.ds(k0, TK), :], saux_ref[pl.ds(k0, TK), :]], axis=1)
        s_ref[...] = _dot_nt(qx_ref[...], kk)

    def consume_scores(kt, s_ref, causal):
        k0 = pl.multiple_of(kt * TK, TK)
        vv1 = with_ones(vs_ref[0, pl.ds(k0, TK), :])
        if causal:
            causal_bias = jnp.where(t1 >= k0 + pos_in_tile, 0.0, -MASK_BIG)
        for r in range(R):
            sc = s_ref[head(r), :]
            if causal:
                sc = sc + causal_bias
            m_old = m_ref[r]
            m_new = jnp.maximum(m_old, jnp.broadcast_to(jnp.max(sc, axis=-1, keepdims=True), (TQ, LANES)))
            alpha = jnp.exp(m_old - m_new)
            p = jnp.concatenate([jnp.exp(sc[:, c * LANES:(c + 1) * LANES] - m_new) for c in range(TK // LANES)],
                                axis=1).astype(BF16)
            m_ref[r] = m_new
            acc_ref[r] = jnp.concatenate([alpha, alpha], axis=1) * acc_ref[r] + _dot(p, vv1)

    def sel_pair(k, carry):
        issue_scores(2 * k + 1, sb_ref)
        consume_scores(2 * k, sa_ref, False)
        issue_scores(2 * k + 2, sa_ref)
        consume_scores(2 * k + 1, sb_ref, False)
        return carry

    n_below = q0 // TK
    issue_scores(0, sa_ref)
    lax.fori_loop(0, n_below // 2, sel_pair, 0)

    @pl.when(n_below % 2 == 1)
    def _():
        issue_scores(n_below, sb_ref)
        consume_scores(n_below - 1, sa_ref, False)
        consume_scores(n_below, sb_ref, True)

    @pl.when(n_below % 2 == 0)
    def _():
        consume_scores(n_below, sa_ref, True)

    gt = _sigmoid(g_ref[0, 0])
    for r in range(R):
        o_sel = acc_ref[r, :, 0:Dh] * (1.0 / jnp.maximum(acc_ref[r, :, Dh:2 * Dh], 1e-30))
        o = (gt[:, 3 * r:3 * r + 1] * o_cmp[r] + gt[:, 3 * r + 1:3 * r + 2] * o_sel
             + gt[:, 3 * r + 2:3 * r + 3] * o_win[r])
        o_ref[0, :, r * Dh:(r + 1) * Dh] = o.astype(o_ref.dtype)


def _nsa_constants(S, NC, NSEL):
    def pos_cols(pos):
        col = jnp.arange(LANES)[None, :]
        hi, lo = (pos // 64)[:, None], (pos % 64)[:, None]
        return jnp.where(col < 2, hi, jnp.where(col < 4, lo, 0)).astype(F32)

    pos = jnp.arange(S)
    paux = pos_cols(pos)
    onehot = (jnp.arange(LANES)[None, :] - LANES // 2 == (pos // SEL_BLOCK)[:, None]) & (jnp.arange(LANES)[None, :] >= LANES // 2)
    saux = paux + jnp.where(onehot, MASK_BIG, 0.0)
    caux = pos_cols(jnp.arange(NC) * CMP_STRIDE + (CMP_BLOCK - 1))
    d = jnp.arange(NC)[None, :] - 4 * jnp.arange(NSEL)[:, None]
    mmat = jnp.where((d == -1) | (d == 3), 1.0, jnp.where((d >= 0) & (d <= 2), 2.0, 0.0))
    return paux.astype(BF16), saux.astype(BF16), caux.astype(BF16), mmat.astype(BF16)


def nsa_attention(proj, kvc, gates, slopes, *, col_q, col_ks, col_vs, col_kw, col_vw):
    B, S, _ = proj.shape
    G, R, Dh = NSA_KV_HEADS, NSA_GROUP, NSA_HEAD_DIM
    NC = kvc.shape[3]
    NSEL = S // SEL_BLOCK
    TQ, TK = min(NSA_TQ, S), min(NSA_TK, S)
    assert S % TQ == 0 and TK % TQ == 0 and S % TK == 0 and S >= WINDOW + TQ and NC == S // CMP_STRIDE
    assert NSEL <= LANES // 2 and S // 64 <= 256 and TQ % (2 * SEL_BLOCK) == 0
    s_hi = slopes.astype(BF16).astype(F32)
    s_lo = (slopes - s_hi).astype(BF16).astype(F32)
    paux, saux, caux, mmat = _nsa_constants(S, NC, NSEL)
    kern = functools.partial(_nsa_kernel, TQ=TQ, TK=TK, NC=NC, NSEL=NSEL, NTOP=min(SEL_TOPN, NSEL))
    kv_spec = lambda col: pl.BlockSpec((1, S, Dh), lambda b, g, i: (b, 0, col + g))
    whole = lambda a: pl.BlockSpec(a.shape, lambda b, g, i: (0, 0))
    return pl.pallas_call(
        kern,
        out_shape=jax.ShapeDtypeStruct((B, S, G * R * Dh), BF16),
        grid=(B, G, S // TQ),
        in_specs=[pl.BlockSpec(memory_space=pltpu.SMEM),
                  pl.BlockSpec((1, TQ, R * Dh), lambda b, g, i: (b, i, col_q // R + g)),
                  pl.BlockSpec((1, 1, 1, NC, Dh), lambda b, g, i: (0, b, g, 0, 0)),
                  pl.BlockSpec((1, 1, 1, NC, Dh), lambda b, g, i: (1, b, g, 0, 0)),
                  kv_spec(col_ks), kv_spec(col_vs), kv_spec(col_kw), kv_spec(col_vw),
                  pl.BlockSpec((1, 1, TQ, 3 * R), lambda b, g, i: (b, g, i, 0)),
                  whole(paux), whole(saux), whole(caux), whole(mmat)],
        out_specs=pl.BlockSpec((1, TQ, R * Dh), lambda b, g, i: (b, i, g)),
        scratch_shapes=[pltpu.VMEM((R * TQ, Dh + LANES), BF16), pltpu.VMEM((R * TQ, NC), F32),
                        pltpu.VMEM((R * TQ, WINDOW + TQ), F32), pltpu.VMEM((R * TQ, TK), F32),
                        pltpu.VMEM((R * TQ, TK), F32),
                        pltpu.VMEM((NSEL, TQ), F32), pltpu.VMEM((R, TQ, LANES), F32),
                        pltpu.VMEM((R, TQ, 2 * Dh), F32)],
        compiler_params=_cparams("parallel", "parallel", "arbitrary"),
        name="nsa_attention",
    )(jnp.stack([s_hi, s_lo]), proj, kvc, kvc, proj, proj, proj, proj, gates, paux, saux, caux, mmat)


def _gla_kernel(q_ref, k_ref, v_ref, r_ref, xa_ref, wa_ref, ba_ref, ng_ref, o_ref, st_ref, la_ref, *, TS, C):
    DK2, DV2 = 2 * GLA_DK, 2 * GLA_DV

    @pl.when(pl.program_id(2) == 0)
    def _():
        st_ref[...] = jnp.zeros_like(st_ref)

    z = _dot(xa_ref[0], wa_ref[...], precision=lax.Precision.HIGHEST) + ba_ref[...]
    la_ref[...] = (jnp.minimum(z, 0.0) - jnp.log(1.0 + jnp.exp(-jnp.abs(z)))) * (1.0 / GLA_TAU)

    lane = lax.broadcasted_iota(I32, (1, DK2), 1)
    head_a = lane < GLA_DK
    row = lax.broadcasted_iota(I32, (C, 1), 0)
    col = lax.broadcasted_iota(I32, (1, C), 1)
    tri = jnp.where(col <= row, 1.0, 0.0).astype(BF16)
    LEVELS = [C >> (i + 1) for i in range(C.bit_length() - 1)]
    level_masks = [(row // (2 * h) == col // (2 * h)) & (row % (2 * h) >= h) & (col % (2 * h) < h) for h in LEVELS]
    eye = row == col
    vk_same = (lax.broadcasted_iota(I32, (DV2, DK2), 0) < GLA_DV) == (lax.broadcasted_iota(I32, (DV2, DK2), 1) < GLA_DK)

    def chunk(c, carry):
        r0 = pl.multiple_of(c * C, C)
        q = q_ref[0, pl.ds(r0, C), :].astype(F32) * (GLA_DK ** -0.5)
        k = k_ref[0, pl.ds(r0, C), :].astype(F32)
        vb = v_ref[0, pl.ds(r0, C), :]
        la = la_ref[pl.ds(r0, C), :]
        l1 = la.astype(BF16)
        lr = la - l1.astype(F32)
        l2 = lr.astype(BF16)
        l3 = (lr - l2.astype(F32)).astype(BF16)
        b = _dot(tri, l1) + _dot(tri, l2) + _dot(tri, l3)

        def ref_rows(h):
            if h >= 4:
                g = C // (2 * h)
                return jnp.broadcast_to(b.reshape(g, 2 * h, DK2)[:, h - 1:h, :], (g, 2 * h, DK2)).reshape(C, DK2)
            dn1 = pltpu.roll(b, 1, 0)
            if h == 1:
                return jnp.where(row % 2 == 0, b, dn1)
            m4 = row % 4
            return jnp.where(m4 == 0, pltpu.roll(b, C - 1, 0),
                             jnp.where(m4 == 1, b, jnp.where(m4 == 2, dn1, pltpu.roll(b, 2, 0))))

        kb = k.astype(BF16)
        a_a = jnp.where(eye, _dot_nt(jnp.where(head_a, q, 0.0).astype(BF16), kb), 0.0)
        a_b = jnp.where(eye, _dot_nt(jnp.where(head_a, 0.0, q).astype(BF16), kb), 0.0)
        for h, mask_h in zip(LEVELS, level_masks):
            r = ref_rows(h)
            qh = q * jnp.exp(jnp.minimum(b - r, 0.0))
            kh = (k * jnp.exp(jnp.minimum(r - b, 0.0))).astype(BF16)
            a_a = a_a + jnp.where(mask_h, _dot_nt(jnp.where(head_a, qh, 0.0).astype(BF16), kh), 0.0)
            a_b = a_b + jnp.where(mask_h, _dot_nt(jnp.where(head_a, 0.0, qh).astype(BF16), kh), 0.0)
        o = jnp.concatenate([_dot(a_a.astype(BF16), vb[:, :GLA_DV]),
                             _dot(a_b.astype(BF16), vb[:, GLA_DV:])], axis=1)

        st = st_ref[...]
        o = o + _dot_nt((q * jnp.exp(b)).astype(BF16), st.astype(BF16))
        bl = b[C - 1:C, :]
        kd = (k * jnp.exp(bl - b)).astype(BF16)
        ds = jnp.where(vk_same, _dot_tn(vb, kd), 0.0)
        st_ref[...] = st * jnp.exp(bl) + ds

        def ln(x):
            mu = jnp.mean(x, axis=-1, keepdims=True)
            xc = x - mu
            return xc * lax.rsqrt(jnp.mean(xc * xc, axis=-1, keepdims=True) + NORM_EPS)

        on = jnp.concatenate([ln(o[:, :GLA_DV]), ln(o[:, GLA_DV:])], axis=1) * ng_ref[...]
        rr = r_ref[0, pl.ds(r0, C), :].astype(F32)
        o_ref[0, pl.ds(r0, C), :] = (on * (rr * _sigmoid(rr))).astype(o_ref.dtype)
        return carry

    lax.fori_loop(0, TS // C, chunk, 0, unroll=8)


def gla_attention(proj, xa, w_alpha, b_alpha, norm_g, *, col_q, col_k, col_v, col_r):
    B, S, _ = proj.shape
    HP = GLA_HEADS // 2
    DK2, DV2 = 2 * GLA_DK, 2 * GLA_DV
    TS, C = min(GLA_TS, S), GLA_C
    assert S % TS == 0 and TS % C == 0 and C == 64
    kern = functools.partial(_gla_kernel, TS=TS, C=C)
    return pl.pallas_call(
        kern,
        out_shape=jax.ShapeDtypeStruct((B, S, GLA_HEADS * GLA_DV), BF16),
        grid=(B, HP, S // TS),
        in_specs=[pl.BlockSpec((1, TS, DK2), lambda b, p, s: (b, s, col_q + p)),
                  pl.BlockSpec((1, TS, DK2), lambda b, p, s: (b, s, col_k + p)),
                  pl.BlockSpec((1, TS, DV2), lambda b, p, s: (b, s, col_v // 2 + p)),
                  pl.BlockSpec((1, TS, DV2), lambda b, p, s: (b, s, col_r // 2 + p)),
                  pl.BlockSpec((1, TS, GLA_GATE_RANK), lambda b, p, s: (b, s, 0)),
                  pl.BlockSpec((GLA_GATE_RANK, DK2), lambda b, p, s: (0, p)),
                  pl.BlockSpec((1, DK2), lambda b, p, s: (0, p)),
                  pl.BlockSpec((1, DV2), lambda b, p, s: (0, p))],
        out_specs=pl.BlockSpec((1, TS, DV2), lambda b, p, s: (b, s, p)),
        scratch_shapes=[pltpu.VMEM((DV2, DK2), F32), pltpu.VMEM((TS, DK2), F32)],
        compiler_params=_cparams("parallel", "parallel", "arbitrary"),
        name="gla_attention",
    )(proj, proj, proj, proj, xa, w_alpha, b_alpha.reshape(1, -1), norm_g.reshape(1, -1))


def _mix_kernel(oa_ref, ob_ref, wa_ref, wb_ref, ma_ref, mb_ref, o_ref):
    ya = _dot(oa_ref[...], wa_ref[...])
    yb = _dot(ob_ref[...], wb_ref[...])
    o_ref[...] = (_sigmoid(ma_ref[...].astype(F32)) * ya + _sigmoid(mb_ref[...].astype(F32)) * yb).astype(o_ref.dtype)


def gated_mix(o_nsa, o_gla, wa, wb, proj2d, col_ma, col_mb):
    T, KA = o_nsa.shape
    KB = o_gla.shape[1]
    N = wa.shape[1]
    tm, tn = min(MIX_TM, T), min(MIX_TN, N)
    ca, cb = col_ma * LANES // tn, col_mb * LANES // tn
    assert (col_ma * LANES) % tn == 0 and (col_mb * LANES) % tn == 0
    return pl.pallas_call(
        _mix_kernel,
        out_shape=jax.ShapeDtypeStruct((T, N), BF16),
        grid=(T // tm, N // tn),
        in_specs=[pl.BlockSpec((tm, KA), lambda i, j: (i, 0)),
                  pl.BlockSpec((tm, KB), lambda i, j: (i, 0)),
                  pl.BlockSpec((KA, tn), lambda i, j: (0, j)),
                  pl.BlockSpec((KB, tn), lambda i, j: (0, j)),
                  pl.BlockSpec((tm, tn), lambda i, j: (i, ca + j)),
                  pl.BlockSpec((tm, tn), lambda i, j: (i, cb + j))],
        out_specs=pl.BlockSpec((tm, tn), lambda i, j: (i, j)),
        compiler_params=_cparams("parallel", "arbitrary"),
        name="gated_mix",
    )(o_nsa, o_gla, wa, wb, proj2d, proj2d)


def _out_kernel(a_ref, w_ref, x_ref, o_ref):
    o_ref[...] = x_ref[...] + _dot(a_ref[...], w_ref[...])


def out_proj(mixed, w, x):
    T, K = mixed.shape
    N = w.shape[1]
    tm, tn = min(OUT_TM, T), min(OUT_TN, N)
    return pl.pallas_call(
        _out_kernel,
        out_shape=jax.ShapeDtypeStruct((T, N), F32),
        grid=(T // tm, N // tn),
        in_specs=[pl.BlockSpec((tm, K), lambda i, j: (i, 0)),
                  pl.BlockSpec((K, tn), lambda i, j: (0, j)),
                  pl.BlockSpec((tm, tn), lambda i, j: (i, j))],
        out_specs=pl.BlockSpec((tm, tn), lambda i, j: (i, j)),
        compiler_params=_cparams("parallel", "arbitrary"),
        name="out_proj",
    )(mixed, w, x)


def _router_kernel(h_ref, g_ref, w_ref, b_ref, id_ref, wt_ref, u_ref):
    x = h_ref[...]
    u = x * lax.rsqrt(jnp.mean(x * x, axis=-1, keepdims=True) + NORM_EPS) * g_ref[...]
    u_hi = u.astype(BF16)
    u_ref[...] = u_hi.reshape(u_ref.shape)
    u_lo = (u - u_hi.astype(F32)).astype(BF16)
    logit = _dot(u_hi, w_ref[0]) + _dot(u_lo, w_ref[0]) + _dot(u_hi, w_ref[1]) + b_ref[...]
    lane = lax.broadcasted_iota(I32, logit.shape, 1)
    big = jnp.int32(1 << 20)
    gmask = lane < N_GROUPS
    gl = jnp.where(gmask, logit, NEG)
    gmax = jnp.max(gl, axis=-1, keepdims=True)
    g_star = jnp.min(jnp.where(gmask & (gl == gmax), lane, big), axis=-1, keepdims=True)
    p_group = 1.0 / jnp.sum(jnp.where(gmask, jnp.exp(gl - gmax), 0.0), axis=-1, keepdims=True)
    e_lo = N_GROUPS + EXPERTS_PER_GROUP * g_star
    emask = (lane >= e_lo) & (lane < e_lo + EXPERTS_PER_GROUP)
    el = jnp.where(emask, logit, NEG)
    m1 = jnp.max(el, axis=-1, keepdims=True)
    i1 = jnp.min(jnp.where(emask & (el == m1), lane, big), axis=-1, keepdims=True)
    emask2 = emask & (lane != i1)
    el2 = jnp.where(emask2, logit, NEG)
    m2 = jnp.max(el2, axis=-1, keepdims=True)
    i2 = jnp.min(jnp.where(emask2 & (el2 == m2), lane, big), axis=-1, keepdims=True)
    e2 = jnp.exp(m2 - m1)
    w1 = p_group / (1.0 + e2)
    w2 = p_group * e2 / (1.0 + e2)
    id_ref[...] = jnp.where(lane == 0, i1 - N_GROUPS, jnp.where(lane == 1, i2 - N_GROUPS, 0))
    wt_ref[...] = jnp.where(lane == 0, w1, jnp.where(lane == 1, w2, 0.0))


def router(h, g, w_r, b_r):
    T, D = h.shape
    tm = min(ROUTE_TM, T)
    w_hi = w_r.astype(BF16)
    return pl.pallas_call(
        _router_kernel,
        out_shape=(jax.ShapeDtypeStruct((T, LANES), I32), jax.ShapeDtypeStruct((T, LANES), F32),
                   jax.ShapeDtypeStruct((T, D // LANES, LANES), BF16)),
        grid=(T // tm,),
        in_specs=[pl.BlockSpec((tm, D), lambda i: (i, 0)),
                  pl.BlockSpec((1, D), lambda i: (0, 0)),
                  pl.BlockSpec((2, D, LANES), lambda i: (0, 0, 0)),
                  pl.BlockSpec((1, LANES), lambda i: (0, 0))],
        out_specs=(pl.BlockSpec((tm, LANES), lambda i: (i, 0)), pl.BlockSpec((tm, LANES), lambda i: (i, 0)),
                   pl.BlockSpec((tm, D // LANES, LANES), lambda i: (i, 0, 0))),
        compiler_params=_cparams("parallel"),
        name="moe_router",
    )(h, g.reshape(1, D), jnp.stack([w_hi, (w_r - w_hi.astype(F32)).astype(BF16)]), b_r)


def _row_copy(src_ref, dst_ref, sem, src_row, dst_row):
    return pltpu.make_async_copy(src_ref.at[pl.ds(src_row, 1)], dst_ref.at[pl.ds(dst_row, 1)], sem)


def _dispatch_rows_kernel(dest_ref, u_ref, init_hbm, out_hbm, sem, *, toks):
    def start(t, c):
        for k in range(2):
            _row_copy(u_ref, out_hbm, sem, t, dest_ref[0, 0, 2 * t + k]).start(priority=k)
        return c

    def wait(t, c):
        for k in range(2):
            _row_copy(u_ref, out_hbm, sem, t, 0).wait()
        return c

    lax.fori_loop(0, toks, start, 0)
    lax.fori_loop(0, toks, wait, 0)


def dispatch_rows(u3, dest, n_rows):
    T = u3.shape[0]
    toks = min(DISPATCH_TOKENS, T)
    assert T % toks == 0
    out_shape = jax.ShapeDtypeStruct((n_rows,) + u3.shape[1:], u3.dtype)
    return pl.pallas_call(
        functools.partial(_dispatch_rows_kernel, toks=toks),
        out_shape=out_shape,
        grid=(T // toks,),
        in_specs=[pl.BlockSpec((1, 1, 2 * toks), lambda i: (i, 0, 0), memory_space=pltpu.SMEM),
                  pl.BlockSpec((toks,) + u3.shape[1:], lambda i: (i, 0, 0)),
                  pl.BlockSpec(memory_space=pl.ANY)],
        out_specs=pl.BlockSpec(memory_space=pl.ANY),
        scratch_shapes=[pltpu.SemaphoreType.DMA],
        input_output_aliases={2: 0},
        compiler_params=_cparams("arbitrary"),
        name="moe_dispatch_rows",
    )(dest.reshape(T // toks, 1, 2 * toks), u3, jnp.zeros(out_shape.shape, out_shape.dtype))


def _ffn_up_kernel(te_ref, new_ref, nv_ref, x_ref, wg_ref, wu_ref, o_ref, wg_bf, wu_bf):
    i = pl.program_id(1)

    @pl.when(new_ref[i] == 1)
    def _():
        wg_bf[...] = wg_ref[0].astype(BF16)
        wu_bf[...] = wu_ref[0].astype(BF16)

    @pl.when(i < nv_ref[0])
    def _():
        x = x_ref[...].reshape(x_ref.shape[0], wg_bf.shape[0])
        a = _dot(x, wg_bf[...])
        u = _dot(x, wu_bf[...])
        o_ref[...] = (a * _sigmoid(a) * u).astype(o_ref.dtype)

    @pl.when(i >= nv_ref[0])
    def _():
        o_ref[...] = jnp.zeros_like(o_ref)


def ffn_up(x_sorted, tables, w_gate, w_up):
    NP = x_sorted.shape[0]
    E, D, DE = w_gate.shape
    tm, cj = MOE_TM, min(MOE_CJ, DE)
    return pl.pallas_call(
        _ffn_up_kernel,
        out_shape=jax.ShapeDtypeStruct((NP, DE), BF16),
        grid_spec=pltpu.PrefetchScalarGridSpec(
            num_scalar_prefetch=3,
            grid=(DE // cj, NP // tm),
            in_specs=[pl.BlockSpec((tm,) + x_sorted.shape[1:], lambda j, i, te, nw, nv: (i, 0, 0)),
                      pl.BlockSpec((1, D, cj), lambda j, i, te, nw, nv: (te[i], 0, j)),
                      pl.BlockSpec((1, D, cj), lambda j, i, te, nw, nv: (te[i], 0, j))],
            out_specs=pl.BlockSpec((tm, cj), lambda j, i, te, nw, nv: (i, j)),
            scratch_shapes=[pltpu.VMEM((D, cj), BF16), pltpu.VMEM((D, cj), BF16)]),
        compiler_params=_cparams("arbitrary", "arbitrary"),
        name="moe_ffn_up",
    )(*tables, x_sorted, w_gate, w_up)


def _ffn_down_kernel(te_ref, new_ref, nv_ref, h_ref, wd_ref, o_ref, wd_bf):
    i = pl.program_id(1)

    @pl.when(new_ref[i] == 1)
    def _():
        wd_bf[...] = wd_ref[0].astype(BF16)

    @pl.when(i < nv_ref[0])
    def _():
        o_ref[...] = _dot(h_ref[...], wd_bf[...]).astype(o_ref.dtype).reshape(o_ref.shape)

    @pl.when(i >= nv_ref[0])
    def _():
        o_ref[...] = jnp.zeros_like(o_ref)


def ffn_down(h_sorted, tables, w_down):
    NP, DE = h_sorted.shape
    E, _, D = w_down.shape
    tm, cn = MOE_TM, min(MOE_CN, D)
    return pl.pallas_call(
        _ffn_down_kernel,
        out_shape=jax.ShapeDtypeStruct((NP, D // LANES, LANES), BF16),
        grid_spec=pltpu.PrefetchScalarGridSpec(
            num_scalar_prefetch=3,
            grid=(D // cn, NP // tm),
            in_specs=[pl.BlockSpec((tm, DE), lambda j, i, te, nw, nv: (i, 0)),
                      pl.BlockSpec((1, DE, cn), lambda j, i, te, nw, nv: (te[i], 0, j))],
            out_specs=pl.BlockSpec((tm, cn // LANES, LANES), lambda j, i, te, nw, nv: (i, j, 0)),
            scratch_shapes=[pltpu.VMEM((DE, cn), BF16)]),
        compiler_params=_cparams("arbitrary", "arbitrary"),
        name="moe_ffn_down",
    )(*tables, h_sorted, w_down)


def _combine_kernel(idx_ref, nxt_ref, h_ref, wt_ref, y_hbm, g_ref, o_ref, buf, sem, *, rows):
    i = pl.program_id(0)
    slot = i % 2

    def issue(table, s):
        def body(r, c):
            _row_copy(y_hbm, buf.at[s, 0], sem.at[s], table[0, 0, 2 * r], r).start(priority=0)
            _row_copy(y_hbm, buf.at[s, 1], sem.at[s], table[0, 0, 2 * r + 1], r).start(priority=1)
            return c
        lax.fori_loop(0, rows, body, 0)

    @pl.when(i == 0)
    def _():
        issue(idx_ref, 0)

    @pl.when(i + 1 < pl.num_programs(0))
    def _():
        issue(nxt_ref, 1 - slot)

    def wait(r, c):
        _row_copy(y_hbm, buf.at[slot, 0], sem.at[slot], 0, r).wait()
        _row_copy(y_hbm, buf.at[slot, 1], sem.at[slot], 0, r).wait()
        return c

    lax.fori_loop(0, rows, wait, 0)
    wt = wt_ref[...]
    y0 = buf[slot, 0].reshape(h_ref.shape).astype(F32)
    y1 = buf[slot, 1].reshape(h_ref.shape).astype(F32)
    x = h_ref[...] + wt[:, 0:1] * y0 + wt[:, 1:2] * y1
    o_ref[...] = x * lax.rsqrt(jnp.mean(x * x, axis=-1, keepdims=True) + NORM_EPS) * g_ref[...]


def combine_norm(h, wts, dest, y3, g):
    T, D = h.shape
    rows = min(COMBINE_ROWS, T)
    n = T // rows
    table = dest.reshape(n, 1, 2 * rows)
    return pl.pallas_call(
        functools.partial(_combine_kernel, rows=rows),
        out_shape=jax.ShapeDtypeStruct((T, D), F32),
        grid=(n,),
        in_specs=[pl.BlockSpec((1, 1, 2 * rows), lambda i: (i, 0, 0), memory_space=pltpu.SMEM),
                  pl.BlockSpec((1, 1, 2 * rows), lambda i: (jnp.minimum(i + 1, n - 1), 0, 0),
                               memory_space=pltpu.SMEM),
                  pl.BlockSpec((rows, D), lambda i: (i, 0)),
                  pl.BlockSpec((rows, LANES), lambda i: (i, 0)),
                  pl.BlockSpec(memory_space=pl.ANY),
                  pl.BlockSpec((1, D), lambda i: (0, 0))],
        out_specs=pl.BlockSpec((rows, D), lambda i: (i, 0)),
        scratch_shapes=[pltpu.VMEM((2, 2, rows) + y3.shape[1:], y3.dtype), pltpu.SemaphoreType.DMA((2,))],
        compiler_params=_cparams("arbitrary"),
        name="moe_combine_norm",
    )(table, table, h, wts, y3, g.reshape(1, D))


def _dispatch_tables(ids, tm):
    T = ids.shape[0]
    E = N_EXPERTS
    eid = ids.reshape(-1)
    onehot = (eid[:, None] == jnp.arange(E, dtype=I32)[None, :]).astype(I32)
    csum = jnp.cumsum(onehot, axis=0)
    rank = jnp.sum(csum * onehot, axis=1) - 1
    counts = csum[-1]
    padded = ((counts + tm - 1) // tm) * tm
    ends = jnp.cumsum(padded)
    dest = jnp.sum(onehot * (ends - padded)[None, :], axis=1) + rank
    n_rows = 2 * T + E * tm
    tile_start = jnp.arange(n_rows // tm, dtype=I32) * tm
    tile_expert = jnp.sum((ends[None, :] <= tile_start[:, None]).astype(I32), axis=1)
    last_used = jnp.max(jnp.where(counts > 0, jnp.arange(E, dtype=I32), 0))
    tile_expert = jnp.minimum(tile_expert, last_used)
    tile_new = jnp.concatenate([jnp.ones((1,), I32), (tile_expert[1:] != tile_expert[:-1]).astype(I32)])
    n_tiles_used = (ends[-1] // tm).reshape(1)
    return dest.astype(I32), n_rows, (tile_expert, tile_new, n_tiles_used.astype(I32))


def _forward(x, norm_mix_g, w_in, cmp_k_pos, cmp_k_w1, cmp_k_b1, cmp_k_w2, cmp_v_pos, cmp_v_w1, cmp_v_b1, cmp_v_w2,
             gla_w_alpha, gla_b_alpha, gla_norm_g, w_branch_nsa, w_branch_gla, w_out, norm_ffn_g,
             w_router_group, b_router_group, w_router_expert, b_router_expert, w_exp_gate, w_exp_up, w_exp_down,
             norm_final_g):
    B, S, D = x.shape
    T = B * S
    G, R, Dh = NSA_KV_HEADS, NSA_GROUP, NSA_HEAD_DIM
    NSA_Q, NSA_KV = NSA_HEADS * Dh, G * Dh
    GQK, GV = GLA_HEADS * GLA_DK, GLA_HEADS * GLA_DV
    h = x.reshape(T, D)
    assert w_in.shape[0] == 1, "the final norm is fused into the (single) layer's combine step"
    for l in range(1):
        w = w_in[l]
        o_ng = NSA_Q + 6 * NSA_KV
        o_gq = o_ng + 3 * NSA_HEADS
        o_ga = o_gq + 2 * GQK + 2 * GV
        o_ma = o_ga + GLA_GATE_RANK
        n_small = 3 * NSA_HEADS + GLA_GATE_RANK
        w_small = jnp.concatenate([w[:, o_ng:o_gq], w[:, o_ga:o_ma], jnp.zeros((D, LANES - n_small), F32)],
                                  axis=1).astype(BF16)
        c_kc = NSA_Q // LANES
        c_ks, c_vs, c_kw, c_vw = c_kc + 2 * G, c_kc + 3 * G, c_kc + 4 * G, c_kc + 5 * G
        c_gk = GQK // LANES
        c_gv = c_gk + GQK // LANES
        c_gr = c_gv + GV // LANES

        xn = rmsnorm(h, norm_mix_g[l])
        proj_nsa = matmul(xn, w[:, :o_ng].astype(BF16), BF16, PROJ_TM, PROJ_TN, "proj_nsa").reshape(B, S, -1)
        proj_gla = matmul(xn, w[:, o_gq:o_ga].astype(BF16), BF16, PROJ_TM, PROJ_TN, "proj_gla").reshape(B, S, -1)
        proj_mix = matmul(xn, w[:, o_ma:].astype(BF16), BF16, PROJ_TM, PROJ_TN, "proj_mix")
        small = matmul(xn, w_small, F32, PROJ_TM, LANES, "proj_small")

        pos2 = jnp.stack([cmp_k_pos[l], cmp_v_pos[l]]).reshape(2, 2, CMP_STRIDE * Dh)
        w1 = jnp.stack([cmp_k_w1[l], cmp_v_w1[l]]).astype(BF16)
        b1 = jnp.stack([cmp_k_b1[l], cmp_v_b1[l]]).reshape(2, 1, Dh)
        w2 = jnp.stack([cmp_k_w2[l], cmp_v_w2[l]]).astype(BF16)
        kvc = compress(proj_nsa, c_kc, pos2, w1, b1, w2)
        gates = small[:, :3 * NSA_HEADS].reshape(B, S, G, 3 * R).transpose(0, 2, 1, 3)
        slopes = jnp.exp2(-8.0 * jnp.arange(1, NSA_HEADS + 1, dtype=F32) / NSA_HEADS)
        o_nsa = nsa_attention(proj_nsa, kvc, gates, slopes, col_q=0, col_ks=c_ks, col_vs=c_vs,
                              col_kw=c_kw, col_vw=c_vw)

        xa = small[:, 3 * NSA_HEADS:n_small].reshape(B, S, GLA_GATE_RANK)
        o_gla = gla_attention(proj_gla, xa, gla_w_alpha[l], gla_b_alpha[l], gla_norm_g[l],
                              col_q=0, col_k=c_gk, col_v=c_gv, col_r=c_gr)

        mixed = gated_mix(o_nsa.reshape(T, -1), o_gla.reshape(T, -1), w_branch_nsa[l].astype(BF16),
                          w_branch_gla[l].astype(BF16), proj_mix, 0, D // LANES)
        h = out_proj(mixed, w_out[l].astype(BF16), h)

        n_r = N_GROUPS + N_EXPERTS
        w_r = jnp.concatenate([w_router_group[l], w_router_expert[l], jnp.zeros((D, LANES - n_r), F32)], axis=1)
        b_r = jnp.concatenate([b_router_group[l], b_router_expert[l], jnp.zeros((LANES - n_r,), F32)]).reshape(1, LANES)
        ids, wts, u = router(h, norm_ffn_g[l], w_r, b_r)
        dest, n_rows, tables = _dispatch_tables(ids[:, :2], MOE_TM)
        x_sorted = dispatch_rows(u, dest, n_rows)
        h_sorted = ffn_up(x_sorted, tables, w_exp_gate[l], w_exp_up[l])
        y_sorted = ffn_down(h_sorted, tables, w_exp_down[l])
        h = combine_norm(h, wts, dest, y_sorted, norm_final_g)
    return h.reshape(B, S, D)


def kernel(x, norm_mix_g, w_in, cmp_k_pos, cmp_k_w1, cmp_k_b1, cmp_k_w2, cmp_v_pos, cmp_v_w1, cmp_v_b1, cmp_v_w2, gla_w_alpha, gla_b_alpha, gla_norm_g, w_branch_nsa, w_branch_gla, w_out, norm_ffn_g, w_router_group, b_router_group, w_router_expert, b_router_expert, w_exp_gate, w_exp_up, w_exp_down, norm_final_g):
    return _forward(x, norm_mix_g, w_in, cmp_k_pos, cmp_k_w1, cmp_k_b1, cmp_k_w2, cmp_v_pos, cmp_v_w1, cmp_v_b1,
                    cmp_v_w2, gla_w_alpha, gla_b_alpha, gla_norm_g, w_branch_nsa, w_branch_gla, w_out, norm_ffn_g,
                    w_router_group, b_router_group, w_router_expert, b_router_expert, w_exp_gate, w_exp_up,
                    w_exp_down, norm_final_g)
```

```python
import functools

import jax
import jax.numpy as jnp
from jax import lax
from jax.experimental import pallas as pl
from jax.experimental.pallas import tpu as pltpu

F32 = jnp.float32
BF16 = jnp.bfloat16
I32 = jnp.int32

NSA_HEAD_DIM = 128
NSA_KV_HEADS = 4
NSA_GROUP = 4
NSA_HEADS = NSA_KV_HEADS * NSA_GROUP
CMP_BLOCK = 32
CMP_STRIDE = 16
SEL_BLOCK = 64
SEL_TOPN = 16
WINDOW = 512
BIG = 1e9
GLA_HEADS = 16
GLA_DK = 64
GLA_DV = 128
GLA_GATE_RANK = 16
GLA_TAU = 16.0
N_GROUPS = 4
EXPERTS_PER_GROUP = 8
N_EXPERTS = N_GROUPS * EXPERTS_PER_GROUP
NORM_EPS = 1e-6

LANES = 128
VMEM_LIMIT = 56 * 1024 * 1024
NEG = -1e30

NORM_ROWS = 512
PROJ_TM, PROJ_TN = 2048, 512
NSA_TQ, NSA_TK = 256, 256
GLA_TS, GLA_C, GLA_SUB = 512, 64, 16
MIX_TM, MIX_TN = 1024, 1024
OUT_TM, OUT_TN = 1024, 1024
ROUTE_TM = 256
MOE_TM = 512
MOE_CJ = 512
MOE_CN = 2048
DISPATCH_TOKENS = 512
COMBINE_ROWS = 256


def _cparams(*sem):
    return pltpu.CompilerParams(dimension_semantics=sem, vmem_limit_bytes=VMEM_LIMIT)


def _dot(a, b, **kw):
    return jnp.dot(a, b, preferred_element_type=F32, **kw)


def _dot_nt(a, b, **kw):
    return lax.dot_general(a, b, (((1,), (1,)), ((), ())), preferred_element_type=F32, **kw)


def _dot_tn(a, b, **kw):
    return lax.dot_general(a, b, (((0,), (0,)), ((), ())), preferred_element_type=F32, **kw)


def _sigmoid(x):
    return 1.0 / (1.0 + jnp.exp(-x))


def _masked_softmax(s, mask):
    sm = jnp.where(mask, s, NEG)
    m = jnp.max(sm, axis=-1, keepdims=True)
    e = jnp.where(mask, jnp.exp(sm - m), 0.0)
    return e * (1.0 / jnp.maximum(jnp.sum(e, axis=-1, keepdims=True), 1e-30))


def _rmsnorm_kernel(x_ref, g_ref, o_ref):
    x = x_ref[...]
    ms = jnp.mean(x * x, axis=-1, keepdims=True)
    o_ref[...] = (x * lax.rsqrt(ms + NORM_EPS) * g_ref[...]).astype(o_ref.dtype)


def rmsnorm(x, g, out_dtype=BF16, rows=NORM_ROWS):
    T, D = x.shape
    rows = min(rows, T)
    return pl.pallas_call(
        _rmsnorm_kernel,
        out_shape=jax.ShapeDtypeStruct((T, D), out_dtype),
        grid=(T // rows,),
        in_specs=[pl.BlockSpec((rows, D), lambda i: (i, 0)),
                  pl.BlockSpec((1, D), lambda i: (0, 0))],
        out_specs=pl.BlockSpec((rows, D), lambda i: (i, 0)),
        compiler_params=_cparams("parallel"),
        name="rmsnorm",
    )(x, g.reshape(1, D))


def _mm_kernel(a_ref, b_ref, o_ref):
    o_ref[...] = _dot(a_ref[...], b_ref[...]).astype(o_ref.dtype)


def matmul(a, b, out_dtype, tm, tn, name):
    M, K = a.shape
    N = b.shape[1]
    tm, tn = min(tm, M), min(tn, N)
    return pl.pallas_call(
        _mm_kernel,
        out_shape=jax.ShapeDtypeStruct((M, N), out_dtype),
        grid=(M // tm, N // tn),
        in_specs=[pl.BlockSpec((tm, K), lambda i, j: (i, 0)),
                  pl.BlockSpec((K, tn), lambda i, j: (0, j))],
        out_specs=pl.BlockSpec((tm, tn), lambda i, j: (i, j)),
        compiler_params=_cparams("parallel", "arbitrary"),
        name=name,
    )(a, b)


def _compress_kernel(seq_ref, pos_ref, w1_ref, b1_ref, w2_ref, o_ref):
    seq = seq_ref[0]
    nc = seq.shape[0] // CMP_STRIDE
    x = seq.reshape(nc, CMP_STRIDE * seq.shape[1]).astype(F32)
    half = x.shape[1]
    pos = pos_ref[0]
    w1 = w1_ref[0]
    u0 = _dot((x + pos[0:1, :]).astype(BF16), w1[:half, :])
    u1 = _dot((x + pos[1:2, :]).astype(BF16), w1[half:, :])
    pre = u0 + pltpu.roll(u1, nc - 1, 0) + b1_ref[0]
    h = 0.5 * pre * (1.0 + jnp.tanh(0.7978845608028654 * (pre + 0.044715 * pre * pre * pre)))
    o_ref[0, 0, 0] = _dot(h.astype(BF16), w2_ref[0]).astype(o_ref.dtype)


def compress(proj, col_kc, pos2, w1, b1, w2):
    B, S, _ = proj.shape
    G = NSA_KV_HEADS
    Dh = w2.shape[-1]
    NC, HW = S // CMP_STRIDE, CMP_STRIDE * Dh
    return pl.pallas_call(
        _compress_kernel,
        out_shape=jax.ShapeDtypeStruct((2, B, G, NC, Dh), BF16),
        grid=(2, B, G),
        in_specs=[pl.BlockSpec((1, S, Dh), lambda a, b, g: (b, 0, col_kc + a * G + g)),
                  pl.BlockSpec((1, 2, HW), lambda a, b, g: (a, 0, 0)),
                  pl.BlockSpec((1, 2 * HW, Dh), lambda a, b, g: (a, 0, 0)),
                  pl.BlockSpec((1, 1, Dh), lambda a, b, g: (a, 0, 0)),
                  pl.BlockSpec((1, Dh, Dh), lambda a, b, g: (a, 0, 0))],
        out_specs=pl.BlockSpec((1, 1, 1, NC, Dh), lambda a, b, g: (a, b, g, 0, 0)),
        compiler_params=_cparams("parallel", "parallel", "parallel"),
        name="nsa_compress",
    )(proj, pos2, w1, b1, w2)


MASK_BIG = 2.0 ** 100


def _nsa_kernel(slopes_ref, q_ref, kc_ref, vc_ref, ks_ref, vs_ref, kw_ref, vw_ref, g_ref,
                paux_ref, saux_ref, caux_ref, mmat_ref, o_ref,
                qx_ref, sc_ref, sw_ref, sa_ref, sb_ref, score_ref, m_ref, acc_ref,
                *, TQ, TK, NC, NSEL, NTOP):
    R, Dh = NSA_GROUP, NSA_HEAD_DIM
    g = pl.program_id(1)
    q0 = pl.program_id(2) * TQ
    scale = Dh ** -0.5
    lane = lax.broadcasted_iota(I32, (1, LANES), 1)
    t1 = q0 + lax.broadcasted_iota(I32, (TQ, 1), 0)
    head = lambda r: slice(r * TQ, (r + 1) * TQ)

    def alibi_cols(r):
        hi, lo = slopes_ref[0, g * R + r], slopes_ref[1, g * R + r]
        c = jnp.where(lane == 0, 64.0 * hi, jnp.where(lane == 1, 64.0 * lo,
                                                      jnp.where(lane == 2, hi, jnp.where(lane == 3, lo, 0.0))))
        return jnp.broadcast_to(c, (TQ, LANES))

    for r in range(R):
        qx_ref[head(r), 0:Dh] = (q_ref[0, :, r * Dh:(r + 1) * Dh].astype(F32) * scale).astype(BF16)
        qx_ref[head(r), Dh:Dh + LANES] = alibi_cols(r).astype(BF16)

    def with_ones(v):
        return jnp.concatenate([v, jnp.ones(v.shape, v.dtype)], axis=1)

    def exp_pv(s, v1):
        m = jnp.broadcast_to(jnp.max(s, axis=-1, keepdims=True), (TQ, LANES))
        e = [jnp.exp(s[:, c * LANES:(c + 1) * LANES] - m) for c in range(s.shape[1] // LANES)]
        return e, _dot(jnp.concatenate(e, axis=1).astype(BF16), v1)

    WK = WINDOW + TQ
    ws = pl.multiple_of(jnp.maximum(q0 - WINDOW, 0), TQ)
    sc_ref[...] = _dot_nt(qx_ref[...], jnp.concatenate([kc_ref[0, 0, 0], caux_ref[...]], axis=1))
    sw_ref[...] = _dot_nt(qx_ref[...], jnp.concatenate([kw_ref[0, pl.ds(ws, WK), :], paux_ref[pl.ds(ws, WK), :]],
                                                       axis=1))

    cmp_end = lax.broadcasted_iota(I32, (1, NC), 1) * CMP_STRIDE + (CMP_BLOCK - 1)
    cmp_bias = jnp.where(cmp_end <= t1, 0.0, NEG)
    row_ok = q0 + lax.broadcasted_iota(I32, (TQ, LANES), 0) >= CMP_BLOCK - 1
    vc1 = with_ones(vc_ref[0, 0, 0])
    o_cmp = []
    imp = None
    for r in range(R):
        e, o2 = exp_pv(sc_ref[head(r), :] + cmp_bias, vc1)
        inv = jnp.where(row_ok, 1.0 / jnp.maximum(o2[:, Dh:], 1e-30), 0.0)
        o_cmp.append(o2[:, :Dh] * inv)
        p = jnp.concatenate([ec * inv for ec in e], axis=1)
        imp = p if imp is None else imp + p

    vw1 = with_ones(vw_ref[0, pl.ds(ws, WK), :])
    dw = t1 - (ws + lax.broadcasted_iota(I32, (1, WK), 1))
    win_bias = jnp.where((dw >= 0) & (dw < WINDOW), 0.0, NEG)
    o_win = []
    for r in range(R):
        _, o2 = exp_pv(sw_ref[head(r), :] + win_bias, vw1)
        o_win.append(o2[:, :Dh] * (1.0 / jnp.maximum(o2[:, Dh:], 1e-30)))

    mm = mmat_ref[...]
    i1 = imp.astype(BF16)
    rem = imp - i1.astype(F32)
    i2 = rem.astype(BF16)
    i3 = (rem - i2.astype(F32)).astype(BF16)
    imp_sel = _dot_nt(mm, i1) + _dot_nt(mm, i2) + _dot_nt(mm, i3)

    cur = (q0 + lax.broadcasted_iota(I32, (1, TQ), 1)) // SEL_BLOCK
    blk = lax.broadcasted_iota(I32, (NSEL, 1), 0)
    forced = (blk == 0) | (blk == cur) | (blk == cur - 1)
    score = jnp.where(blk <= cur, jnp.where(forced, BIG, imp_sel), -BIG)
    score_ref[...] = score

    def rank_pair(i2, cnt):
        for d in range(2):
            ii = 2 * i2 + d
            other = score_ref[pl.ds(ii, 1), :]
            beats = (other > score) | ((other == score) & (blk > ii))
            cnt = cnt + jnp.where(beats, 1.0, 0.0)
        return cnt

    n_blk = (q0 + TQ) // SEL_BLOCK
    cnt = lax.fori_loop(0, n_blk // 2, rank_pair, jnp.zeros((NSEL, TQ), F32))
    unsel = jnp.where(cnt < NTOP, 0.0, -1.0).astype(BF16)
    place = jnp.where(lax.broadcasted_iota(I32, (NSEL, LANES), 1)
                      == lax.broadcasted_iota(I32, (NSEL, LANES), 0) + LANES // 2, 1.0, 0.0).astype(BF16)
    unsel_l = _dot_tn(unsel, place)
    for r in range(R):
        qx_ref[head(r), Dh:Dh + LANES] = (alibi_cols(r) + unsel_l).astype(BF16)

    m_ref[...] = jnp.full(m_ref.shape, NEG, F32)
    acc_ref[...] = jnp.zeros(acc_ref.shape, F32)
    pos_in_tile = lax.broadcasted_iota(I32, (1, TK), 1)

    def issue_scores(kt, s_ref):
        k0 = pl.multiple_of(kt * TK, TK)
        kk = jnp.concatenate([ks_ref[0, pl.ds(k0, TK), :], saux_ref[pl.ds(k0, TK), :]], axis=1)
        s_ref[...] = _dot_nt(qx_ref[...], kk)

    def consume_scores(kt, s_ref, causal):
        k0 = pl.multiple_of(kt * TK, TK)
        vv1 = with_ones(vs_ref[0, pl.ds(k0, TK), :])
        if causal:
            causal_bias = jnp.where(t1 >= k0 + pos_in_tile, 0.0, -MASK_BIG)
        for r in range(R):
            sc = s_ref[head(r), :]
            if causal:
                sc = sc + causal_bias
            m_old = m_ref[r]
            m_new = jnp.maximum(m_old, jnp.broadcast_to(jnp.max(sc, axis=-1, keepdims=True), (TQ, LANES)))
            alpha = jnp.exp(m_old - m_new)
            p = jnp.concatenate([jnp.exp(sc[:, c * LANES:(c + 1) * LANES] - m_new) for c in range(TK // LANES)],
                                axis=1).astype(BF16)
            m_ref[r] = m_new
            acc_ref[r] = jnp.concatenate([alpha, alpha], axis=1) * acc_ref[r] + _dot(p, vv1)

    def sel_pair(k, carry):
        issue_scores(2 * k + 1, sb_ref)
        consume_scores(2 * k, sa_ref, False)
        issue_scores(2 * k + 2, sa_ref)
        consume_scores(2 * k + 1, sb_ref, False)
        return carry

    n_below = q0 // TK
    issue_scores(0, sa_ref)
    lax.fori_loop(0, n_below // 2, sel_pair, 0)

    @pl.when(n_below % 2 == 1)
    def _():
        issue_scores(n_below, sb_ref)
        consume_scores(n_below - 1, sa_ref, False)
        consume_scores(n_below, sb_ref, True)

    @pl.when(n_below % 2 == 0)
    def _():
        consume_scores(n_below, sa_ref, True)

    gt = _sigmoid(g_ref[0, 0])
    for r in range(R):
        o_sel = acc_ref[r, :, 0:Dh] * (1.0 / jnp.maximum(acc_ref[r, :, Dh:2 * Dh], 1e-30))
        o = (gt[:, 3 * r:3 * r + 1] * o_cmp[r] + gt[:, 3 * r + 1:3 * r + 2] * o_sel
             + gt[:, 3 * r + 2:3 * r + 3] * o_win[r])
        o_ref[0, :, r * Dh:(r + 1) * Dh] = o.astype(o_ref.dtype)


def _nsa_constants(S, NC, NSEL):
    def pos_cols(pos):
        col = jnp.arange(LANES)[None, :]
        hi, lo = (pos // 64)[:, None], (pos % 64)[:, None]
        return jnp.where(col < 2, hi, jnp.where(col < 4, lo, 0)).astype(F32)

    pos = jnp.arange(S)
    paux = pos_cols(pos)
    onehot = (jnp.arange(LANES)[None, :] - LANES // 2 == (pos // SEL_BLOCK)[:, None]) & (jnp.arange(LANES)[None, :] >= LANES // 2)
    saux = paux + jnp.where(onehot, MASK_BIG, 0.0)
    caux = pos_cols(jnp.arange(NC) * CMP_STRIDE + (CMP_BLOCK - 1))
    d = jnp.arange(NC)[None, :] - 4 * jnp.arange(NSEL)[:, None]
    mmat = jnp.where((d == -1) | (d == 3), 1.0, jnp.where((d >= 0) & (d <= 2), 2.0, 0.0))
    return paux.astype(BF16), saux.astype(BF16), caux.astype(BF16), mmat.astype(BF16)


def nsa_attention(proj, kvc, gates, slopes, *, col_q, col_ks, col_vs, col_kw, col_vw):
    B, S, _ = proj.shape
    G, R, Dh = NSA_KV_HEADS, NSA_GROUP, NSA_HEAD_DIM
    NC = kvc.shape[3]
    NSEL = S // SEL_BLOCK
    TQ, TK = min(NSA_TQ, S), min(NSA_TK, S)
    assert S % TQ == 0 and TK % TQ == 0 and S % TK == 0 and S >= WINDOW + TQ and NC == S // CMP_STRIDE
    assert NSEL <= LANES // 2 and S // 64 <= 256 and TQ % (2 * SEL_BLOCK) == 0
    s_hi = slopes.astype(BF16).astype(F32)
    s_lo = (slopes - s_hi).astype(BF16).astype(F32)
    paux, saux, caux, mmat = _nsa_constants(S, NC, NSEL)
    kern = functools.partial(_nsa_kernel, TQ=TQ, TK=TK, NC=NC, NSEL=NSEL, NTOP=min(SEL_TOPN, NSEL))
    kv_spec = lambda col: pl.BlockSpec((1, S, Dh), lambda b, g, i: (b, 0, col + g))
    whole = lambda a: pl.BlockSpec(a.shape, lambda b, g, i: (0, 0))
    return pl.pallas_call(
        kern,
        out_shape=jax.ShapeDtypeStruct((B, S, G * R * Dh), BF16),
        grid=(B, G, S // TQ),
        in_specs=[pl.BlockSpec(memory_space=pltpu.SMEM),
                  pl.BlockSpec((1, TQ, R * Dh), lambda b, g, i: (b, i, col_q // R + g)),
                  pl.BlockSpec((1, 1, 1, NC, Dh), lambda b, g, i: (0, b, g, 0, 0)),
                  pl.BlockSpec((1, 1, 1, NC, Dh), lambda b, g, i: (1, b, g, 0, 0)),
                  kv_spec(col_ks), kv_spec(col_vs), kv_spec(col_kw), kv_spec(col_vw),
                  pl.BlockSpec((1, 1, TQ, 3 * R), lambda b, g, i: (b, g, i, 0)),
                  whole(paux), whole(saux), whole(caux), whole(mmat)],
        out_specs=pl.BlockSpec((1, TQ, R * Dh), lambda b, g, i: (b, i, g)),
        scratch_shapes=[pltpu.VMEM((R * TQ, Dh + LANES), BF16), pltpu.VMEM((R * TQ, NC), F32),
                        pltpu.VMEM((R * TQ, WINDOW + TQ), F32), pltpu.VMEM((R * TQ, TK), F32),
                        pltpu.VMEM((R * TQ, TK), F32),
                        pltpu.VMEM((NSEL, TQ), F32), pltpu.VMEM((R, TQ, LANES), F32),
                        pltpu.VMEM((R, TQ, 2 * Dh), F32)],
        compiler_params=_cparams("parallel", "parallel", "arbitrary"),
        name="nsa_attention",
    )(jnp.stack([s_hi, s_lo]), proj, kvc, kvc, proj, proj, proj, proj, gates, paux, saux, caux, mmat)


def _gla_kernel(q_ref, k_ref, v_ref, r_ref, xa_ref, wa_ref, ba_ref, ng_ref, tri_ref, o_ref,
                st_ref, b_ref, q2_ref, kh_ref, qe_ref, kd_ref, oacc_ref, *, TS, C):
    DK2, DV2 = 2 * GLA_DK, 2 * GLA_DV
    NCH = TS // C
    LEVELS = [C >> (i + 1) for i in range(C.bit_length() - 1)]

    @pl.when(pl.program_id(2) == 0)
    def _():
        st_ref[...] = jnp.zeros_like(st_ref)

    z = _dot(xa_ref[0], wa_ref[...], precision=lax.Precision.HIGHEST) + ba_ref[...]
    la = (jnp.minimum(z, 0.0) - jnp.log(1.0 + jnp.exp(-jnp.abs(z)))) * (1.0 / GLA_TAU)
    l1 = la.astype(BF16)
    lr = la - l1.astype(F32)
    l2 = lr.astype(BF16)
    l3 = (lr - l2.astype(F32)).astype(BF16)
    tri = tri_ref[...]
    b = _dot(tri, l1) + _dot(tri, l2) + _dot(tri, l3)
    b_ref[...] = b

    head_a = lax.broadcasted_iota(I32, (1, DK2), 1) < GLA_DK
    row = lax.broadcasted_iota(I32, (TS, 1), 0)
    q = q_ref[0].astype(F32) * (GLA_DK ** -0.5)
    k = k_ref[0].astype(F32)

    def ref_rows(h):
        if h >= 4:
            g = TS // (2 * h)
            return jnp.broadcast_to(b.reshape(g, 2 * h, DK2)[:, h - 1:h, :], (g, 2 * h, DK2)).reshape(TS, DK2)
        dn1 = pltpu.roll(b, 1, 0)
        if h == 1:
            return jnp.where(row % 2 == 0, b, dn1)
        m4 = row % 4
        return jnp.where(m4 == 0, pltpu.roll(b, TS - 1, 0),
                         jnp.where(m4 == 1, b, jnp.where(m4 == 2, dn1, pltpu.roll(b, 2, 0))))

    def put(lvl, qh, kh):
        q2_ref[lvl, :, 0:C, :] = jnp.where(head_a, qh, 0.0).astype(BF16).reshape(NCH, C, DK2)
        q2_ref[lvl, :, C:2 * C, :] = jnp.where(head_a, 0.0, qh).astype(BF16).reshape(NCH, C, DK2)
        kh_ref[lvl] = kh.astype(BF16)

    put(0, q, k)
    for lvl, h in enumerate(LEVELS, start=1):
        r = ref_rows(h)
        put(lvl, q * jnp.exp(jnp.minimum(b - r, 0.0)), k * jnp.exp(jnp.minimum(r - b, 0.0)))
    b_last = jnp.broadcast_to(b.reshape(NCH, C, DK2)[:, C - 1:C, :], (NCH, C, DK2)).reshape(TS, DK2)
    qe_ref[...] = (q * jnp.exp(b)).astype(BF16)
    kd_ref[...] = (k * jnp.exp(b_last - b)).astype(BF16)

    row2 = lax.broadcasted_iota(I32, (2 * C, 1), 0) % C
    col = lax.broadcasted_iota(I32, (1, C), 1)
    masks = [row2 == col] + [(row2 // (2 * h) == col // (2 * h)) & (row2 % (2 * h) >= h) & (col % (2 * h) < h)
                             for h in LEVELS]
    vk_same = (lax.broadcasted_iota(I32, (DV2, DK2), 0) < GLA_DV) == (lax.broadcasted_iota(I32, (DV2, DK2), 1) < GLA_DK)
    for c in range(NCH):
        rows = slice(c * C, (c + 1) * C)
        vb = v_ref[0, rows, :]
        a2 = jnp.zeros((2 * C, C), F32)
        for lvl, mask in enumerate(masks):
            a2 = a2 + jnp.where(mask, _dot_nt(q2_ref[lvl, c], kh_ref[lvl, rows, :]), 0.0)
        a2 = a2.astype(BF16)
        o = jnp.concatenate([_dot(a2[0:C], vb[:, :GLA_DV]), _dot(a2[C:2 * C], vb[:, GLA_DV:])], axis=1)
        st = st_ref[...]
        oacc_ref[rows, :] = o + _dot_nt(qe_ref[rows, :], st.astype(BF16))
        ds = jnp.where(vk_same, _dot_tn(vb, kd_ref[rows, :]), 0.0)
        st_ref[...] = st * jnp.exp(b_ref[(c + 1) * C - 1:(c + 1) * C, :]) + ds

    def ln(x):
        mu = jnp.mean(x, axis=-1, keepdims=True)
        xc = x - mu
        return xc * lax.rsqrt(jnp.mean(xc * xc, axis=-1, keepdims=True) + NORM_EPS)

    o = oacc_ref[...]
    on = jnp.concatenate([ln(o[:, :GLA_DV]), ln(o[:, GLA_DV:])], axis=1) * ng_ref[...]
    rr = r_ref[0].astype(F32)
    o_ref[0] = (on * (rr * _sigmoid(rr))).astype(o_ref.dtype)


def gla_attention(proj, xa, w_alpha, b_alpha, norm_g, *, col_q, col_k, col_v, col_r):
    B, S, _ = proj.shape
    HP = GLA_HEADS // 2
    DK2, DV2 = 2 * GLA_DK, 2 * GLA_DV
    TS, C = min(GLA_TS, S), GLA_C
    assert S % TS == 0 and TS % C == 0 and C & (C - 1) == 0 and C >= 16
    n_lvl = C.bit_length()
    idx = jnp.arange(TS)
    tri = ((idx[None, :] <= idx[:, None]) & (idx[None, :] // C == idx[:, None] // C)).astype(BF16)
    kern = functools.partial(_gla_kernel, TS=TS, C=C)
    return pl.pallas_call(
        kern,
        out_shape=jax.ShapeDtypeStruct((B, S, GLA_HEADS * GLA_DV), BF16),
        grid=(B, HP, S // TS),
        in_specs=[pl.BlockSpec((1, TS, DK2), lambda b, p, s: (b, s, col_q + p)),
                  pl.BlockSpec((1, TS, DK2), lambda b, p, s: (b, s, col_k + p)),
                  pl.BlockSpec((1, TS, DV2), lambda b, p, s: (b, s, col_v // 2 + p)),
                  pl.BlockSpec((1, TS, DV2), lambda b, p, s: (b, s, col_r // 2 + p)),
                  pl.BlockSpec((1, TS, GLA_GATE_RANK), lambda b, p, s: (b, s, 0)),
                  pl.BlockSpec((GLA_GATE_RANK, DK2), lambda b, p, s: (0, p)),
                  pl.BlockSpec((1, DK2), lambda b, p, s: (0, p)),
                  pl.BlockSpec((1, DV2), lambda b, p, s: (0, p)),
                  pl.BlockSpec((TS, TS), lambda b, p, s: (0, 0))],
        out_specs=pl.BlockSpec((1, TS, DV2), lambda b, p, s: (b, s, p)),
        scratch_shapes=[pltpu.VMEM((DV2, DK2), F32), pltpu.VMEM((TS, DK2), F32),
                        pltpu.VMEM((n_lvl, TS // C, 2 * C, DK2), BF16), pltpu.VMEM((n_lvl, TS, DK2), BF16),
                        pltpu.VMEM((TS, DK2), BF16), pltpu.VMEM((TS, DK2), BF16), pltpu.VMEM((TS, DV2), F32)],
        compiler_params=_cparams("parallel", "parallel", "arbitrary"),
        name="gla_attention",
    )(proj, proj, proj, proj, xa, w_alpha, b_alpha.reshape(1, -1), norm_g.reshape(1, -1), tri)


def _mix_kernel(oa_ref, ob_ref, wa_ref, wb_ref, ma_ref, mb_ref, o_ref):
    ya = _dot(oa_ref[...], wa_ref[...])
    yb = _dot(ob_ref[...], wb_ref[...])
    o_ref[...] = (_sigmoid(ma_ref[...].astype(F32)) * ya + _sigmoid(mb_ref[...].astype(F32)) * yb).astype(o_ref.dtype)


def gated_mix(o_nsa, o_gla, wa, wb, proj2d, col_ma, col_mb):
    T, KA = o_nsa.shape
    KB = o_gla.shape[1]
    N = wa.shape[1]
    tm, tn = min(MIX_TM, T), min(MIX_TN, N)
    ca, cb = col_ma * LANES // tn, col_mb * LANES // tn
    assert (col_ma * LANES) % tn == 0 and (col_mb * LANES) % tn == 0
    return pl.pallas_call(
        _mix_kernel,
        out_shape=jax.ShapeDtypeStruct((T, N), BF16),
        grid=(T // tm, N // tn),
        in_specs=[pl.BlockSpec((tm, KA), lambda i, j: (i, 0)),
                  pl.BlockSpec((tm, KB), lambda i, j: (i, 0)),
                  pl.BlockSpec((KA, tn), lambda i, j: (0, j)),
                  pl.BlockSpec((KB, tn), lambda i, j: (0, j)),
                  pl.BlockSpec((tm, tn), lambda i, j: (i, ca + j)),
                  pl.BlockSpec((tm, tn), lambda i, j: (i, cb + j))],
        out_specs=pl.BlockSpec((tm, tn), lambda i, j: (i, j)),
        compiler_params=_cparams("parallel", "arbitrary"),
        name="gated_mix",
    )(o_nsa, o_gla, wa, wb, proj2d, proj2d)


def _out_kernel(a_ref, w_ref, x_ref, o_ref):
    o_ref[...] = x_ref[...] + _dot(a_ref[...], w_ref[...])


def out_proj(mixed, w, x):
    T, K = mixed.shape
    N = w.shape[1]
    tm, tn = min(OUT_TM, T), min(OUT_TN, N)
    return pl.pallas_call(
        _out_kernel,
        out_shape=jax.ShapeDtypeStruct((T, N), F32),
        grid=(T // tm, N // tn),
        in_specs=[pl.BlockSpec((tm, K), lambda i, j: (i, 0)),
                  pl.BlockSpec((K, tn), lambda i, j: (0, j)),
                  pl.BlockSpec((tm, tn), lambda i, j: (i, j))],
        out_specs=pl.BlockSpec((tm, tn), lambda i, j: (i, j)),
        compiler_params=_cparams("parallel", "arbitrary"),
        name="out_proj",
    )(mixed, w, x)


def _router_kernel(h_ref, g_ref, w_ref, b_ref, id_ref, wt_ref, u_ref):
    x = h_ref[...]
    u = x * lax.rsqrt(jnp.mean(x * x, axis=-1, keepdims=True) + NORM_EPS) * g_ref[...]
    u_hi = u.astype(BF16)
    u_ref[...] = u_hi.reshape(u_ref.shape)
    u_lo = (u - u_hi.astype(F32)).astype(BF16)
    logit = _dot(u_hi, w_ref[0]) + _dot(u_lo, w_ref[0]) + _dot(u_hi, w_ref[1]) + b_ref[...]
    lane = lax.broadcasted_iota(I32, logit.shape, 1)
    big = jnp.int32(1 << 20)
    gmask = lane < N_GROUPS
    gl = jnp.where(gmask, logit, NEG)
    gmax = jnp.max(gl, axis=-1, keepdims=True)
    g_star = jnp.min(jnp.where(gmask & (gl == gmax), lane, big), axis=-1, keepdims=True)
    p_group = 1.0 / jnp.sum(jnp.where(gmask, jnp.exp(gl - gmax), 0.0), axis=-1, keepdims=True)
    e_lo = N_GROUPS + EXPERTS_PER_GROUP * g_star
    emask = (lane >= e_lo) & (lane < e_lo + EXPERTS_PER_GROUP)
    el = jnp.where(emask, logit, NEG)
    m1 = jnp.max(el, axis=-1, keepdims=True)
    i1 = jnp.min(jnp.where(emask & (el == m1), lane, big), axis=-1, keepdims=True)
    emask2 = emask & (lane != i1)
    el2 = jnp.where(emask2, logit, NEG)
    m2 = jnp.max(el2, axis=-1, keepdims=True)
    i2 = jnp.min(jnp.where(emask2 & (el2 == m2), lane, big), axis=-1, keepdims=True)
    e2 = jnp.exp(m2 - m1)
    w1 = p_group / (1.0 + e2)
    w2 = p_group * e2 / (1.0 + e2)
    id_ref[...] = jnp.where(lane == 0, i1 - N_GROUPS, jnp.where(lane == 1, i2 - N_GROUPS, 0))
    wt_ref[...] = jnp.where(lane == 0, w1, jnp.where(lane == 1, w2, 0.0))


def router(h, g, w_r, b_r):
    T, D = h.shape
    tm = min(ROUTE_TM, T)
    w_hi = w_r.astype(BF16)
    return pl.pallas_call(
        _router_kernel,
        out_shape=(jax.ShapeDtypeStruct((T, LANES), I32), jax.ShapeDtypeStruct((T, LANES), F32),
                   jax.ShapeDtypeStruct((T, D // LANES, LANES), BF16)),
        grid=(T // tm,),
        in_specs=[pl.BlockSpec((tm, D), lambda i: (i, 0)),
                  pl.BlockSpec((1, D), lambda i: (0, 0)),
                  pl.BlockSpec((2, D, LANES), lambda i: (0, 0, 0)),
                  pl.BlockSpec((1, LANES), lambda i: (0, 0))],
        out_specs=(pl.BlockSpec((tm, LANES), lambda i: (i, 0)), pl.BlockSpec((tm, LANES), lambda i: (i, 0)),
                   pl.BlockSpec((tm, D // LANES, LANES), lambda i: (i, 0, 0))),
        compiler_params=_cparams("parallel"),
        name="moe_router",
    )(h, g.reshape(1, D), jnp.stack([w_hi, (w_r - w_hi.astype(F32)).astype(BF16)]), b_r)


def _row_copy(src_ref, dst_ref, sem, src_row, dst_row):
    return pltpu.make_async_copy(src_ref.at[pl.ds(src_row, 1)], dst_ref.at[pl.ds(dst_row, 1)], sem)


def _dispatch_rows_kernel(dest_ref, u_ref, init_hbm, out_hbm, sem, *, toks):
    def start(t, c):
        for k in range(2):
            _row_copy(u_ref, out_hbm, sem, t, dest_ref[0, 0, 2 * t + k]).start(priority=k)
        return c

    def wait(t, c):
        for k in range(2):
            _row_copy(u_ref, out_hbm, sem, t, 0).wait()
        return c

    lax.fori_loop(0, toks, start, 0)
    lax.fori_loop(0, toks, wait, 0)


def dispatch_rows(u3, dest, n_rows):
    T = u3.shape[0]
    toks = min(DISPATCH_TOKENS, T)
    assert T % toks == 0
    out_shape = jax.ShapeDtypeStruct((n_rows,) + u3.shape[1:], u3.dtype)
    return pl.pallas_call(
        functools.partial(_dispatch_rows_kernel, toks=toks),
        out_shape=out_shape,
        grid=(T // toks,),
        in_specs=[pl.BlockSpec((1, 1, 2 * toks), lambda i: (i, 0, 0), memory_space=pltpu.SMEM),
                  pl.BlockSpec((toks,) + u3.shape[1:], lambda i: (i, 0, 0)),
                  pl.BlockSpec(memory_space=pl.ANY)],
        out_specs=pl.BlockSpec(memory_space=pl.ANY),
        scratch_shapes=[pltpu.SemaphoreType.DMA],
        input_output_aliases={2: 0},
        compiler_params=_cparams("arbitrary"),
        name="moe_dispatch_rows",
    )(dest.reshape(T // toks, 1, 2 * toks), u3, jnp.zeros(out_shape.shape, out_shape.dtype))


def _ffn_up_kernel(te_ref, new_ref, nv_ref, x_ref, wg_ref, wu_ref, o_ref, wg_bf, wu_bf):
    i = pl.program_id(1)

    @pl.when(new_ref[i] == 1)
    def _():
        wg_bf[...] = wg_ref[0].astype(BF16)
        wu_bf[...] = wu_ref[0].astype(BF16)

    @pl.when(i < nv_ref[0])
    def _():
        x = x_ref[...].reshape(x_ref.shape[0], wg_bf.shape[0])
        a = _dot(x, wg_bf[...])
        u = _dot(x, wu_bf[...])
        o_ref[...] = (a * _sigmoid(a) * u).astype(o_ref.dtype)

    @pl.when(i >= nv_ref[0])
    def _():
        o_ref[...] = jnp.zeros_like(o_ref)


def ffn_up(x_sorted, tables, w_gate, w_up):
    NP = x_sorted.shape[0]
    E, D, DE = w_gate.shape
    tm, cj = MOE_TM, min(MOE_CJ, DE)
    return pl.pallas_call(
        _ffn_up_kernel,
        out_shape=jax.ShapeDtypeStruct((NP, DE), BF16),
        grid_spec=pltpu.PrefetchScalarGridSpec(
            num_scalar_prefetch=3,
            grid=(DE // cj, NP // tm),
            in_specs=[pl.BlockSpec((tm,) + x_sorted.shape[1:], lambda j, i, te, nw, nv: (i, 0, 0)),
                      pl.BlockSpec((1, D, cj), lambda j, i, te, nw, nv: (te[i], 0, j)),
                      pl.BlockSpec((1, D, cj), lambda j, i, te, nw, nv: (te[i], 0, j))],
            out_specs=pl.BlockSpec((tm, cj), lambda j, i, te, nw, nv: (i, j)),
            scratch_shapes=[pltpu.VMEM((D, cj), BF16), pltpu.VMEM((D, cj), BF16)]),
        compiler_params=_cparams("arbitrary", "arbitrary"),
        name="moe_ffn_up",
    )(*tables, x_sorted, w_gate, w_up)


def _ffn_down_kernel(te_ref, new_ref, nv_ref, h_ref, wd_ref, o_ref, wd_bf):
    i = pl.program_id(1)

    @pl.when(new_ref[i] == 1)
    def _():
        wd_bf[...] = wd_ref[0].astype(BF16)

    @pl.when(i < nv_ref[0])
    def _():
        o_ref[...] = _dot(h_ref[...], wd_bf[...]).astype(o_ref.dtype).reshape(o_ref.shape)

    @pl.when(i >= nv_ref[0])
    def _():
        o_ref[...] = jnp.zeros_like(o_ref)


def ffn_down(h_sorted, tables, w_down):
    NP, DE = h_sorted.shape
    E, _, D = w_down.shape
    tm, cn = MOE_TM, min(MOE_CN, D)
    return pl.pallas_call(
        _ffn_down_kernel,
        out_shape=jax.ShapeDtypeStruct((NP, D // LANES, LANES), BF16),
        grid_spec=pltpu.PrefetchScalarGridSpec(
            num_scalar_prefetch=3,
            grid=(D // cn, NP // tm),
            in_specs=[pl.BlockSpec((tm, DE), lambda j, i, te, nw, nv: (i, 0)),
                      pl.BlockSpec((1, DE, cn), lambda j, i, te, nw, nv: (te[i], 0, j))],
            out_specs=pl.BlockSpec((tm, cn // LANES, LANES), lambda j, i, te, nw, nv: (i, j, 0)),
            scratch_shapes=[pltpu.VMEM((DE, cn), BF16)]),
        compiler_params=_cparams("arbitrary", "arbitrary"),
        name="moe_ffn_down",
    )(*tables, h_sorted, w_down)


def _combine_kernel(idx_ref, nxt_ref, h_ref, wt_ref, y_hbm, g_ref, o_ref, buf, sem, *, rows):
    i = pl.program_id(0)
    slot = i % 2

    def issue(table, s):
        def body(r, c):
            _row_copy(y_hbm, buf.at[s, 0], sem.at[s], table[0, 0, 2 * r], r).start(priority=0)
            _row_copy(y_hbm, buf.at[s, 1], sem.at[s], table[0, 0, 2 * r + 1], r).start(priority=1)
            return c
        lax.fori_loop(0, rows, body, 0)

    @pl.when(i == 0)
    def _():
        issue(idx_ref, 0)

    @pl.when(i + 1 < pl.num_programs(0))
    def _():
        issue(nxt_ref, 1 - slot)

    def wait(r, c):
        _row_copy(y_hbm, buf.at[slot, 0], sem.at[slot], 0, r).wait()
        _row_copy(y_hbm, buf.at[slot, 1], sem.at[slot], 0, r).wait()
        return c

    lax.fori_loop(0, rows, wait, 0)
    wt = wt_ref[...]
    y0 = buf[slot, 0].reshape(h_ref.shape).astype(F32)
    y1 = buf[slot, 1].reshape(h_ref.shape).astype(F32)
    x = h_ref[...] + wt[:, 0:1] * y0 + wt[:, 1:2] * y1
    o_ref[...] = x * lax.rsqrt(jnp.mean(x * x, axis=-1, keepdims=True) + NORM_EPS) * g_ref[...]


def combine_norm(h, wts, dest, y3, g):
    T, D = h.shape
    rows = min(COMBINE_ROWS, T)
    n = T // rows
    table = dest.reshape(n, 1, 2 * rows)
    return pl.pallas_call(
        functools.partial(_combine_kernel, rows=rows),
        out_shape=jax.ShapeDtypeStruct((T, D), F32),
        grid=(n,),
        in_specs=[pl.BlockSpec((1, 1, 2 * rows), lambda i: (i, 0, 0), memory_space=pltpu.SMEM),
                  pl.BlockSpec((1, 1, 2 * rows), lambda i: (jnp.minimum(i + 1, n - 1), 0, 0),
                               memory_space=pltpu.SMEM),
                  pl.BlockSpec((rows, D), lambda i: (i, 0)),
                  pl.BlockSpec((rows, LANES), lambda i: (i, 0)),
                  pl.BlockSpec(memory_space=pl.ANY),
                  pl.BlockSpec((1, D), lambda i: (0, 0))],
        out_specs=pl.BlockSpec((rows, D), lambda i: (i, 0)),
        scratch_shapes=[pltpu.VMEM((2, 2, rows) + y3.shape[1:], y3.dtype), pltpu.SemaphoreType.DMA((2,))],
        compiler_params=_cparams("arbitrary"),
        name="moe_combine_norm",
    )(table, table, h, wts, y3, g.reshape(1, D))


def _dispatch_tables(ids, tm):
    T = ids.shape[0]
    E = N_EXPERTS
    eid = ids.reshape(-1)
    onehot = (eid[:, None] == jnp.arange(E, dtype=I32)[None, :]).astype(I32)
    csum = jnp.cumsum(onehot, axis=0)
    rank = jnp.sum(csum * onehot, axis=1) - 1
    counts = csum[-1]
    padded = ((counts + tm - 1) // tm) * tm
    ends = jnp.cumsum(padded)
    dest = jnp.sum(onehot * (ends - padded)[None, :], axis=1) + rank
    n_rows = 2 * T + E * tm
    tile_start = jnp.arange(n_rows // tm, dtype=I32) * tm
    tile_expert = jnp.sum((ends[None, :] <= tile_start[:, None]).astype(I32), axis=1)
    last_used = jnp.max(jnp.where(counts > 0, jnp.arange(E, dtype=I32), 0))
    tile_expert = jnp.minimum(tile_expert, last_used)
    tile_new = jnp.concatenate([jnp.ones((1,), I32), (tile_expert[1:] != tile_expert[:-1]).astype(I32)])
    n_tiles_used = (ends[-1] // tm).reshape(1)
    return dest.astype(I32), n_rows, (tile_expert, tile_new, n_tiles_used.astype(I32))


def _forward(x, norm_mix_g, w_in, cmp_k_pos, cmp_k_w1, cmp_k_b1, cmp_k_w2, cmp_v_pos, cmp_v_w1, cmp_v_b1, cmp_v_w2,
             gla_w_alpha, gla_b_alpha, gla_norm_g, w_branch_nsa, w_branch_gla, w_out, norm_ffn_g,
             w_router_group, b_router_group, w_router_expert, b_router_expert, w_exp_gate, w_exp_up, w_exp_down,
             norm_final_g):
    B, S, D = x.shape
    T = B * S
    G, R, Dh = NSA_KV_HEADS, NSA_GROUP, NSA_HEAD_DIM
    NSA_Q, NSA_KV = NSA_HEADS * Dh, G * Dh
    GQK, GV = GLA_HEADS * GLA_DK, GLA_HEADS * GLA_DV
    h = x.reshape(T, D)
    assert w_in.shape[0] == 1, "the final norm is fused into the (single) layer's combine step"
    for l in range(1):
        w = w_in[l]
        o_ng = NSA_Q + 6 * NSA_KV
        o_gq = o_ng + 3 * NSA_HEADS
        o_ga = o_gq + 2 * GQK + 2 * GV
        o_ma = o_ga + GLA_GATE_RANK
        n_small = 3 * NSA_HEADS + GLA_GATE_RANK
        w_small = jnp.concatenate([w[:, o_ng:o_gq], w[:, o_ga:o_ma], jnp.zeros((D, LANES - n_small), F32)],
                                  axis=1).astype(BF16)
        c_kc = NSA_Q // LANES
        c_ks, c_vs, c_kw, c_vw = c_kc + 2 * G, c_kc + 3 * G, c_kc + 4 * G, c_kc + 5 * G
        c_gk = GQK // LANES
        c_gv = c_gk + GQK // LANES
        c_gr = c_gv + GV // LANES

        xn = rmsnorm(h, norm_mix_g[l])
        proj_nsa = matmul(xn, w[:, :o_ng].astype(BF16), BF16, PROJ_TM, PROJ_TN, "proj_nsa").reshape(B, S, -1)
        proj_gla = matmul(xn, w[:, o_gq:o_ga].astype(BF16), BF16, PROJ_TM, PROJ_TN, "proj_gla").reshape(B, S, -1)
        proj_mix = matmul(xn, w[:, o_ma:].astype(BF16), BF16, PROJ_TM, PROJ_TN, "proj_mix")
        small = matmul(xn, w_small, F32, PROJ_TM, LANES, "proj_small")

        pos2 = jnp.stack([cmp_k_pos[l], cmp_v_pos[l]]).reshape(2, 2, CMP_STRIDE * Dh)
        w1 = jnp.stack([cmp_k_w1[l], cmp_v_w1[l]]).astype(BF16)
        b1 = jnp.stack([cmp_k_b1[l], cmp_v_b1[l]]).reshape(2, 1, Dh)
        w2 = jnp.stack([cmp_k_w2[l], cmp_v_w2[l]]).astype(BF16)
        kvc = compress(proj_nsa, c_kc, pos2, w1, b1, w2)
        gates = small[:, :3 * NSA_HEADS].reshape(B, S, G, 3 * R).transpose(0, 2, 1, 3)
        slopes = jnp.exp2(-8.0 * jnp.arange(1, NSA_HEADS + 1, dtype=F32) / NSA_HEADS)
        o_nsa = nsa_attention(proj_nsa, kvc, gates, slopes, col_q=0, col_ks=c_ks, col_vs=c_vs,
                              col_kw=c_kw, col_vw=c_vw)

        xa = small[:, 3 * NSA_HEADS:n_small].reshape(B, S, GLA_GATE_RANK)
        o_gla = gla_attention(proj_gla, xa, gla_w_alpha[l], gla_b_alpha[l], gla_norm_g[l],
                              col_q=0, col_k=c_gk, col_v=c_gv, col_r=c_gr)

        mixed = gated_mix(o_nsa.reshape(T, -1), o_gla.reshape(T, -1), w_branch_nsa[l].astype(BF16),
                          w_branch_gla[l].astype(BF16), proj_mix, 0, D // LANES)
        h = out_proj(mixed, w_out[l].astype(BF16), h)

        n_r = N_GROUPS + N_EXPERTS
        w_r = jnp.concatenate([w_router_group[l], w_router_expert[l], jnp.zeros((D, LANES - n_r), F32)], axis=1)
        b_r = jnp.concatenate([b_router_group[l], b_router_expert[l], jnp.zeros((LANES - n_r,), F32)]).reshape(1, LANES)
        ids, wts, u = router(h, norm_ffn_g[l], w_r, b_r)
        dest, n_rows, tables = _dispatch_tables(ids[:, :2], MOE_TM)
        x_sorted = dispatch_rows(u, dest, n_rows)
        h_sorted = ffn_up(x_sorted, tables, w_exp_gate[l], w_exp_up[l])
        y_sorted = ffn_down(h_sorted, tables, w_exp_down[l])
        h = combine_norm(h, wts, dest, y_sorted, norm_final_g)
    return h.reshape(B, S, D)


def kernel(x, norm_mix_g, w_in, cmp_k_pos, cmp_k_w1, cmp_k_b1, cmp_k_w2, cmp_v_pos, cmp_v_w1, cmp_v_b1, cmp_v_w2, gla_w_alpha, gla_b_alpha, gla_norm_g, w_branch_nsa, w_branch_gla, w_out, norm_ffn_g, w_router_group, b_router_group, w_router_expert, b_router_expert, w_exp_gate, w_exp_up, w_exp_down, norm_final_g):
    return _forward(x, norm_mix_g, w_in, cmp_k_pos, cmp_k_w1, cmp_k_b1, cmp_k_w2, cmp_v_pos, cmp_v_w1, cmp_v_b1,
                    cmp_v_w2, gla_w_alpha, gla_b_alpha, gla_norm_g, w_branch_nsa, w_branch_gla, w_out, norm_ffn_g,
                    w_router_group, b_router_group, w_router_expert, b_router_expert, w_exp_gate, w_exp_up,
                    w_exp_down, norm_final_g)
```

```python
import functools

import jax
import jax.numpy as jnp
from jax import lax
from jax.experimental import pallas as pl
from jax.experimental.pallas import tpu as pltpu

F32 = jnp.float32
BF16 = jnp.bfloat16
I32 = jnp.int32

NSA_HEAD_DIM = 128
NSA_KV_HEADS = 4
NSA_GROUP = 4
NSA_HEADS = NSA_KV_HEADS * NSA_GROUP
CMP_BLOCK = 32
CMP_STRIDE = 16
SEL_BLOCK = 64
SEL_TOPN = 16
WINDOW = 512
BIG = 1e9
GLA_HEADS = 16
GLA_DK = 64
GLA_DV = 128
GLA_GATE_RANK = 16
GLA_TAU = 16.0
N_GROUPS = 4
EXPERTS_PER_GROUP = 8
N_EXPERTS = N_GROUPS * EXPERTS_PER_GROUP
NORM_EPS = 1e-6

LANES = 128
VMEM_LIMIT = 56 * 1024 * 1024
NEG = -1e30

NORM_ROWS = 512
PROJ_TM, PROJ_TN = 2048, 512
NSA_TQ, NSA_TK = 512, 512
GLA_TS, GLA_C, GLA_SUB = 512, 64, 16
MIX_TM, MIX_TN = 1024, 1024
OUT_TM, OUT_TN = 1024, 1024
ROUTE_TM = 256
MOE_TM = 512
MOE_CJ = 512
MOE_CN = 2048
DISPATCH_TOKENS = 512
COMBINE_ROWS = 256


def _cparams(*sem):
    return pltpu.CompilerParams(dimension_semantics=sem, vmem_limit_bytes=VMEM_LIMIT)


def _dot(a, b, **kw):
    return jnp.dot(a, b, preferred_element_type=F32, **kw)


def _dot_nt(a, b, **kw):
    return lax.dot_general(a, b, (((1,), (1,)), ((), ())), preferred_element_type=F32, **kw)


def _dot_tn(a, b, **kw):
    return lax.dot_general(a, b, (((0,), (0,)), ((), ())), preferred_element_type=F32, **kw)


def _sigmoid(x):
    return 1.0 / (1.0 + jnp.exp(-x))


def _masked_softmax(s, mask):
    sm = jnp.where(mask, s, NEG)
    m = jnp.max(sm, axis=-1, keepdims=True)
    e = jnp.where(mask, jnp.exp(sm - m), 0.0)
    return e * (1.0 / jnp.maximum(jnp.sum(e, axis=-1, keepdims=True), 1e-30))


def _rmsnorm_kernel(x_ref, g_ref, o_ref):
    x = x_ref[...]
    ms = jnp.mean(x * x, axis=-1, keepdims=True)
    o_ref[...] = (x * lax.rsqrt(ms + NORM_EPS) * g_ref[...]).astype(o_ref.dtype)


def rmsnorm(x, g, out_dtype=BF16, rows=NORM_ROWS):
    T, D = x.shape
    rows = min(rows, T)
    return pl.pallas_call(
        _rmsnorm_kernel,
        out_shape=jax.ShapeDtypeStruct((T, D), out_dtype),
        grid=(T // rows,),
        in_specs=[pl.BlockSpec((rows, D), lambda i: (i, 0)),
                  pl.BlockSpec((1, D), lambda i: (0, 0))],
        out_specs=pl.BlockSpec((rows, D), lambda i: (i, 0)),
        compiler_params=_cparams("parallel"),
        name="rmsnorm",
    )(x, g.reshape(1, D))


def _mm_kernel(a_ref, b_ref, o_ref):
    o_ref[...] = _dot(a_ref[...], b_ref[...]).astype(o_ref.dtype)


def matmul(a, b, out_dtype, tm, tn, name):
    M, K = a.shape
    N = b.shape[1]
    tm, tn = min(tm, M), min(tn, N)
    return pl.pallas_call(
        _mm_kernel,
        out_shape=jax.ShapeDtypeStruct((M, N), out_dtype),
        grid=(M // tm, N // tn),
        in_specs=[pl.BlockSpec((tm, K), lambda i, j: (i, 0)),
                  pl.BlockSpec((K, tn), lambda i, j: (0, j))],
        out_specs=pl.BlockSpec((tm, tn), lambda i, j: (i, j)),
        compiler_params=_cparams("parallel", "arbitrary"),
        name=name,
    )(a, b)


def _compress_kernel(seq_ref, pos_ref, w1_ref, b1_ref, w2_ref, o_ref):
    seq = seq_ref[0]
    nc = seq.shape[0] // CMP_STRIDE
    x = seq.reshape(nc, CMP_STRIDE * seq.shape[1]).astype(F32)
    half = x.shape[1]
    pos = pos_ref[0]
    w1 = w1_ref[0]
    u0 = _dot((x + pos[0:1, :]).astype(BF16), w1[:half, :])
    u1 = _dot((x + pos[1:2, :]).astype(BF16), w1[half:, :])
    pre = u0 + pltpu.roll(u1, nc - 1, 0) + b1_ref[0]
    h = 0.5 * pre * (1.0 + jnp.tanh(0.7978845608028654 * (pre + 0.044715 * pre * pre * pre)))
    o_ref[0, 0, 0] = _dot(h.astype(BF16), w2_ref[0]).astype(o_ref.dtype)


def compress(proj, col_kc, pos2, w1, b1, w2):
    B, S, _ = proj.shape
    G = NSA_KV_HEADS
    Dh = w2.shape[-1]
    NC, HW = S // CMP_STRIDE, CMP_STRIDE * Dh
    return pl.pallas_call(
        _compress_kernel,
        out_shape=jax.ShapeDtypeStruct((2, B, G, NC, Dh), BF16),
        grid=(2, B, G),
        in_specs=[pl.BlockSpec((1, S, Dh), lambda a, b, g: (b, 0, col_kc + a * G + g)),
                  pl.BlockSpec((1, 2, HW), lambda a, b, g: (a, 0, 0)),
                  pl.BlockSpec((1, 2 * HW, Dh), lambda a, b, g: (a, 0, 0)),
                  pl.BlockSpec((1, 1, Dh), lambda a, b, g: (a, 0, 0)),
                  pl.BlockSpec((1, Dh, Dh), lambda a, b, g: (a, 0, 0))],
        out_specs=pl.BlockSpec((1, 1, 1, NC, Dh), lambda a, b, g: (a, b, g, 0, 0)),
        compiler_params=_cparams("parallel", "parallel", "parallel"),
        name="nsa_compress",
    )(proj, pos2, w1, b1, w2)


MASK_BIG = 2.0 ** 100


def _nsa_kernel(slopes_ref, q_ref, kc_ref, vc_ref, ks_ref, vs_ref, kw_ref, vw_ref, g_ref,
                paux_ref, saux_ref, caux_ref, mmat_ref, o_ref,
                qx_ref, sc_ref, sw_ref, sa_ref, sb_ref, score_ref, m_ref, acc_ref,
                *, TQ, TK, NC, NSEL, NTOP):
    R, Dh = NSA_GROUP, NSA_HEAD_DIM
    g = pl.program_id(1)
    q0 = pl.program_id(2) * TQ
    scale = Dh ** -0.5
    lane = lax.broadcasted_iota(I32, (1, LANES), 1)
    t1 = q0 + lax.broadcasted_iota(I32, (TQ, 1), 0)
    head = lambda r: slice(r * TQ, (r + 1) * TQ)

    def alibi_cols(r):
        hi, lo = slopes_ref[0, g * R + r], slopes_ref[1, g * R + r]
        c = jnp.where(lane == 0, 64.0 * hi, jnp.where(lane == 1, 64.0 * lo,
                                                      jnp.where(lane == 2, hi, jnp.where(lane == 3, lo, 0.0))))
        return jnp.broadcast_to(c, (TQ, LANES))

    for r in range(R):
        qx_ref[head(r), 0:Dh] = (q_ref[0, :, r * Dh:(r + 1) * Dh].astype(F32) * scale).astype(BF16)
        qx_ref[head(r), Dh:Dh + LANES] = alibi_cols(r).astype(BF16)

    def with_ones(v):
        return jnp.concatenate([v, jnp.ones(v.shape, v.dtype)], axis=1)

    def exp_pv(s, v1):
        m = jnp.broadcast_to(jnp.max(s, axis=-1, keepdims=True), (TQ, LANES))
        e = [jnp.exp(s[:, c * LANES:(c + 1) * LANES] - m) for c in range(s.shape[1] // LANES)]
        return e, _dot(jnp.concatenate(e, axis=1).astype(BF16), v1)

    WK = WINDOW + TQ
    ws = pl.multiple_of(jnp.maximum(q0 - WINDOW, 0), TQ)
    sc_ref[...] = _dot_nt(qx_ref[...], jnp.concatenate([kc_ref[0, 0, 0], caux_ref[...]], axis=1))
    sw_ref[...] = _dot_nt(qx_ref[...], jnp.concatenate([kw_ref[0, pl.ds(ws, WK), :], paux_ref[pl.ds(ws, WK), :]],
                                                       axis=1))

    cmp_end = lax.broadcasted_iota(I32, (1, NC), 1) * CMP_STRIDE + (CMP_BLOCK - 1)
    cmp_bias = jnp.where(cmp_end <= t1, 0.0, NEG)
    row_ok = q0 + lax.broadcasted_iota(I32, (TQ, LANES), 0) >= CMP_BLOCK - 1
    vc1 = with_ones(vc_ref[0, 0, 0])
    o_cmp = []
    imp = None
    for r in range(R):
        e, o2 = exp_pv(sc_ref[head(r), :] + cmp_bias, vc1)
        inv = jnp.where(row_ok, 1.0 / jnp.maximum(o2[:, Dh:], 1e-30), 0.0)
        o_cmp.append(o2[:, :Dh] * inv)
        p = jnp.concatenate([ec * inv for ec in e], axis=1)
        imp = p if imp is None else imp + p

    vw1 = with_ones(vw_ref[0, pl.ds(ws, WK), :])
    dw = t1 - (ws + lax.broadcasted_iota(I32, (1, WK), 1))
    win_bias = jnp.where((dw >= 0) & (dw < WINDOW), 0.0, NEG)
    o_win = []
    for r in range(R):
        _, o2 = exp_pv(sw_ref[head(r), :] + win_bias, vw1)
        o_win.append(o2[:, :Dh] * (1.0 / jnp.maximum(o2[:, Dh:], 1e-30)))

    mm = mmat_ref[...]
    i1 = imp.astype(BF16)
    rem = imp - i1.astype(F32)
    i2 = rem.astype(BF16)
    i3 = (rem - i2.astype(F32)).astype(BF16)
    imp_sel = _dot_nt(mm, i1) + _dot_nt(mm, i2) + _dot_nt(mm, i3)

    cur = (q0 + lax.broadcasted_iota(I32, (1, TQ), 1)) // SEL_BLOCK
    blk = lax.broadcasted_iota(I32, (NSEL, 1), 0)
    forced = (blk == 0) | (blk == cur) | (blk == cur - 1)
    score = jnp.where(blk <= cur, jnp.where(forced, BIG, imp_sel), -BIG)
    score_ref[...] = score

    def rank_pair(i2, cnt):
        for d in range(2):
            ii = 2 * i2 + d
            other = score_ref[pl.ds(ii, 1), :]
            beats = (other > score) | ((other == score) & (blk > ii))
            cnt = cnt + jnp.where(beats, 1.0, 0.0)
        return cnt

    n_blk = (q0 + TQ) // SEL_BLOCK
    cnt = lax.fori_loop(0, n_blk // 2, rank_pair, jnp.zeros((NSEL, TQ), F32))
    unsel = jnp.where(cnt < NTOP, 0.0, -1.0).astype(BF16)
    place = jnp.where(lax.broadcasted_iota(I32, (NSEL, LANES), 1)
                      == lax.broadcasted_iota(I32, (NSEL, LANES), 0) + LANES // 2, 1.0, 0.0).astype(BF16)
    unsel_l = _dot_tn(unsel, place)
    for r in range(R):
        qx_ref[head(r), Dh:Dh + LANES] = (alibi_cols(r) + unsel_l).astype(BF16)

    m_ref[...] = jnp.full(m_ref.shape, NEG, F32)
    acc_ref[...] = jnp.zeros(acc_ref.shape, F32)
    pos_in_tile = lax.broadcasted_iota(I32, (1, TK), 1)

    def issue_scores(kt, s_ref):
        k0 = pl.multiple_of(kt * TK, TK)
        kk = jnp.concatenate([ks_ref[0, pl.ds(k0, TK), :], saux_ref[pl.ds(k0, TK), :]], axis=1)
        s_ref[...] = _dot_nt(qx_ref[...], kk)

    def consume_scores(kt, s_ref, causal):
        k0 = pl.multiple_of(kt * TK, TK)
        vv1 = with_ones(vs_ref[0, pl.ds(k0, TK), :])
        if causal:
            causal_bias = jnp.where(t1 >= k0 + pos_in_tile, 0.0, -MASK_BIG)
        for r in range(R):
            sc = s_ref[head(r), :]
            if causal:
                sc = sc + causal_bias
            m_old = m_ref[r]
            m_new = jnp.maximum(m_old, jnp.broadcast_to(jnp.max(sc, axis=-1, keepdims=True), (TQ, LANES)))
            alpha = jnp.exp(m_old - m_new)
            p = jnp.concatenate([jnp.exp(sc[:, c * LANES:(c + 1) * LANES] - m_new) for c in range(TK // LANES)],
                                axis=1).astype(BF16)
            m_ref[r] = m_new
            acc_ref[r] = jnp.concatenate([alpha, alpha], axis=1) * acc_ref[r] + _dot(p, vv1)

    def sel_pair(k, carry):
        issue_scores(2 * k + 1, sb_ref)
        consume_scores(2 * k, sa_ref, False)
        issue_scores(2 * k + 2, sa_ref)
        consume_scores(2 * k + 1, sb_ref, False)
        return carry

    n_below = q0 // TK
    issue_scores(0, sa_ref)
    lax.fori_loop(0, n_below // 2, sel_pair, 0)

    @pl.when(n_below % 2 == 1)
    def _():
        issue_scores(n_below, sb_ref)
        consume_scores(n_below - 1, sa_ref, False)
        consume_scores(n_below, sb_ref, True)

    @pl.when(n_below % 2 == 0)
    def _():
        consume_scores(n_below, sa_ref, True)

    gt = _sigmoid(g_ref[0, 0])
    for r in range(R):
        o_sel = acc_ref[r, :, 0:Dh] * (1.0 / jnp.maximum(acc_ref[r, :, Dh:2 * Dh], 1e-30))
        o = (gt[:, 3 * r:3 * r + 1] * o_cmp[r] + gt[:, 3 * r + 1:3 * r + 2] * o_sel
             + gt[:, 3 * r + 2:3 * r + 3] * o_win[r])
        o_ref[0, :, r * Dh:(r + 1) * Dh] = o.astype(o_ref.dtype)


def _nsa_constants(S, NC, NSEL):
    def pos_cols(pos):
        col = jnp.arange(LANES)[None, :]
        hi, lo = (pos // 64)[:, None], (pos % 64)[:, None]
        return jnp.where(col < 2, hi, jnp.where(col < 4, lo, 0)).astype(F32)

    pos = jnp.arange(S)
    paux = pos_cols(pos)
    onehot = (jnp.arange(LANES)[None, :] - LANES // 2 == (pos // SEL_BLOCK)[:, None]) & (jnp.arange(LANES)[None, :] >= LANES // 2)
    saux = paux + jnp.where(onehot, MASK_BIG, 0.0)
    caux = pos_cols(jnp.arange(NC) * CMP_STRIDE + (CMP_BLOCK - 1))
    d = jnp.arange(NC)[None, :] - 4 * jnp.arange(NSEL)[:, None]
    mmat = jnp.where((d == -1) | (d == 3), 1.0, jnp.where((d >= 0) & (d <= 2), 2.0, 0.0))
    return paux.astype(BF16), saux.astype(BF16), caux.astype(BF16), mmat.astype(BF16)


def nsa_attention(proj, kvc, gates, slopes, *, col_q, col_ks, col_vs, col_kw, col_vw):
    B, S, _ = proj.shape
    G, R, Dh = NSA_KV_HEADS, NSA_GROUP, NSA_HEAD_DIM
    NC = kvc.shape[3]
    NSEL = S // SEL_BLOCK
    TQ, TK = min(NSA_TQ, S), min(NSA_TK, S)
    assert S % TQ == 0 and TK % TQ == 0 and S % TK == 0 and S >= WINDOW + TQ and NC == S // CMP_STRIDE
    assert NSEL <= LANES // 2 and S // 64 <= 256 and TQ % (2 * SEL_BLOCK) == 0
    s_hi = slopes.astype(BF16).astype(F32)
    s_lo = (slopes - s_hi).astype(BF16).astype(F32)
    paux, saux, caux, mmat = _nsa_constants(S, NC, NSEL)
    kern = functools.partial(_nsa_kernel, TQ=TQ, TK=TK, NC=NC, NSEL=NSEL, NTOP=min(SEL_TOPN, NSEL))
    kv_spec = lambda col: pl.BlockSpec((1, S, Dh), lambda b, g, i: (b, 0, col + g))
    whole = lambda a: pl.BlockSpec(a.shape, lambda b, g, i: (0, 0))
    return pl.pallas_call(
        kern,
        out_shape=jax.ShapeDtypeStruct((B, S, G * R * Dh), BF16),
        grid=(B, G, S // TQ),
        in_specs=[pl.BlockSpec(memory_space=pltpu.SMEM),
                  pl.BlockSpec((1, TQ, R * Dh), lambda b, g, i: (b, i, col_q // R + g)),
                  pl.BlockSpec((1, 1, 1, NC, Dh), lambda b, g, i: (0, b, g, 0, 0)),
                  pl.BlockSpec((1, 1, 1, NC, Dh), lambda b, g, i: (1, b, g, 0, 0)),
                  kv_spec(col_ks), kv_spec(col_vs), kv_spec(col_kw), kv_spec(col_vw),
                  pl.BlockSpec((1, 1, TQ, 3 * R), lambda b, g, i: (b, g, i, 0)),
                  whole(paux), whole(saux), whole(caux), whole(mmat)],
        out_specs=pl.BlockSpec((1, TQ, R * Dh), lambda b, g, i: (b, i, g)),
        scratch_shapes=[pltpu.VMEM((R * TQ, Dh + LANES), BF16), pltpu.VMEM((R * TQ, NC), F32),
                        pltpu.VMEM((R * TQ, WINDOW + TQ), F32), pltpu.VMEM((R * TQ, TK), F32),
                        pltpu.VMEM((R * TQ, TK), F32),
                        pltpu.VMEM((NSEL, TQ), F32), pltpu.VMEM((R, TQ, LANES), F32),
                        pltpu.VMEM((R, TQ, 2 * Dh), F32)],
        compiler_params=_cparams("parallel", "parallel", "arbitrary"),
        name="nsa_attention",
    )(jnp.stack([s_hi, s_lo]), proj, kvc, kvc, proj, proj, proj, proj, gates, paux, saux, caux, mmat)


def _gla_kernel(q_ref, k_ref, v_ref, r_ref, xa_ref, wa_ref, ba_ref, ng_ref, tri_ref, o_ref,
                st_ref, b_ref, q2_ref, kh_ref, qe_ref, kd_ref, oacc_ref, *, TS, C):
    DK2, DV2 = 2 * GLA_DK, 2 * GLA_DV
    NCH = TS // C
    LEVELS = [C >> (i + 1) for i in range(C.bit_length() - 1)]

    @pl.when(pl.program_id(2) == 0)
    def _():
        st_ref[...] = jnp.zeros_like(st_ref)

    z = _dot(xa_ref[0], wa_ref[...], precision=lax.Precision.HIGHEST) + ba_ref[...]
    la = (jnp.minimum(z, 0.0) - jnp.log(1.0 + jnp.exp(-jnp.abs(z)))) * (1.0 / GLA_TAU)
    l1 = la.astype(BF16)
    lr = la - l1.astype(F32)
    l2 = lr.astype(BF16)
    l3 = (lr - l2.astype(F32)).astype(BF16)
    tri = tri_ref[...]
    b = _dot(tri, l1) + _dot(tri, l2) + _dot(tri, l3)
    b_ref[...] = b

    head_a = lax.broadcasted_iota(I32, (1, DK2), 1) < GLA_DK
    row = lax.broadcasted_iota(I32, (TS, 1), 0)
    q = q_ref[0].astype(F32) * (GLA_DK ** -0.5)
    k = k_ref[0].astype(F32)

    def ref_rows(h):
        if h >= 4:
            g = TS // (2 * h)
            return jnp.broadcast_to(b.reshape(g, 2 * h, DK2)[:, h - 1:h, :], (g, 2 * h, DK2)).reshape(TS, DK2)
        dn1 = pltpu.roll(b, 1, 0)
        if h == 1:
            return jnp.where(row % 2 == 0, b, dn1)
        m4 = row % 4
        return jnp.where(m4 == 0, pltpu.roll(b, TS - 1, 0),
                         jnp.where(m4 == 1, b, jnp.where(m4 == 2, dn1, pltpu.roll(b, 2, 0))))

    def put(lvl, qh, kh):
        q2_ref[lvl, :, 0:C, :] = jnp.where(head_a, qh, 0.0).astype(BF16).reshape(NCH, C, DK2)
        q2_ref[lvl, :, C:2 * C, :] = jnp.where(head_a, 0.0, qh).astype(BF16).reshape(NCH, C, DK2)
        kh_ref[lvl] = kh.astype(BF16)

    put(0, q, k)
    for lvl, h in enumerate(LEVELS, start=1):
        r = ref_rows(h)
        put(lvl, q * jnp.exp(jnp.minimum(b - r, 0.0)), k * jnp.exp(jnp.minimum(r - b, 0.0)))
    b_last = jnp.broadcast_to(b.reshape(NCH, C, DK2)[:, C - 1:C, :], (NCH, C, DK2)).reshape(TS, DK2)
    qe_ref[...] = (q * jnp.exp(b)).astype(BF16)
    kd_ref[...] = (k * jnp.exp(b_last - b)).astype(BF16)

    row2 = lax.broadcasted_iota(I32, (2 * C, 1), 0) % C
    col = lax.broadcasted_iota(I32, (1, C), 1)
    masks = [row2 == col] + [(row2 // (2 * h) == col // (2 * h)) & (row2 % (2 * h) >= h) & (col % (2 * h) < h)
                             for h in LEVELS]
    vk_same = (lax.broadcasted_iota(I32, (DV2, DK2), 0) < GLA_DV) == (lax.broadcasted_iota(I32, (DV2, DK2), 1) < GLA_DK)
    for c in range(NCH):
        rows = slice(c * C, (c + 1) * C)
        vb = v_ref[0, rows, :]
        a2 = jnp.zeros((2 * C, C), F32)
        for lvl, mask in enumerate(masks):
            a2 = a2 + jnp.where(mask, _dot_nt(q2_ref[lvl, c], kh_ref[lvl, rows, :]), 0.0)
        a2 = a2.astype(BF16)
        o = jnp.concatenate([_dot(a2[0:C], vb[:, :GLA_DV]), _dot(a2[C:2 * C], vb[:, GLA_DV:])], axis=1)
        st = st_ref[...]
        oacc_ref[rows, :] = o + _dot_nt(qe_ref[rows, :], st.astype(BF16))
        ds = jnp.where(vk_same, _dot_tn(vb, kd_ref[rows, :]), 0.0)
        st_ref[...] = st * jnp.exp(b_ref[(c + 1) * C - 1:(c + 1) * C, :]) + ds

    def ln(x):
        mu = jnp.mean(x, axis=-1, keepdims=True)
        xc = x - mu
        return xc * lax.rsqrt(jnp.mean(xc * xc, axis=-1, keepdims=True) + NORM_EPS)

    o = oacc_ref[...]
    on = jnp.concatenate([ln(o[:, :GLA_DV]), ln(o[:, GLA_DV:])], axis=1) * ng_ref[...]
    rr = r_ref[0].astype(F32)
    o_ref[0] = (on * (rr * _sigmoid(rr))).astype(o_ref.dtype)


def gla_attention(proj, xa, w_alpha, b_alpha, norm_g, *, col_q, col_k, col_v, col_r):
    B, S, _ = proj.shape
    HP = GLA_HEADS // 2
    DK2, DV2 = 2 * GLA_DK, 2 * GLA_DV
    TS, C = min(GLA_TS, S), GLA_C
    assert S % TS == 0 and TS % C == 0 and C & (C - 1) == 0 and C >= 16
    n_lvl = C.bit_length()
    idx = jnp.arange(TS)
    tri = ((idx[None, :] <= idx[:, None]) & (idx[None, :] // C == idx[:, None] // C)).astype(BF16)
    kern = functools.partial(_gla_kernel, TS=TS, C=C)
    return pl.pallas_call(
        kern,
        out_shape=jax.ShapeDtypeStruct((B, S, GLA_HEADS * GLA_DV), BF16),
        grid=(B, HP, S // TS),
        in_specs=[pl.BlockSpec((1, TS, DK2), lambda b, p, s: (b, s, col_q + p)),
                  pl.BlockSpec((1, TS, DK2), lambda b, p, s: (b, s, col_k + p)),
                  pl.BlockSpec((1, TS, DV2), lambda b, p, s: (b, s, col_v // 2 + p)),
                  pl.BlockSpec((1, TS, DV2), lambda b, p, s: (b, s, col_r // 2 + p)),
                  pl.BlockSpec((1, TS, GLA_GATE_RANK), lambda b, p, s: (b, s, 0)),
                  pl.BlockSpec((GLA_GATE_RANK, DK2), lambda b, p, s: (0, p)),
                  pl.BlockSpec((1, DK2), lambda b, p, s: (0, p)),
                  pl.BlockSpec((1, DV2), lambda b, p, s: (0, p)),
                  pl.BlockSpec((TS, TS), lambda b, p, s: (0, 0))],
        out_specs=pl.BlockSpec((1, TS, DV2), lambda b, p, s: (b, s, p)),
        scratch_shapes=[pltpu.VMEM((DV2, DK2), F32), pltpu.VMEM((TS, DK2), F32),
                        pltpu.VMEM((n_lvl, TS // C, 2 * C, DK2), BF16), pltpu.VMEM((n_lvl, TS, DK2), BF16),
                        pltpu.VMEM((TS, DK2), BF16), pltpu.VMEM((TS, DK2), BF16), pltpu.VMEM((TS, DV2), F32)],
        compiler_params=_cparams("parallel", "parallel", "arbitrary"),
        name="gla_attention",
    )(proj, proj, proj, proj, xa, w_alpha, b_alpha.reshape(1, -1), norm_g.reshape(1, -1), tri)


def _mix_kernel(oa_ref, ob_ref, wa_ref, wb_ref, ma_ref, mb_ref, o_ref):
    ya = _dot(oa_ref[...], wa_ref[...])
    yb = _dot(ob_ref[...], wb_ref[...])
    o_ref[...] = (_sigmoid(ma_ref[...].astype(F32)) * ya + _sigmoid(mb_ref[...].astype(F32)) * yb).astype(o_ref.dtype)


def gated_mix(o_nsa, o_gla, wa, wb, proj2d, col_ma, col_mb):
    T, KA = o_nsa.shape
    KB = o_gla.shape[1]
    N = wa.shape[1]
    tm, tn = min(MIX_TM, T), min(MIX_TN, N)
    ca, cb = col_ma * LANES // tn, col_mb * LANES // tn
    assert (col_ma * LANES) % tn == 0 and (col_mb * LANES) % tn == 0
    return pl.pallas_call(
        _mix_kernel,
        out_shape=jax.ShapeDtypeStruct((T, N), BF16),
        grid=(T // tm, N // tn),
        in_specs=[pl.BlockSpec((tm, KA), lambda i, j: (i, 0)),
                  pl.BlockSpec((tm, KB), lambda i, j: (i, 0)),
                  pl.BlockSpec((KA, tn), lambda i, j: (0, j)),
                  pl.BlockSpec((KB, tn), lambda i, j: (0, j)),
                  pl.BlockSpec((tm, tn), lambda i, j: (i, ca + j)),
                  pl.BlockSpec((tm, tn), lambda i, j: (i, cb + j))],
        out_specs=pl.BlockSpec((tm, tn), lambda i, j: (i, j)),
        compiler_params=_cparams("parallel", "arbitrary"),
        name="gated_mix",
    )(o_nsa, o_gla, wa, wb, proj2d, proj2d)


def _out_kernel(a_ref, w_ref, x_ref, o_ref):
    o_ref[...] = x_ref[...] + _dot(a_ref[...], w_ref[...])


def out_proj(mixed, w, x):
    T, K = mixed.shape
    N = w.shape[1]
    tm, tn = min(OUT_TM, T), min(OUT_TN, N)
    return pl.pallas_call(
        _out_kernel,
        out_shape=jax.ShapeDtypeStruct((T, N), F32),
        grid=(T // tm, N // tn),
        in_specs=[pl.BlockSpec((tm, K), lambda i, j: (i, 0)),
                  pl.BlockSpec((K, tn), lambda i, j: (0, j)),
                  pl.BlockSpec((tm, tn), lambda i, j: (i, j))],
        out_specs=pl.BlockSpec((tm, tn), lambda i, j: (i, j)),
        compiler_params=_cparams("parallel", "arbitrary"),
        name="out_proj",
    )(mixed, w, x)


def _router_kernel(h_ref, g_ref, w_ref, b_ref, id_ref, wt_ref, u_ref):
    x = h_ref[...]
    u = x * lax.rsqrt(jnp.mean(x * x, axis=-1, keepdims=True) + NORM_EPS) * g_ref[...]
    u_hi = u.astype(BF16)
    u_ref[...] = u_hi.reshape(u_ref.shape)
    u_lo = (u - u_hi.astype(F32)).astype(BF16)
    logit = _dot(u_hi, w_ref[0]) + _dot(u_lo, w_ref[0]) + _dot(u_hi, w_ref[1]) + b_ref[...]
    lane = lax.broadcasted_iota(I32, logit.shape, 1)
    big = jnp.int32(1 << 20)
    gmask = lane < N_GROUPS
    gl = jnp.where(gmask, logit, NEG)
    gmax = jnp.max(gl, axis=-1, keepdims=True)
    g_star = jnp.min(jnp.where(gmask & (gl == gmax), lane, big), axis=-1, keepdims=True)
    p_group = 1.0 / jnp.sum(jnp.where(gmask, jnp.exp(gl - gmax), 0.0), axis=-1, keepdims=True)
    e_lo = N_GROUPS + EXPERTS_PER_GROUP * g_star
    emask = (lane >= e_lo) & (lane < e_lo + EXPERTS_PER_GROUP)
    el = jnp.where(emask, logit, NEG)
    m1 = jnp.max(el, axis=-1, keepdims=True)
    i1 = jnp.min(jnp.where(emask & (el == m1), lane, big), axis=-1, keepdims=True)
    emask2 = emask & (lane != i1)
    el2 = jnp.where(emask2, logit, NEG)
    m2 = jnp.max(el2, axis=-1, keepdims=True)
    i2 = jnp.min(jnp.where(emask2 & (el2 == m2), lane, big), axis=-1, keepdims=True)
    e2 = jnp.exp(m2 - m1)
    w1 = p_group / (1.0 + e2)
    w2 = p_group * e2 / (1.0 + e2)
    id_ref[...] = jnp.where(lane == 0, i1 - N_GROUPS, jnp.where(lane == 1, i2 - N_GROUPS, 0))
    wt_ref[...] = jnp.where(lane == 0, w1, jnp.where(lane == 1, w2, 0.0))


def router(h, g, w_r, b_r):
    T, D = h.shape
    tm = min(ROUTE_TM, T)
    w_hi = w_r.astype(BF16)
    return pl.pallas_call(
        _router_kernel,
        out_shape=(jax.ShapeDtypeStruct((T, LANES), I32), jax.ShapeDtypeStruct((T, LANES), F32),
                   jax.ShapeDtypeStruct((T, D // LANES, LANES), BF16)),
        grid=(T // tm,),
        in_specs=[pl.BlockSpec((tm, D), lambda i: (i, 0)),
                  pl.BlockSpec((1, D), lambda i: (0, 0)),
                  pl.BlockSpec((2, D, LANES), lambda i: (0, 0, 0)),
                  pl.BlockSpec((1, LANES), lambda i: (0, 0))],
        out_specs=(pl.BlockSpec((tm, LANES), lambda i: (i, 0)), pl.BlockSpec((tm, LANES), lambda i: (i, 0)),
                   pl.BlockSpec((tm, D // LANES, LANES), lambda i: (i, 0, 0))),
        compiler_params=_cparams("parallel"),
        name="moe_router",
    )(h, g.reshape(1, D), jnp.stack([w_hi, (w_r - w_hi.astype(F32)).astype(BF16)]), b_r)


def _row_copy(src_ref, dst_ref, sem, src_row, dst_row):
    return pltpu.make_async_copy(src_ref.at[pl.ds(src_row, 1)], dst_ref.at[pl.ds(dst_row, 1)], sem)


def _dispatch_rows_kernel(dest_ref, u_ref, init_hbm, out_hbm, sem, *, toks):
    def start(t, c):
        for k in range(2):
            _row_copy(u_ref, out_hbm, sem, t, dest_ref[0, 0, 2 * t + k]).start(priority=k)
        return c

    def wait(t, c):
        for k in range(2):
            _row_copy(u_ref, out_hbm, sem, t, 0).wait()
        return c

    lax.fori_loop(0, toks, start, 0)
    lax.fori_loop(0, toks, wait, 0)


def dispatch_rows(u3, dest, n_rows):
    T = u3.shape[0]
    toks = min(DISPATCH_TOKENS, T)
    assert T % toks == 0
    out_shape = jax.ShapeDtypeStruct((n_rows,) + u3.shape[1:], u3.dtype)
    return pl.pallas_call(
        functools.partial(_dispatch_rows_kernel, toks=toks),
        out_shape=out_shape,
        grid=(T // toks,),
        in_specs=[pl.BlockSpec((1, 1, 2 * toks), lambda i: (i, 0, 0), memory_space=pltpu.SMEM),
                  pl.BlockSpec((toks,) + u3.shape[1:], lambda i: (i, 0, 0)),
                  pl.BlockSpec(memory_space=pl.ANY)],
        out_specs=pl.BlockSpec(memory_space=pl.ANY),
        scratch_shapes=[pltpu.SemaphoreType.DMA],
        input_output_aliases={2: 0},
        compiler_params=_cparams("arbitrary"),
        name="moe_dispatch_rows",
    )(dest.reshape(T // toks, 1, 2 * toks), u3, jnp.zeros(out_shape.shape, out_shape.dtype))


def _ffn_up_kernel(te_ref, new_ref, nv_ref, x_ref, wg_ref, wu_ref, o_ref, wg_bf, wu_bf):
    i = pl.program_id(1)
    used = i < nv_ref[0]
    fresh = new_ref[i] == 1

    def swiglu(wg, wu):
        x = x_ref[...].reshape(x_ref.shape[0], wg_bf.shape[0])
        a = _dot(x, wg)
        o_ref[...] = (a * _sigmoid(a) * _dot(x, wu)).astype(o_ref.dtype)

    @pl.when(fresh)
    def _():
        wg, wu = wg_ref[0].astype(BF16), wu_ref[0].astype(BF16)
        wg_bf[...] = wg
        wu_bf[...] = wu
        swiglu(wg, wu)

    @pl.when(jnp.logical_and(used, jnp.logical_not(fresh)))
    def _():
        swiglu(wg_bf[...], wu_bf[...])

    @pl.when(jnp.logical_not(used))
    def _():
        o_ref[...] = jnp.zeros_like(o_ref)


def ffn_up(x_sorted, tables, w_gate, w_up):
    NP = x_sorted.shape[0]
    E, D, DE = w_gate.shape
    tm, cj = MOE_TM, min(MOE_CJ, DE)
    return pl.pallas_call(
        _ffn_up_kernel,
        out_shape=jax.ShapeDtypeStruct((NP, DE), BF16),
        grid_spec=pltpu.PrefetchScalarGridSpec(
            num_scalar_prefetch=3,
            grid=(DE // cj, NP // tm),
            in_specs=[pl.BlockSpec((tm,) + x_sorted.shape[1:], lambda j, i, te, nw, nv: (i, 0, 0)),
                      pl.BlockSpec((1, D, cj), lambda j, i, te, nw, nv: (te[i], 0, j)),
                      pl.BlockSpec((1, D, cj), lambda j, i, te, nw, nv: (te[i], 0, j))],
            out_specs=pl.BlockSpec((tm, cj), lambda j, i, te, nw, nv: (i, j)),
            scratch_shapes=[pltpu.VMEM((D, cj), BF16), pltpu.VMEM((D, cj), BF16)]),
        compiler_params=_cparams("arbitrary", "arbitrary"),
        name="moe_ffn_up",
    )(*tables, x_sorted, w_gate, w_up)


def _ffn_down_kernel(te_ref, new_ref, nv_ref, h_ref, wd_ref, o_ref, wd_bf):
    i = pl.program_id(1)
    used = i < nv_ref[0]
    fresh = new_ref[i] == 1

    def down(wd):
        o_ref[...] = _dot(h_ref[...], wd).astype(o_ref.dtype).reshape(o_ref.shape)

    @pl.when(fresh)
    def _():
        wd = wd_ref[0].astype(BF16)
        wd_bf[...] = wd
        down(wd)

    @pl.when(jnp.logical_and(used, jnp.logical_not(fresh)))
    def _():
        down(wd_bf[...])

    @pl.when(jnp.logical_not(used))
    def _():
        o_ref[...] = jnp.zeros_like(o_ref)


def ffn_down(h_sorted, tables, w_down):
    NP, DE = h_sorted.shape
    E, _, D = w_down.shape
    tm, cn = MOE_TM, min(MOE_CN, D)
    return pl.pallas_call(
        _ffn_down_kernel,
        out_shape=jax.ShapeDtypeStruct((NP, D // LANES, LANES), BF16),
        grid_spec=pltpu.PrefetchScalarGridSpec(
            num_scalar_prefetch=3,
            grid=(D // cn, NP // tm),
            in_specs=[pl.BlockSpec((tm, DE), lambda j, i, te, nw, nv: (i, 0)),
                      pl.BlockSpec((1, DE, cn), lambda j, i, te, nw, nv: (te[i], 0, j))],
            out_specs=pl.BlockSpec((tm, cn // LANES, LANES), lambda j, i, te, nw, nv: (i, j, 0)),
            scratch_shapes=[pltpu.VMEM((DE, cn), BF16)]),
        compiler_params=_cparams("arbitrary", "arbitrary"),
        name="moe_ffn_down",
    )(*tables, h_sorted, w_down)


def _combine_kernel(idx_ref, nxt_ref, h_ref, wt_ref, y_hbm, g_ref, o_ref, buf, sem, *, rows):
    i = pl.program_id(0)
    slot = i % 2

    def issue(table, s):
        def body(r, c):
            _row_copy(y_hbm, buf.at[s, 0], sem.at[s], table[0, 0, 2 * r], r).start(priority=0)
            _row_copy(y_hbm, buf.at[s, 1], sem.at[s], table[0, 0, 2 * r + 1], r).start(priority=1)
            return c
        lax.fori_loop(0, rows, body, 0)

    @pl.when(i == 0)
    def _():
        issue(idx_ref, 0)

    @pl.when(i + 1 < pl.num_programs(0))
    def _():
        issue(nxt_ref, 1 - slot)

    def wait(r, c):
        _row_copy(y_hbm, buf.at[slot, 0], sem.at[slot], 0, r).wait()
        _row_copy(y_hbm, buf.at[slot, 1], sem.at[slot], 0, r).wait()
        return c

    lax.fori_loop(0, rows, wait, 0)
    wt = wt_ref[...]
    y0 = buf[slot, 0].reshape(h_ref.shape).astype(F32)
    y1 = buf[slot, 1].reshape(h_ref.shape).astype(F32)
    x = h_ref[...] + wt[:, 0:1] * y0 + wt[:, 1:2] * y1
    o_ref[...] = x * lax.rsqrt(jnp.mean(x * x, axis=-1, keepdims=True) + NORM_EPS) * g_ref[...]


def combine_norm(h, wts, dest, y3, g):
    T, D = h.shape
    rows = min(COMBINE_ROWS, T)
    n = T // rows
    table = dest.reshape(n, 1, 2 * rows)
    return pl.pallas_call(
        functools.partial(_combine_kernel, rows=rows),
        out_shape=jax.ShapeDtypeStruct((T, D), F32),
        grid=(n,),
        in_specs=[pl.BlockSpec((1, 1, 2 * rows), lambda i: (i, 0, 0), memory_space=pltpu.SMEM),
                  pl.BlockSpec((1, 1, 2 * rows), lambda i: (jnp.minimum(i + 1, n - 1), 0, 0),
                               memory_space=pltpu.SMEM),
                  pl.BlockSpec((rows, D), lambda i: (i, 0)),
                  pl.BlockSpec((rows, LANES), lambda i: (i, 0)),
                  pl.BlockSpec(memory_space=pl.ANY),
                  pl.BlockSpec((1, D), lambda i: (0, 0))],
        out_specs=pl.BlockSpec((rows, D), lambda i: (i, 0)),
        scratch_shapes=[pltpu.VMEM((2, 2, rows) + y3.shape[1:], y3.dtype), pltpu.SemaphoreType.DMA((2,))],
        compiler_params=_cparams("arbitrary"),
        name="moe_combine_norm",
    )(table, table, h, wts, y3, g.reshape(1, D))


def _dispatch_tables(ids, tm):
    T = ids.shape[0]
    E = N_EXPERTS
    eid = ids.reshape(-1)
    onehot = (eid[:, None] == jnp.arange(E, dtype=I32)[None, :]).astype(I32)
    csum = jnp.cumsum(onehot, axis=0)
    rank = jnp.sum(csum * onehot, axis=1) - 1
    counts = csum[-1]
    padded = ((counts + tm - 1) // tm) * tm
    ends = jnp.cumsum(padded)
    dest = jnp.sum(onehot * (ends - padded)[None, :], axis=1) + rank
    n_rows = 2 * T + E * tm
    tile_start = jnp.arange(n_rows // tm, dtype=I32) * tm
    tile_expert = jnp.sum((ends[None, :] <= tile_start[:, None]).astype(I32), axis=1)
    last_used = jnp.max(jnp.where(counts > 0, jnp.arange(E, dtype=I32), 0))
    tile_expert = jnp.minimum(tile_expert, last_used)
    tile_new = jnp.concatenate([jnp.ones((1,), I32), (tile_expert[1:] != tile_expert[:-1]).astype(I32)])
    n_tiles_used = (ends[-1] // tm).reshape(1)
    return dest.astype(I32), n_rows, (tile_expert, tile_new, n_tiles_used.astype(I32))


def _forward(x, norm_mix_g, w_in, cmp_k_pos, cmp_k_w1, cmp_k_b1, cmp_k_w2, cmp_v_pos, cmp_v_w1, cmp_v_b1, cmp_v_w2,
             gla_w_alpha, gla_b_alpha, gla_norm_g, w_branch_nsa, w_branch_gla, w_out, norm_ffn_g,
             w_router_group, b_router_group, w_router_expert, b_router_expert, w_exp_gate, w_exp_up, w_exp_down,
             norm_final_g):
    B, S, D = x.shape
    T = B * S
    G, R, Dh = NSA_KV_HEADS, NSA_GROUP, NSA_HEAD_DIM
    NSA_Q, NSA_KV = NSA_HEADS * Dh, G * Dh
    GQK, GV = GLA_HEADS * GLA_DK, GLA_HEADS * GLA_DV
    h = x.reshape(T, D)
    assert w_in.shape[0] == 1, "the final norm is fused into the (single) layer's combine step"
    for l in range(1):
        w = w_in[l]
        o_ng = NSA_Q + 6 * NSA_KV
        o_gq = o_ng + 3 * NSA_HEADS
        o_ga = o_gq + 2 * GQK + 2 * GV
        o_ma = o_ga + GLA_GATE_RANK
        n_small = 3 * NSA_HEADS + GLA_GATE_RANK
        w_small = jnp.concatenate([w[:, o_ng:o_gq], w[:, o_ga:o_ma], jnp.zeros((D, LANES - n_small), F32)],
                                  axis=1).astype(BF16)
        c_kc = NSA_Q // LANES
        c_ks, c_vs, c_kw, c_vw = c_kc + 2 * G, c_kc + 3 * G, c_kc + 4 * G, c_kc + 5 * G
        c_gk = GQK // LANES
        c_gv = c_gk + GQK // LANES
        c_gr = c_gv + GV // LANES

        xn = rmsnorm(h, norm_mix_g[l])
        proj_nsa = matmul(xn, w[:, :o_ng].astype(BF16), BF16, PROJ_TM, PROJ_TN, "proj_nsa").reshape(B, S, -1)
        proj_gla = matmul(xn, w[:, o_gq:o_ga].astype(BF16), BF16, PROJ_TM, PROJ_TN, "proj_gla").reshape(B, S, -1)
        proj_mix = matmul(xn, w[:, o_ma:].astype(BF16), BF16, PROJ_TM, PROJ_TN, "proj_mix")
        small = matmul(xn, w_small, F32, PROJ_TM, LANES, "proj_small")

        pos2 = jnp.stack([cmp_k_pos[l], cmp_v_pos[l]]).reshape(2, 2, CMP_STRIDE * Dh)
        w1 = jnp.stack([cmp_k_w1[l], cmp_v_w1[l]]).astype(BF16)
        b1 = jnp.stack([cmp_k_b1[l], cmp_v_b1[l]]).reshape(2, 1, Dh)
        w2 = jnp.stack([cmp_k_w2[l], cmp_v_w2[l]]).astype(BF16)
        kvc = compress(proj_nsa, c_kc, pos2, w1, b1, w2)
        gates = small[:, :3 * NSA_HEADS].reshape(B, S, G, 3 * R).transpose(0, 2, 1, 3)
        slopes = jnp.exp2(-8.0 * jnp.arange(1, NSA_HEADS + 1, dtype=F32) / NSA_HEADS)
        o_nsa = nsa_attention(proj_nsa, kvc, gates, slopes, col_q=0, col_ks=c_ks, col_vs=c_vs,
                              col_kw=c_kw, col_vw=c_vw)

        xa = small[:, 3 * NSA_HEADS:n_small].reshape(B, S, GLA_GATE_RANK)
        o_gla = gla_attention(proj_gla, xa, gla_w_alpha[l], gla_b_alpha[l], gla_norm_g[l],
                              col_q=0, col_k=c_gk, col_v=c_gv, col_r=c_gr)

        mixed = gated_mix(o_nsa.reshape(T, -1), o_gla.reshape(T, -1), w_branch_nsa[l].astype(BF16),
                          w_branch_gla[l].astype(BF16), proj_mix, 0, D // LANES)
        h = out_proj(mixed, w_out[l].astype(BF16), h)

        n_r = N_GROUPS + N_EXPERTS
        w_r = jnp.concatenate([w_router_group[l], w_router_expert[l], jnp.zeros((D, LANES - n_r), F32)], axis=1)
        b_r = jnp.concatenate([b_router_group[l], b_router_expert[l], jnp.zeros((LANES - n_r,), F32)]).reshape(1, LANES)
        ids, wts, u = router(h, norm_ffn_g[l], w_r, b_r)
        dest, n_rows, tables = _dispatch_tables(ids[:, :2], MOE_TM)
        x_sorted = dispatch_rows(u, dest, n_rows)
        h_sorted = ffn_up(x_sorted, tables, w_exp_gate[l], w_exp_up[l])
        y_sorted = ffn_down(h_sorted, tables, w_exp_down[l])
        h = combine_norm(h, wts, dest, y_sorted, norm_final_g)
    return h.reshape(B, S, D)


def kernel(x, norm_mix_g, w_in, cmp_k_pos, cmp_k_w1, cmp_k_b1, cmp_k_w2, cmp_v_pos, cmp_v_w1, cmp_v_b1, cmp_v_w2, gla_w_alpha, gla_b_alpha, gla_norm_g, w_branch_nsa, w_branch_gla, w_out, norm_ffn_g, w_router_group, b_router_group, w_router_expert, b_router_expert, w_exp_gate, w_exp_up, w_exp_down, norm_final_g):
    return _forward(x, norm_mix_g, w_in, cmp_k_pos, cmp_k_w1, cmp_k_b1, cmp_k_w2, cmp_v_pos, cmp_v_w1, cmp_v_b1,
                    cmp_v_w2, gla_w_alpha, gla_b_alpha, gla_norm_g, w_branch_nsa, w_branch_gla, w_out, norm_ffn_g,
                    w_router_group, b_router_group, w_router_expert, b_router_expert, w_exp_gate, w_exp_up,
                    w_exp_down, norm_final_g)
```

```python
import functools

import jax
import jax.numpy as jnp
from jax import lax
from jax.experimental import pallas as pl
from jax.experimental.pallas import tpu as pltpu

F32 = jnp.float32
BF16 = jnp.bfloat16
I32 = jnp.int32

NSA_HEAD_DIM = 128
NSA_KV_HEADS = 4
NSA_GROUP = 4
NSA_HEADS = NSA_KV_HEADS * NSA_GROUP
CMP_BLOCK = 32
CMP_STRIDE = 16
SEL_BLOCK = 64
SEL_TOPN = 16
WINDOW = 512
BIG = 1e9
GLA_HEADS = 16
GLA_DK = 64
GLA_DV = 128
GLA_GATE_RANK = 16
GLA_TAU = 16.0
N_GROUPS = 4
EXPERTS_PER_GROUP = 8
N_EXPERTS = N_GROUPS * EXPERTS_PER_GROUP
NORM_EPS = 1e-6

LANES = 128
VMEM_LIMIT = 56 * 1024 * 1024
NEG = -1e30

NORM_ROWS = 512
PROJ_TM, PROJ_TN = 1024, 512
NSA_TQ, NSA_TK = 512, 512
GLA_TS, GLA_C, GLA_SUB = 512, 64, 16
MIX_TM, MIX_TN = 1024, 1024
OUT_TM, OUT_TN = 1024, 1024
ROUTE_TM = 256
MOE_TM = 512
MOE_CJ = 512
MOE_CN = 2048
DISPATCH_TOKENS = 512
COMBINE_ROWS = 256


def _cparams(*sem):
    return pltpu.CompilerParams(dimension_semantics=sem, vmem_limit_bytes=VMEM_LIMIT)


def _dot(a, b, **kw):
    return jnp.dot(a, b, preferred_element_type=F32, **kw)


def _dot_nt(a, b, **kw):
    return lax.dot_general(a, b, (((1,), (1,)), ((), ())), preferred_element_type=F32, **kw)


def _dot_tn(a, b, **kw):
    return lax.dot_general(a, b, (((0,), (0,)), ((), ())), preferred_element_type=F32, **kw)


def _sigmoid(x):
    return 1.0 / (1.0 + jnp.exp(-x))


def _masked_softmax(s, mask):
    sm = jnp.where(mask, s, NEG)
    m = jnp.max(sm, axis=-1, keepdims=True)
    e = jnp.where(mask, jnp.exp(sm - m), 0.0)
    return e * (1.0 / jnp.maximum(jnp.sum(e, axis=-1, keepdims=True), 1e-30))


def _rmsnorm_kernel(x_ref, g_ref, o_ref):
    x = x_ref[...]
    ms = jnp.mean(x * x, axis=-1, keepdims=True)
    o_ref[...] = (x * lax.rsqrt(ms + NORM_EPS) * g_ref[...]).astype(o_ref.dtype)


def rmsnorm(x, g, out_dtype=BF16, rows=NORM_ROWS):
    T, D = x.shape
    rows = min(rows, T)
    return pl.pallas_call(
        _rmsnorm_kernel,
        out_shape=jax.ShapeDtypeStruct((T, D), out_dtype),
        grid=(T // rows,),
        in_specs=[pl.BlockSpec((rows, D), lambda i: (i, 0)),
                  pl.BlockSpec((1, D), lambda i: (0, 0))],
        out_specs=pl.BlockSpec((rows, D), lambda i: (i, 0)),
        compiler_params=_cparams("parallel"),
        name="rmsnorm",
    )(x, g.reshape(1, D))


def _mm_kernel(a_ref, b_ref, o_ref):
    o_ref[...] = _dot(a_ref[...], b_ref[...]).astype(o_ref.dtype)


def matmul(a, b, out_dtype, tm, tn, name):
    M, K = a.shape
    N = b.shape[1]
    tm, tn = min(tm, M), min(tn, N)
    return pl.pallas_call(
        _mm_kernel,
        out_shape=jax.ShapeDtypeStruct((M, N), out_dtype),
        grid=(M // tm, N // tn),
        in_specs=[pl.BlockSpec((tm, K), lambda i, j: (i, 0)),
                  pl.BlockSpec((K, tn), lambda i, j: (0, j))],
        out_specs=pl.BlockSpec((tm, tn), lambda i, j: (i, j)),
        compiler_params=_cparams("parallel", "arbitrary"),
        name=name,
    )(a, b)


def _mm_wcast_kernel(a_ref, w_ref, o_ref, w_bf):
    @pl.when(pl.program_id(1) == 0)
    def _():
        w = w_ref[...].astype(BF16)
        w_bf[...] = w
        o_ref[...] = _dot(a_ref[...], w).astype(o_ref.dtype)

    @pl.when(pl.program_id(1) != 0)
    def _():
        o_ref[...] = _dot(a_ref[...], w_bf[...]).astype(o_ref.dtype)


def matmul_wcast(a, w, col0, n_cols, out_dtype, tm, tn, name):
    M, K = a.shape
    tm = min(tm, M)
    assert col0 % tn == 0 and n_cols % tn == 0 and M % tm == 0
    c0 = col0 // tn
    return pl.pallas_call(
        _mm_wcast_kernel,
        out_shape=jax.ShapeDtypeStruct((M, n_cols), out_dtype),
        grid=(n_cols // tn, M // tm),
        in_specs=[pl.BlockSpec((tm, K), lambda j, i: (i, 0)),
                  pl.BlockSpec((K, tn), lambda j, i: (0, c0 + j))],
        out_specs=pl.BlockSpec((tm, tn), lambda j, i: (i, j)),
        scratch_shapes=[pltpu.VMEM((K, tn), BF16)],
        compiler_params=_cparams("arbitrary", "arbitrary"),
        name=name,
    )(a, w)


def _compress_kernel(seq_ref, pos_ref, w1_ref, b1_ref, w2_ref, o_ref):
    seq = seq_ref[0]
    nc = seq.shape[0] // CMP_STRIDE
    x = seq.reshape(nc, CMP_STRIDE * seq.shape[1]).astype(F32)
    half = x.shape[1]
    pos = pos_ref[0]
    w1 = w1_ref[0]
    u0 = _dot((x + pos[0:1, :]).astype(BF16), w1[:half, :])
    u1 = _dot((x + pos[1:2, :]).astype(BF16), w1[half:, :])
    pre = u0 + pltpu.roll(u1, nc - 1, 0) + b1_ref[0]
    h = 0.5 * pre * (1.0 + jnp.tanh(0.7978845608028654 * (pre + 0.044715 * pre * pre * pre)))
    o_ref[0, 0, 0] = _dot(h.astype(BF16), w2_ref[0]).astype(o_ref.dtype)


def compress(proj, col_kc, pos2, w1, b1, w2):
    B, S, _ = proj.shape
    G = NSA_KV_HEADS
    Dh = w2.shape[-1]
    NC, HW = S // CMP_STRIDE, CMP_STRIDE * Dh
    return pl.pallas_call(
        _compress_kernel,
        out_shape=jax.ShapeDtypeStruct((2, B, G, NC, Dh), BF16),
        grid=(2, B, G),
        in_specs=[pl.BlockSpec((1, S, Dh), lambda a, b, g: (b, 0, col_kc + a * G + g)),
                  pl.BlockSpec((1, 2, HW), lambda a, b, g: (a, 0, 0)),
                  pl.BlockSpec((1, 2 * HW, Dh), lambda a, b, g: (a, 0, 0)),
                  pl.BlockSpec((1, 1, Dh), lambda a, b, g: (a, 0, 0)),
                  pl.BlockSpec((1, Dh, Dh), lambda a, b, g: (a, 0, 0))],
        out_specs=pl.BlockSpec((1, 1, 1, NC, Dh), lambda a, b, g: (a, b, g, 0, 0)),
        compiler_params=_cparams("parallel", "parallel", "parallel"),
        name="nsa_compress",
    )(proj, pos2, w1, b1, w2)


MASK_BIG = 2.0 ** 100


def _nsa_kernel(slopes_ref, q_ref, kc_ref, vc_ref, ks_ref, vs_ref, kw_ref, vw_ref, g_ref,
                paux_ref, saux_ref, caux_ref, mmat_ref, o_ref,
                qx_ref, sc_ref, sw_ref, sa_ref, sb_ref, score_ref, m_ref, acc_ref,
                *, TQ, TK, NC, NSEL, NTOP):
    R, Dh = NSA_GROUP, NSA_HEAD_DIM
    g = pl.program_id(1)
    q0 = pl.program_id(2) * TQ
    scale = Dh ** -0.5
    lane = lax.broadcasted_iota(I32, (1, LANES), 1)
    t1 = q0 + lax.broadcasted_iota(I32, (TQ, 1), 0)
    head = lambda r: slice(r * TQ, (r + 1) * TQ)

    def alibi_cols(r):
        hi, lo = slopes_ref[0, g * R + r], slopes_ref[1, g * R + r]
        c = jnp.where(lane == 0, 64.0 * hi, jnp.where(lane == 1, 64.0 * lo,
                                                      jnp.where(lane == 2, hi, jnp.where(lane == 3, lo, 0.0))))
        return jnp.broadcast_to(c, (TQ, LANES))

    for r in range(R):
        qx_ref[head(r), 0:Dh] = (q_ref[0, :, r * Dh:(r + 1) * Dh].astype(F32) * scale).astype(BF16)
        qx_ref[head(r), Dh:Dh + LANES] = alibi_cols(r).astype(BF16)

    def with_ones(v):
        return jnp.concatenate([v, jnp.ones(v.shape, v.dtype)], axis=1)

    def exp_pv(s, v1):
        m = jnp.broadcast_to(jnp.max(s, axis=-1, keepdims=True), (TQ, LANES))
        e = [jnp.exp(s[:, c * LANES:(c + 1) * LANES] - m) for c in range(s.shape[1] // LANES)]
        return e, _dot(jnp.concatenate(e, axis=1).astype(BF16), v1)

    WK = WINDOW + TQ
    ws = pl.multiple_of(jnp.maximum(q0 - WINDOW, 0), TQ)
    sc_ref[...] = _dot_nt(qx_ref[...], jnp.concatenate([kc_ref[0, 0, 0], caux_ref[...]], axis=1))
    sw_ref[...] = _dot_nt(qx_ref[...], jnp.concatenate([kw_ref[0, pl.ds(ws, WK), :], paux_ref[pl.ds(ws, WK), :]],
                                                       axis=1))

    cmp_end = lax.broadcasted_iota(I32, (1, NC), 1) * CMP_STRIDE + (CMP_BLOCK - 1)
    cmp_bias = jnp.where(cmp_end <= t1, 0.0, NEG)
    row_ok = q0 + lax.broadcasted_iota(I32, (TQ, LANES), 0) >= CMP_BLOCK - 1
    vc1 = with_ones(vc_ref[0, 0, 0])
    o_cmp = []
    imp = None
    for r in range(R):
        e, o2 = exp_pv(sc_ref[head(r), :] + cmp_bias, vc1)
        inv = jnp.where(row_ok, 1.0 / jnp.maximum(o2[:, Dh:], 1e-30), 0.0)
        o_cmp.append(o2[:, :Dh] * inv)
        p = jnp.concatenate([ec * inv for ec in e], axis=1)
        imp = p if imp is None else imp + p

    vw1 = with_ones(vw_ref[0, pl.ds(ws, WK), :])
    dw = t1 - (ws + lax.broadcasted_iota(I32, (1, WK), 1))
    win_bias = jnp.where((dw >= 0) & (dw < WINDOW), 0.0, NEG)
    o_win = []
    for r in range(R):
        _, o2 = exp_pv(sw_ref[head(r), :] + win_bias, vw1)
        o_win.append(o2[:, :Dh] * (1.0 / jnp.maximum(o2[:, Dh:], 1e-30)))

    mm = mmat_ref[...]
    i1 = imp.astype(BF16)
    rem = imp - i1.astype(F32)
    i2 = rem.astype(BF16)
    i3 = (rem - i2.astype(F32)).astype(BF16)
    imp_sel = _dot_nt(mm, i1) + _dot_nt(mm, i2) + _dot_nt(mm, i3)

    cur = (q0 + lax.broadcasted_iota(I32, (1, TQ), 1)) // SEL_BLOCK
    blk = lax.broadcasted_iota(I32, (NSEL, 1), 0)
    forced = (blk == 0) | (blk == cur) | (blk == cur - 1)
    score = jnp.where(blk <= cur, jnp.where(forced, BIG, imp_sel), -BIG)
    score_ref[...] = score

    def rank_pair(i2, cnt):
        for d in range(2):
            ii = 2 * i2 + d
            other = score_ref[pl.ds(ii, 1), :]
            beats = (other > score) | ((other == score) & (blk > ii))
            cnt = cnt + jnp.where(beats, 1.0, 0.0)
        return cnt

    n_blk = (q0 + TQ) // SEL_BLOCK
    cnt = lax.fori_loop(0, n_blk // 2, rank_pair, jnp.zeros((NSEL, TQ), F32))
    unsel = jnp.where(cnt < NTOP, 0.0, -1.0).astype(BF16)
    place = jnp.where(lax.broadcasted_iota(I32, (NSEL, LANES), 1)
                      == lax.broadcasted_iota(I32, (NSEL, LANES), 0) + LANES // 2, 1.0, 0.0).astype(BF16)
    unsel_l = _dot_tn(unsel, place)
    for r in range(R):
        qx_ref[head(r), Dh:Dh + LANES] = (alibi_cols(r) + unsel_l).astype(BF16)

    m_ref[...] = jnp.full(m_ref.shape, NEG, F32)
    acc_ref[...] = jnp.zeros(acc_ref.shape, F32)
    pos_in_tile = lax.broadcasted_iota(I32, (1, TK), 1)

    def issue_scores(kt, s_ref):
        k0 = pl.multiple_of(kt * TK, TK)
        kk = jnp.concatenate([ks_ref[0, pl.ds(k0, TK), :], saux_ref[pl.ds(k0, TK), :]], axis=1)
        s_ref[...] = _dot_nt(qx_ref[...], kk)

    def consume_scores(kt, s_ref, causal):
        k0 = pl.multiple_of(kt * TK, TK)
        vv1 = with_ones(vs_ref[0, pl.ds(k0, TK), :])
        if causal:
            causal_bias = jnp.where(t1 >= k0 + pos_in_tile, 0.0, -MASK_BIG)
        for r in range(R):
            sc = s_ref[head(r), :]
            if causal:
                sc = sc + causal_bias
            m_old = m_ref[r]
            m_new = jnp.maximum(m_old, jnp.broadcast_to(jnp.max(sc, axis=-1, keepdims=True), (TQ, LANES)))
            alpha = jnp.exp(m_old - m_new)
            p = jnp.concatenate([jnp.exp(sc[:, c * LANES:(c + 1) * LANES] - m_new) for c in range(TK // LANES)],
                                axis=1).astype(BF16)
            m_ref[r] = m_new
            acc_ref[r] = jnp.concatenate([alpha, alpha], axis=1) * acc_ref[r] + _dot(p, vv1)

    def sel_pair(k, carry):
        issue_scores(2 * k + 1, sb_ref)
        consume_scores(2 * k, sa_ref, False)
        issue_scores(2 * k + 2, sa_ref)
        consume_scores(2 * k + 1, sb_ref, False)
        return carry

    n_below = q0 // TK
    issue_scores(0, sa_ref)
    lax.fori_loop(0, n_below // 2, sel_pair, 0)

    @pl.when(n_below % 2 == 1)
    def _():
        issue_scores(n_below, sb_ref)
        consume_scores(n_below - 1, sa_ref, False)
        consume_scores(n_below, sb_ref, True)

    @pl.when(n_below % 2 == 0)
    def _():
        consume_scores(n_below, sa_ref, True)

    gt = _sigmoid(g_ref[0, 0])
    for r in range(R):
        o_sel = acc_ref[r, :, 0:Dh] * (1.0 / jnp.maximum(acc_ref[r, :, Dh:2 * Dh], 1e-30))
        o = (gt[:, 3 * r:3 * r + 1] * o_cmp[r] + gt[:, 3 * r + 1:3 * r + 2] * o_sel
             + gt[:, 3 * r + 2:3 * r + 3] * o_win[r])
        o_ref[0, :, r * Dh:(r + 1) * Dh] = o.astype(o_ref.dtype)


def _nsa_constants(S, NC, NSEL):
    def pos_cols(pos):
        col = jnp.arange(LANES)[None, :]
        hi, lo = (pos // 64)[:, None], (pos % 64)[:, None]
        return jnp.where(col < 2, hi, jnp.where(col < 4, lo, 0)).astype(F32)

    pos = jnp.arange(S)
    paux = pos_cols(pos)
    onehot = (jnp.arange(LANES)[None, :] - LANES // 2 == (pos // SEL_BLOCK)[:, None]) & (jnp.arange(LANES)[None, :] >= LANES // 2)
    saux = paux + jnp.where(onehot, MASK_BIG, 0.0)
    caux = pos_cols(jnp.arange(NC) * CMP_STRIDE + (CMP_BLOCK - 1))
    d = jnp.arange(NC)[None, :] - 4 * jnp.arange(NSEL)[:, None]
    mmat = jnp.where((d == -1) | (d == 3), 1.0, jnp.where((d >= 0) & (d <= 2), 2.0, 0.0))
    return paux.astype(BF16), saux.astype(BF16), caux.astype(BF16), mmat.astype(BF16)


def nsa_attention(proj, kvc, gates, slopes, *, col_q, col_ks, col_vs, col_kw, col_vw):
    B, S, _ = proj.shape
    G, R, Dh = NSA_KV_HEADS, NSA_GROUP, NSA_HEAD_DIM
    NC = kvc.shape[3]
    NSEL = S // SEL_BLOCK
    TQ, TK = min(NSA_TQ, S), min(NSA_TK, S)
    assert S % TQ == 0 and TK % TQ == 0 and S % TK == 0 and S >= WINDOW + TQ and NC == S // CMP_STRIDE
    assert NSEL <= LANES // 2 and S // 64 <= 256 and TQ % (2 * SEL_BLOCK) == 0
    s_hi = slopes.astype(BF16).astype(F32)
    s_lo = (slopes - s_hi).astype(BF16).astype(F32)
    paux, saux, caux, mmat = _nsa_constants(S, NC, NSEL)
    kern = functools.partial(_nsa_kernel, TQ=TQ, TK=TK, NC=NC, NSEL=NSEL, NTOP=min(SEL_TOPN, NSEL))
    kv_spec = lambda col: pl.BlockSpec((1, S, Dh), lambda b, g, i: (b, 0, col + g))
    whole = lambda a: pl.BlockSpec(a.shape, lambda b, g, i: (0, 0))
    return pl.pallas_call(
        kern,
        out_shape=jax.ShapeDtypeStruct((B, S, G * R * Dh), BF16),
        grid=(B, G, S // TQ),
        in_specs=[pl.BlockSpec(memory_space=pltpu.SMEM),
                  pl.BlockSpec((1, TQ, R * Dh), lambda b, g, i: (b, i, col_q // R + g)),
                  pl.BlockSpec((1, 1, 1, NC, Dh), lambda b, g, i: (0, b, g, 0, 0)),
                  pl.BlockSpec((1, 1, 1, NC, Dh), lambda b, g, i: (1, b, g, 0, 0)),
                  kv_spec(col_ks), kv_spec(col_vs), kv_spec(col_kw), kv_spec(col_vw),
                  pl.BlockSpec((1, 1, TQ, 3 * R), lambda b, g, i: (b, g, i, 0)),
                  whole(paux), whole(saux), whole(caux), whole(mmat)],
        out_specs=pl.BlockSpec((1, TQ, R * Dh), lambda b, g, i: (b, i, g)),
        scratch_shapes=[pltpu.VMEM((R * TQ, Dh + LANES), BF16), pltpu.VMEM((R * TQ, NC), F32),
                        pltpu.VMEM((R * TQ, WINDOW + TQ), F32), pltpu.VMEM((R * TQ, TK), F32),
                        pltpu.VMEM((R * TQ, TK), F32),
                        pltpu.VMEM((NSEL, TQ), F32), pltpu.VMEM((R, TQ, LANES), F32),
                        pltpu.VMEM((R, TQ, 2 * Dh), F32)],
        compiler_params=_cparams("parallel", "parallel", "arbitrary"),
        name="nsa_attention",
    )(jnp.stack([s_hi, s_lo]), proj, kvc, kvc, proj, proj, proj, proj, gates, paux, saux, caux, mmat)


def _gla_kernel(q_ref, k_ref, v_ref, r_ref, xa_ref, wa_ref, ba_ref, ng_ref, tri_ref, o_ref,
                st_ref, b_ref, q2_ref, kh_ref, qe_ref, kd_ref, oacc_ref, *, TS, C):
    DK2, DV2 = 2 * GLA_DK, 2 * GLA_DV
    NCH = TS // C
    LEVELS = [C >> (i + 1) for i in range(C.bit_length() - 1)]

    @pl.when(pl.program_id(2) == 0)
    def _():
        st_ref[...] = jnp.zeros_like(st_ref)

    z = _dot(xa_ref[0], wa_ref[...], precision=lax.Precision.HIGHEST) + ba_ref[...]
    la = (jnp.minimum(z, 0.0) - jnp.log(1.0 + jnp.exp(-jnp.abs(z)))) * (1.0 / GLA_TAU)
    l1 = la.astype(BF16)
    lr = la - l1.astype(F32)
    l2 = lr.astype(BF16)
    l3 = (lr - l2.astype(F32)).astype(BF16)
    tri = tri_ref[...]
    b = _dot(tri, l1) + _dot(tri, l2) + _dot(tri, l3)
    b_ref[...] = b

    head_a = lax.broadcasted_iota(I32, (1, DK2), 1) < GLA_DK
    row = lax.broadcasted_iota(I32, (TS, 1), 0)
    q = q_ref[0].astype(F32) * (GLA_DK ** -0.5)
    k = k_ref[0].astype(F32)

    def ref_rows(h):
        if h >= 4:
            g = TS // (2 * h)
            return jnp.broadcast_to(b.reshape(g, 2 * h, DK2)[:, h - 1:h, :], (g, 2 * h, DK2)).reshape(TS, DK2)
        dn1 = pltpu.roll(b, 1, 0)
        if h == 1:
            return jnp.where(row % 2 == 0, b, dn1)
        m4 = row % 4
        return jnp.where(m4 == 0, pltpu.roll(b, TS - 1, 0),
                         jnp.where(m4 == 1, b, jnp.where(m4 == 2, dn1, pltpu.roll(b, 2, 0))))

    def put(lvl, qh, kh):
        q2_ref[lvl, :, 0:C, :] = jnp.where(head_a, qh, 0.0).astype(BF16).reshape(NCH, C, DK2)
        q2_ref[lvl, :, C:2 * C, :] = jnp.where(head_a, 0.0, qh).astype(BF16).reshape(NCH, C, DK2)
        kh_ref[lvl] = kh.astype(BF16)

    put(0, q, k)
    for lvl, h in enumerate(LEVELS, start=1):
        r = ref_rows(h)
        put(lvl, q * jnp.exp(jnp.minimum(b - r, 0.0)), k * jnp.exp(jnp.minimum(r - b, 0.0)))
    b_last = jnp.broadcast_to(b.reshape(NCH, C, DK2)[:, C - 1:C, :], (NCH, C, DK2)).reshape(TS, DK2)
    qe_ref[...] = (q * jnp.exp(b)).astype(BF16)
    kd_ref[...] = (k * jnp.exp(b_last - b)).astype(BF16)

    row2 = lax.broadcasted_iota(I32, (2 * C, 1), 0) % C
    col = lax.broadcasted_iota(I32, (1, C), 1)
    masks = [row2 == col] + [(row2 // (2 * h) == col // (2 * h)) & (row2 % (2 * h) >= h) & (col % (2 * h) < h)
                             for h in LEVELS]
    vk_same = (lax.broadcasted_iota(I32, (DV2, DK2), 0) < GLA_DV) == (lax.broadcasted_iota(I32, (DV2, DK2), 1) < GLA_DK)
    for c in range(NCH):
        rows = slice(c * C, (c + 1) * C)
        vb = v_ref[0, rows, :]
        a2 = jnp.zeros((2 * C, C), F32)
        for lvl, mask in enumerate(masks):
            a2 = a2 + jnp.where(mask, _dot_nt(q2_ref[lvl, c], kh_ref[lvl, rows, :]), 0.0)
        a2 = a2.astype(BF16)
        o = jnp.concatenate([_dot(a2[0:C], vb[:, :GLA_DV]), _dot(a2[C:2 * C], vb[:, GLA_DV:])], axis=1)
        st = st_ref[...]
        oacc_ref[rows, :] = o + _dot_nt(qe_ref[rows, :], st.astype(BF16))
        ds = jnp.where(vk_same, _dot_tn(vb, kd_ref[rows, :]), 0.0)
        st_ref[...] = st * jnp.exp(b_ref[(c + 1) * C - 1:(c + 1) * C, :]) + ds

    def ln(x):
        mu = jnp.mean(x, axis=-1, keepdims=True)
        xc = x - mu
        return xc * lax.rsqrt(jnp.mean(xc * xc, axis=-1, keepdims=True) + NORM_EPS)

    o = oacc_ref[...]
    on = jnp.concatenate([ln(o[:, :GLA_DV]), ln(o[:, GLA_DV:])], axis=1) * ng_ref[...]
    rr = r_ref[0].astype(F32)
    o_ref[0] = (on * (rr * _sigmoid(rr))).astype(o_ref.dtype)


def gla_attention(proj, xa, w_alpha, b_alpha, norm_g, *, col_q, col_k, col_v, col_r):
    B, S, _ = proj.shape
    HP = GLA_HEADS // 2
    DK2, DV2 = 2 * GLA_DK, 2 * GLA_DV
    TS, C = min(GLA_TS, S), GLA_C
    assert S % TS == 0 and TS % C == 0 and C & (C - 1) == 0 and C >= 16
    n_lvl = C.bit_length()
    idx = jnp.arange(TS)
    tri = ((idx[None, :] <= idx[:, None]) & (idx[None, :] // C == idx[:, None] // C)).astype(BF16)
    kern = functools.partial(_gla_kernel, TS=TS, C=C)
    return pl.pallas_call(
        kern,
        out_shape=jax.ShapeDtypeStruct((B, S, GLA_HEADS * GLA_DV), BF16),
        grid=(B, HP, S // TS),
        in_specs=[pl.BlockSpec((1, TS, DK2), lambda b, p, s: (b, s, col_q + p)),
                  pl.BlockSpec((1, TS, DK2), lambda b, p, s: (b, s, col_k + p)),
                  pl.BlockSpec((1, TS, DV2), lambda b, p, s: (b, s, col_v // 2 + p)),
                  pl.BlockSpec((1, TS, DV2), lambda b, p, s: (b, s, col_r // 2 + p)),
                  pl.BlockSpec((1, TS, GLA_GATE_RANK), lambda b, p, s: (b, s, 0)),
                  pl.BlockSpec((GLA_GATE_RANK, DK2), lambda b, p, s: (0, p)),
                  pl.BlockSpec((1, DK2), lambda b, p, s: (0, p)),
                  pl.BlockSpec((1, DV2), lambda b, p, s: (0, p)),
                  pl.BlockSpec((TS, TS), lambda b, p, s: (0, 0))],
        out_specs=pl.BlockSpec((1, TS, DV2), lambda b, p, s: (b, s, p)),
        scratch_shapes=[pltpu.VMEM((DV2, DK2), F32), pltpu.VMEM((TS, DK2), F32),
                        pltpu.VMEM((n_lvl, TS // C, 2 * C, DK2), BF16), pltpu.VMEM((n_lvl, TS, DK2), BF16),
                        pltpu.VMEM((TS, DK2), BF16), pltpu.VMEM((TS, DK2), BF16), pltpu.VMEM((TS, DV2), F32)],
        compiler_params=_cparams("parallel", "parallel", "arbitrary"),
        name="gla_attention",
    )(proj, proj, proj, proj, xa, w_alpha, b_alpha.reshape(1, -1), norm_g.reshape(1, -1), tri)


def _mix_kernel(oa_ref, ob_ref, wa_ref, wb_ref, ma_ref, mb_ref, o_ref):
    ya = _dot(oa_ref[...], wa_ref[...])
    yb = _dot(ob_ref[...], wb_ref[...])
    o_ref[...] = (_sigmoid(ma_ref[...].astype(F32)) * ya + _sigmoid(mb_ref[...].astype(F32)) * yb).astype(o_ref.dtype)


def gated_mix(o_nsa, o_gla, wa, wb, proj2d, col_ma, col_mb):
    T, KA = o_nsa.shape
    KB = o_gla.shape[1]
    N = wa.shape[1]
    tm, tn = min(MIX_TM, T), min(MIX_TN, N)
    ca, cb = col_ma * LANES // tn, col_mb * LANES // tn
    assert (col_ma * LANES) % tn == 0 and (col_mb * LANES) % tn == 0
    return pl.pallas_call(
        _mix_kernel,
        out_shape=jax.ShapeDtypeStruct((T, N), BF16),
        grid=(T // tm, N // tn),
        in_specs=[pl.BlockSpec((tm, KA), lambda i, j: (i, 0)),
                  pl.BlockSpec((tm, KB), lambda i, j: (i, 0)),
                  pl.BlockSpec((KA, tn), lambda i, j: (0, j)),
                  pl.BlockSpec((KB, tn), lambda i, j: (0, j)),
                  pl.BlockSpec((tm, tn), lambda i, j: (i, ca + j)),
                  pl.BlockSpec((tm, tn), lambda i, j: (i, cb + j))],
        out_specs=pl.BlockSpec((tm, tn), lambda i, j: (i, j)),
        compiler_params=_cparams("parallel", "arbitrary"),
        name="gated_mix",
    )(o_nsa, o_gla, wa, wb, proj2d, proj2d)


def _out_kernel(a_ref, w_ref, x_ref, o_ref):
    o_ref[...] = x_ref[...] + _dot(a_ref[...], w_ref[...])


def out_proj(mixed, w, x):
    T, K = mixed.shape
    N = w.shape[1]
    tm, tn = min(OUT_TM, T), min(OUT_TN, N)
    return pl.pallas_call(
        _out_kernel,
        out_shape=jax.ShapeDtypeStruct((T, N), F32),
        grid=(T // tm, N // tn),
        in_specs=[pl.BlockSpec((tm, K), lambda i, j: (i, 0)),
                  pl.BlockSpec((K, tn), lambda i, j: (0, j)),
                  pl.BlockSpec((tm, tn), lambda i, j: (i, j))],
        out_specs=pl.BlockSpec((tm, tn), lambda i, j: (i, j)),
        compiler_params=_cparams("parallel", "arbitrary"),
        name="out_proj",
    )(mixed, w, x)


def _router_kernel(h_ref, g_ref, w_ref, b_ref, id_ref, wt_ref, u_ref):
    x = h_ref[...]
    u = x * lax.rsqrt(jnp.mean(x * x, axis=-1, keepdims=True) + NORM_EPS) * g_ref[...]
    u_hi = u.astype(BF16)
    u_ref[...] = u_hi.reshape(u_ref.shape)
    u_lo = (u - u_hi.astype(F32)).astype(BF16)
    logit = _dot(u_hi, w_ref[0]) + _dot(u_lo, w_ref[0]) + _dot(u_hi, w_ref[1]) + b_ref[...]
    lane = lax.broadcasted_iota(I32, logit.shape, 1)
    big = jnp.int32(1 << 20)
    gmask = lane < N_GROUPS
    gl = jnp.where(gmask, logit, NEG)
    gmax = jnp.max(gl, axis=-1, keepdims=True)
    g_star = jnp.min(jnp.where(gmask & (gl == gmax), lane, big), axis=-1, keepdims=True)
    p_group = 1.0 / jnp.sum(jnp.where(gmask, jnp.exp(gl - gmax), 0.0), axis=-1, keepdims=True)
    e_lo = N_GROUPS + EXPERTS_PER_GROUP * g_star
    emask = (lane >= e_lo) & (lane < e_lo + EXPERTS_PER_GROUP)
    el = jnp.where(emask, logit, NEG)
    m1 = jnp.max(el, axis=-1, keepdims=True)
    i1 = jnp.min(jnp.where(emask & (el == m1), lane, big), axis=-1, keepdims=True)
    emask2 = emask & (lane != i1)
    el2 = jnp.where(emask2, logit, NEG)
    m2 = jnp.max(el2, axis=-1, keepdims=True)
    i2 = jnp.min(jnp.where(emask2 & (el2 == m2), lane, big), axis=-1, keepdims=True)
    e2 = jnp.exp(m2 - m1)
    w1 = p_group / (1.0 + e2)
    w2 = p_group * e2 / (1.0 + e2)
    id_ref[...] = jnp.where(lane == 0, i1 - N_GROUPS, jnp.where(lane == 1, i2 - N_GROUPS, 0))
    wt_ref[...] = jnp.where(lane == 0, w1, jnp.where(lane == 1, w2, 0.0))


def router(h, g, w_r, b_r):
    T, D = h.shape
    tm = min(ROUTE_TM, T)
    w_hi = w_r.astype(BF16)
    return pl.pallas_call(
        _router_kernel,
        out_shape=(jax.ShapeDtypeStruct((T, LANES), I32), jax.ShapeDtypeStruct((T, LANES), F32),
                   jax.ShapeDtypeStruct((T, D // LANES, LANES), BF16)),
        grid=(T // tm,),
        in_specs=[pl.BlockSpec((tm, D), lambda i: (i, 0)),
                  pl.BlockSpec((1, D), lambda i: (0, 0)),
                  pl.BlockSpec((2, D, LANES), lambda i: (0, 0, 0)),
                  pl.BlockSpec((1, LANES), lambda i: (0, 0))],
        out_specs=(pl.BlockSpec((tm, LANES), lambda i: (i, 0)), pl.BlockSpec((tm, LANES), lambda i: (i, 0)),
                   pl.BlockSpec((tm, D // LANES, LANES), lambda i: (i, 0, 0))),
        compiler_params=_cparams("parallel"),
        name="moe_router",
    )(h, g.reshape(1, D), jnp.stack([w_hi, (w_r - w_hi.astype(F32)).astype(BF16)]), b_r)


def _row_copy(src_ref, dst_ref, sem, src_row, dst_row):
    return pltpu.make_async_copy(src_ref.at[pl.ds(src_row, 1)], dst_ref.at[pl.ds(dst_row, 1)], sem)


def _dispatch_rows_kernel(dest_ref, u_ref, init_hbm, out_hbm, sem, *, toks):
    def start(t, c):
        for k in range(2):
            _row_copy(u_ref, out_hbm, sem, t, dest_ref[0, 0, 2 * t + k]).start(priority=k)
        return c

    def wait(t, c):
        for k in range(2):
            _row_copy(u_ref, out_hbm, sem, t, 0).wait()
        return c

    lax.fori_loop(0, toks, start, 0)
    lax.fori_loop(0, toks, wait, 0)


def dispatch_rows(u3, dest, n_rows):
    T = u3.shape[0]
    toks = min(DISPATCH_TOKENS, T)
    assert T % toks == 0
    out_shape = jax.ShapeDtypeStruct((n_rows,) + u3.shape[1:], u3.dtype)
    return pl.pallas_call(
        functools.partial(_dispatch_rows_kernel, toks=toks),
        out_shape=out_shape,
        grid=(T // toks,),
        in_specs=[pl.BlockSpec((1, 1, 2 * toks), lambda i: (i, 0, 0), memory_space=pltpu.SMEM),
                  pl.BlockSpec((toks,) + u3.shape[1:], lambda i: (i, 0, 0)),
                  pl.BlockSpec(memory_space=pl.ANY)],
        out_specs=pl.BlockSpec(memory_space=pl.ANY),
        scratch_shapes=[pltpu.SemaphoreType.DMA],
        input_output_aliases={2: 0},
        compiler_params=_cparams("arbitrary"),
        name="moe_dispatch_rows",
    )(dest.reshape(T // toks, 1, 2 * toks), u3, jnp.zeros(out_shape.shape, out_shape.dtype))


def _ffn_up_kernel(te_ref, new_ref, nv_ref, half_ref, x_ref, wg_ref, wu_ref, o_ref, wg_bf, wu_bf):
    i = pl.program_id(1)
    used = i < nv_ref[0]
    fresh = new_ref[i] == 1
    tm = x_ref.shape[0]

    def swiglu(n):
        x = x_ref[0:n].reshape(n, wg_bf.shape[0])
        a = _dot(x, wg_bf[...])
        o_ref[0:n, :] = (a * _sigmoid(a) * _dot(x, wu_bf[...])).astype(o_ref.dtype)
        if n < tm:
            o_ref[n:tm, :] = jnp.zeros((tm - n, o_ref.shape[1]), o_ref.dtype)

    @pl.when(fresh)
    def _():
        wg_bf[...] = wg_ref[0].astype(BF16)
        wu_bf[...] = wu_ref[0].astype(BF16)
        swiglu(tm)

    later = jnp.logical_and(used, jnp.logical_not(fresh))
    pl.when(jnp.logical_and(later, half_ref[i] == 1))(lambda: swiglu(tm // 2))
    pl.when(jnp.logical_and(later, half_ref[i] == 0))(lambda: swiglu(tm))

    @pl.when(jnp.logical_not(used))
    def _():
        o_ref[...] = jnp.zeros_like(o_ref)


def ffn_up(x_sorted, tables, w_gate, w_up):
    NP = x_sorted.shape[0]
    E, D, DE = w_gate.shape
    tm, cj = MOE_TM, min(MOE_CJ, DE)
    return pl.pallas_call(
        _ffn_up_kernel,
        out_shape=jax.ShapeDtypeStruct((NP, DE), BF16),
        grid_spec=pltpu.PrefetchScalarGridSpec(
            num_scalar_prefetch=4,
            grid=(DE // cj, NP // tm),
            in_specs=[pl.BlockSpec((tm,) + x_sorted.shape[1:], lambda j, i, te, nw, nv, hf: (i, 0, 0)),
                      pl.BlockSpec((1, D, cj), lambda j, i, te, nw, nv, hf: (te[i], 0, j)),
                      pl.BlockSpec((1, D, cj), lambda j, i, te, nw, nv, hf: (te[i], 0, j))],
            out_specs=pl.BlockSpec((tm, cj), lambda j, i, te, nw, nv, hf: (i, j)),
            scratch_shapes=[pltpu.VMEM((D, cj), BF16), pltpu.VMEM((D, cj), BF16)]),
        compiler_params=_cparams("arbitrary", "arbitrary"),
        name="moe_ffn_up",
    )(*tables, x_sorted, w_gate, w_up)


def _ffn_down_kernel(te_ref, new_ref, nv_ref, half_ref, h_ref, wd_ref, o_ref, wd_bf):
    i = pl.program_id(1)
    used = i < nv_ref[0]
    fresh = new_ref[i] == 1
    tm = h_ref.shape[0]

    def down(n):
        o_ref[0:n] = _dot(h_ref[0:n, :], wd_bf[...]).astype(o_ref.dtype).reshape((n,) + o_ref.shape[1:])
        if n < tm:
            o_ref[n:tm] = jnp.zeros((tm - n,) + o_ref.shape[1:], o_ref.dtype)

    @pl.when(fresh)
    def _():
        wd_bf[...] = wd_ref[0].astype(BF16)
        down(tm)

    later = jnp.logical_and(used, jnp.logical_not(fresh))
    pl.when(jnp.logical_and(later, half_ref[i] == 1))(lambda: down(tm // 2))
    pl.when(jnp.logical_and(later, half_ref[i] == 0))(lambda: down(tm))

    @pl.when(jnp.logical_not(used))
    def _():
        o_ref[...] = jnp.zeros_like(o_ref)


def ffn_down(h_sorted, tables, w_down):
    NP, DE = h_sorted.shape
    E, _, D = w_down.shape
    tm, cn = MOE_TM, min(MOE_CN, D)
    return pl.pallas_call(
        _ffn_down_kernel,
        out_shape=jax.ShapeDtypeStruct((NP, D // LANES, LANES), BF16),
        grid_spec=pltpu.PrefetchScalarGridSpec(
            num_scalar_prefetch=4,
            grid=(D // cn, NP // tm),
            in_specs=[pl.BlockSpec((tm, DE), lambda j, i, te, nw, nv, hf: (i, 0)),
                      pl.BlockSpec((1, DE, cn), lambda j, i, te, nw, nv, hf: (te[i], 0, j))],
            out_specs=pl.BlockSpec((tm, cn // LANES, LANES), lambda j, i, te, nw, nv, hf: (i, j, 0)),
            scratch_shapes=[pltpu.VMEM((DE, cn), BF16)]),
        compiler_params=_cparams("arbitrary", "arbitrary"),
        name="moe_ffn_down",
    )(*tables, h_sorted, w_down)


def _combine_kernel(idx_ref, nxt_ref, h_ref, wt_ref, y_hbm, g_ref, o_ref, buf, sem, *, rows):
    i = pl.program_id(0)
    slot = i % 2

    def issue(table, s):
        def body(r, c):
            _row_copy(y_hbm, buf.at[s, 0], sem.at[s], table[0, 0, 2 * r], r).start(priority=0)
            _row_copy(y_hbm, buf.at[s, 1], sem.at[s], table[0, 0, 2 * r + 1], r).start(priority=1)
            return c
        lax.fori_loop(0, rows, body, 0)

    @pl.when(i == 0)
    def _():
        issue(idx_ref, 0)

    @pl.when(i + 1 < pl.num_programs(0))
    def _():
        issue(nxt_ref, 1 - slot)

    def wait(r, c):
        _row_copy(y_hbm, buf.at[slot, 0], sem.at[slot], 0, r).wait()
        _row_copy(y_hbm, buf.at[slot, 1], sem.at[slot], 0, r).wait()
        return c

    lax.fori_loop(0, rows, wait, 0)
    wt = wt_ref[...]
    y0 = buf[slot, 0].reshape(h_ref.shape).astype(F32)
    y1 = buf[slot, 1].reshape(h_ref.shape).astype(F32)
    x = h_ref[...] + wt[:, 0:1] * y0 + wt[:, 1:2] * y1
    o_ref[...] = x * lax.rsqrt(jnp.mean(x * x, axis=-1, keepdims=True) + NORM_EPS) * g_ref[...]


def combine_norm(h, wts, dest, y3, g):
    T, D = h.shape
    rows = min(COMBINE_ROWS, T)
    n = T // rows
    table = dest.reshape(n, 1, 2 * rows)
    return pl.pallas_call(
        functools.partial(_combine_kernel, rows=rows),
        out_shape=jax.ShapeDtypeStruct((T, D), F32),
        grid=(n,),
        in_specs=[pl.BlockSpec((1, 1, 2 * rows), lambda i: (i, 0, 0), memory_space=pltpu.SMEM),
                  pl.BlockSpec((1, 1, 2 * rows), lambda i: (jnp.minimum(i + 1, n - 1), 0, 0),
                               memory_space=pltpu.SMEM),
                  pl.BlockSpec((rows, D), lambda i: (i, 0)),
                  pl.BlockSpec((rows, LANES), lambda i: (i, 0)),
                  pl.BlockSpec(memory_space=pl.ANY),
                  pl.BlockSpec((1, D), lambda i: (0, 0))],
        out_specs=pl.BlockSpec((rows, D), lambda i: (i, 0)),
        scratch_shapes=[pltpu.VMEM((2, 2, rows) + y3.shape[1:], y3.dtype), pltpu.SemaphoreType.DMA((2,))],
        compiler_params=_cparams("arbitrary"),
        name="moe_combine_norm",
    )(table, table, h, wts, y3, g.reshape(1, D))


def _dispatch_tables(ids, tm):
    T = ids.shape[0]
    E = N_EXPERTS
    eid = ids.reshape(-1)
    onehot = (eid[:, None] == jnp.arange(E, dtype=I32)[None, :]).astype(I32)
    csum = jnp.cumsum(onehot, axis=0)
    rank = jnp.sum(csum * onehot, axis=1) - 1
    counts = csum[-1]
    padded = ((counts + tm - 1) // tm) * tm
    ends = jnp.cumsum(padded)
    dest = jnp.sum(onehot * (ends - padded)[None, :], axis=1) + rank
    n_rows = 2 * T + E * tm
    tile_start = jnp.arange(n_rows // tm, dtype=I32) * tm
    tile_expert = jnp.sum((ends[None, :] <= tile_start[:, None]).astype(I32), axis=1)
    last_used = jnp.max(jnp.where(counts > 0, jnp.arange(E, dtype=I32), 0))
    tile_expert = jnp.minimum(tile_expert, last_used)
    tile_new = jnp.concatenate([jnp.ones((1,), I32), (tile_expert[1:] != tile_expert[:-1]).astype(I32)])
    n_tiles_used = (ends[-1] // tm).reshape(1)
    rows_end = jnp.sum(jnp.where(jnp.arange(E, dtype=I32)[None, :] == tile_expert[:, None],
                                 (ends - padded + counts)[None, :], 0), axis=1)
    tile_half = (rows_end - tile_start <= tm // 2).astype(I32)
    return dest.astype(I32), n_rows, (tile_expert, tile_new, n_tiles_used.astype(I32), tile_half)


def _forward(x, norm_mix_g, w_in, cmp_k_pos, cmp_k_w1, cmp_k_b1, cmp_k_w2, cmp_v_pos, cmp_v_w1, cmp_v_b1, cmp_v_w2,
             gla_w_alpha, gla_b_alpha, gla_norm_g, w_branch_nsa, w_branch_gla, w_out, norm_ffn_g,
             w_router_group, b_router_group, w_router_expert, b_router_expert, w_exp_gate, w_exp_up, w_exp_down,
             norm_final_g):
    B, S, D = x.shape
    T = B * S
    G, R, Dh = NSA_KV_HEADS, NSA_GROUP, NSA_HEAD_DIM
    NSA_Q, NSA_KV = NSA_HEADS * Dh, G * Dh
    GQK, GV = GLA_HEADS * GLA_DK, GLA_HEADS * GLA_DV
    h = x.reshape(T, D)
    assert w_in.shape[0] == 1, "the final norm is fused into the (single) layer's combine step"
    for l in range(1):
        w = w_in[l]
        o_ng = NSA_Q + 6 * NSA_KV
        o_gq = o_ng + 3 * NSA_HEADS
        o_ga = o_gq + 2 * GQK + 2 * GV
        o_ma = o_ga + GLA_GATE_RANK
        n_small = 3 * NSA_HEADS + GLA_GATE_RANK
        w_small = jnp.concatenate([w[:, o_ng:o_gq], w[:, o_ga:o_ma], jnp.zeros((D, LANES - n_small), F32)],
                                  axis=1).astype(BF16)
        c_kc = NSA_Q // LANES
        c_ks, c_vs, c_kw, c_vw = c_kc + 2 * G, c_kc + 3 * G, c_kc + 4 * G, c_kc + 5 * G
        c_gk = GQK // LANES
        c_gv = c_gk + GQK // LANES
        c_gr = c_gv + GV // LANES

        xn = rmsnorm(h, norm_mix_g[l])
        proj_nsa = matmul_wcast(xn, w, 0, o_ng, BF16, PROJ_TM, PROJ_TN, "proj_nsa").reshape(B, S, -1)
        proj_gla = matmul_wcast(xn, w[:, o_gq:o_ga], 0, o_ga - o_gq, BF16, PROJ_TM, PROJ_TN,
                                "proj_gla").reshape(B, S, -1)
        proj_mix = matmul_wcast(xn, w[:, o_ma:], 0, 2 * D, BF16, PROJ_TM, PROJ_TN, "proj_mix")
        small = matmul(xn, w_small, F32, PROJ_TM, LANES, "proj_small")

        pos2 = jnp.stack([cmp_k_pos[l], cmp_v_pos[l]]).reshape(2, 2, CMP_STRIDE * Dh)
        w1 = jnp.stack([cmp_k_w1[l], cmp_v_w1[l]]).astype(BF16)
        b1 = jnp.stack([cmp_k_b1[l], cmp_v_b1[l]]).reshape(2, 1, Dh)
        w2 = jnp.stack([cmp_k_w2[l], cmp_v_w2[l]]).astype(BF16)
        kvc = compress(proj_nsa, c_kc, pos2, w1, b1, w2)
        gates = small[:, :3 * NSA_HEADS].reshape(B, S, G, 3 * R).transpose(0, 2, 1, 3)
        slopes = jnp.exp2(-8.0 * jnp.arange(1, NSA_HEADS + 1, dtype=F32) / NSA_HEADS)
        o_nsa = nsa_attention(proj_nsa, kvc, gates, slopes, col_q=0, col_ks=c_ks, col_vs=c_vs,
                              col_kw=c_kw, col_vw=c_vw)

        xa = small[:, 3 * NSA_HEADS:n_small].reshape(B, S, GLA_GATE_RANK)
        o_gla = gla_attention(proj_gla, xa, gla_w_alpha[l], gla_b_alpha[l], gla_norm_g[l],
                              col_q=0, col_k=c_gk, col_v=c_gv, col_r=c_gr)

        mixed = gated_mix(o_nsa.reshape(T, -1), o_gla.reshape(T, -1), w_branch_nsa[l].astype(BF16),
                          w_branch_gla[l].astype(BF16), proj_mix, 0, D // LANES)
        h = out_proj(mixed, w_out[l].astype(BF16), h)

        n_r = N_GROUPS + N_EXPERTS
        w_r = jnp.concatenate([w_router_group[l], w_router_expert[l], jnp.zeros((D, LANES - n_r), F32)], axis=1)
        b_r = jnp.concatenate([b_router_group[l], b_router_expert[l], jnp.zeros((LANES - n_r,), F32)]).reshape(1, LANES)
        ids, wts, u = router(h, norm_ffn_g[l], w_r, b_r)
        dest, n_rows, tables = _dispatch_tables(ids[:, :2], MOE_TM)
        x_sorted = dispatch_rows(u, dest, n_rows)
        h_sorted = ffn_up(x_sorted, tables, w_exp_gate[l], w_exp_up[l])
        y_sorted = ffn_down(h_sorted, tables, w_exp_down[l])
        h = combine_norm(h, wts, dest, y_sorted, norm_final_g)
    return h.reshape(B, S, D)


def kernel(x, norm_mix_g, w_in, cmp_k_pos, cmp_k_w1, cmp_k_b1, cmp_k_w2, cmp_v_pos, cmp_v_w1, cmp_v_b1, cmp_v_w2, gla_w_alpha, gla_b_alpha, gla_norm_g, w_branch_nsa, w_branch_gla, w_out, norm_ffn_g, w_router_group, b_router_group, w_router_expert, b_router_expert, w_exp_gate, w_exp_up, w_exp_down, norm_final_g):
    return _forward(x, norm_mix_g, w_in, cmp_k_pos, cmp_k_w1, cmp_k_b1, cmp_k_w2, cmp_v_pos, cmp_v_w1, cmp_v_b1,
                    cmp_v_w2, gla_w_alpha, gla_b_alpha, gla_norm_g, w_branch_nsa, w_branch_gla, w_out, norm_ffn_g,
                    w_router_group, b_router_group, w_router_expert, b_router_expert, w_exp_gate, w_exp_up,
                    w_exp_down, norm_final_g)
```

```python
import functools

import jax
import jax.numpy as jnp
from jax import lax
from jax.experimental import pallas as pl
from jax.experimental.pallas import tpu as pltpu

F32 = jnp.float32
BF16 = jnp.bfloat16
I32 = jnp.int32

NSA_HEAD_DIM = 128
NSA_KV_HEADS = 4
NSA_GROUP = 4
NSA_HEADS = NSA_KV_HEADS * NSA_GROUP
CMP_BLOCK = 32
CMP_STRIDE = 16
SEL_BLOCK = 64
SEL_TOPN = 16
WINDOW = 512
BIG = 1e9
GLA_HEADS = 16
GLA_DK = 64
GLA_DV = 128
GLA_GATE_RANK = 16
GLA_TAU = 16.0
N_GROUPS = 4
EXPERTS_PER_GROUP = 8
N_EXPERTS = N_GROUPS * EXPERTS_PER_GROUP
NORM_EPS = 1e-6

LANES = 128
VMEM_LIMIT = 56 * 1024 * 1024
NEG = -1e30

NORM_ROWS = 512
PROJ_TM, PROJ_TN = 2048, 512
NSA_TQ, NSA_TK = 512, 512
GLA_TS, GLA_C = 512, 64
MIX_TM, MIX_TN = 1024, 1024
OUT_TM, OUT_TN = 1024, 1024
ROUTE_TM = 256
MOE_TM = 512
MOE_CJ = 512
MOE_CN = 2048
DISPATCH_TOKENS = 512
COMBINE_ROWS = 256


def _cparams(*sem):
    return pltpu.CompilerParams(dimension_semantics=sem, vmem_limit_bytes=VMEM_LIMIT)


def _dot(a, b, **kw):
    return jnp.dot(a, b, preferred_element_type=F32, **kw)


def _dot_nt(a, b, **kw):
    return lax.dot_general(a, b, (((1,), (1,)), ((), ())), preferred_element_type=F32, **kw)


def _dot_tn(a, b, **kw):
    return lax.dot_general(a, b, (((0,), (0,)), ((), ())), preferred_element_type=F32, **kw)


def _sigmoid(x):
    return 1.0 / (1.0 + jnp.exp(-x))


def _masked_softmax(s, mask):
    sm = jnp.where(mask, s, NEG)
    m = jnp.max(sm, axis=-1, keepdims=True)
    e = jnp.where(mask, jnp.exp(sm - m), 0.0)
    return e * (1.0 / jnp.maximum(jnp.sum(e, axis=-1, keepdims=True), 1e-30))


def _rmsnorm_kernel(x_ref, g_ref, o_ref):
    x = x_ref[...]
    ms = jnp.mean(x * x, axis=-1, keepdims=True)
    o_ref[...] = (x * lax.rsqrt(ms + NORM_EPS) * g_ref[...]).astype(o_ref.dtype)


def rmsnorm(x, g, out_dtype=BF16, rows=NORM_ROWS):
    T, D = x.shape
    rows = min(rows, T)
    return pl.pallas_call(
        _rmsnorm_kernel,
        out_shape=jax.ShapeDtypeStruct((T, D), out_dtype),
        grid=(T // rows,),
        in_specs=[pl.BlockSpec((rows, D), lambda i: (i, 0)),
                  pl.BlockSpec((1, D), lambda i: (0, 0))],
        out_specs=pl.BlockSpec((rows, D), lambda i: (i, 0)),
        compiler_params=_cparams("parallel"),
        name="rmsnorm",
    )(x, g.reshape(1, D))


def _mm_kernel(a_ref, b_ref, o_ref):
    o_ref[...] = _dot(a_ref[...], b_ref[...]).astype(o_ref.dtype)


def matmul(a, b, out_dtype, tm, tn, name):
    M, K = a.shape
    N = b.shape[1]
    tm, tn = min(tm, M), min(tn, N)
    return pl.pallas_call(
        _mm_kernel,
        out_shape=jax.ShapeDtypeStruct((M, N), out_dtype),
        grid=(M // tm, N // tn),
        in_specs=[pl.BlockSpec((tm, K), lambda i, j: (i, 0)),
                  pl.BlockSpec((K, tn), lambda i, j: (0, j))],
        out_specs=pl.BlockSpec((tm, tn), lambda i, j: (i, j)),
        compiler_params=_cparams("parallel", "arbitrary"),
        name=name,
    )(a, b)


def _compress_kernel(seq_ref, pos_ref, w1_ref, b1_ref, w2_ref, o_ref):
    seq = seq_ref[0]
    nc = seq.shape[0] // CMP_STRIDE
    x = seq.reshape(nc, CMP_STRIDE * seq.shape[1]).astype(F32)
    half = x.shape[1]
    pos = pos_ref[0]
    w1 = w1_ref[0]
    u0 = _dot((x + pos[0:1, :]).astype(BF16), w1[:half, :])
    u1 = _dot((x + pos[1:2, :]).astype(BF16), w1[half:, :])
    pre = u0 + pltpu.roll(u1, nc - 1, 0) + b1_ref[0]
    h = 0.5 * pre * (1.0 + jnp.tanh(0.7978845608028654 * (pre + 0.044715 * pre * pre * pre)))
    o_ref[0, 0, 0] = _dot(h.astype(BF16), w2_ref[0]).astype(o_ref.dtype)


def compress(proj, col_kc, pos2, w1, b1, w2):
    B, S, _ = proj.shape
    G = NSA_KV_HEADS
    Dh = w2.shape[-1]
    NC, HW = S // CMP_STRIDE, CMP_STRIDE * Dh
    return pl.pallas_call(
        _compress_kernel,
        out_shape=jax.ShapeDtypeStruct((2, B, G, NC, Dh), BF16),
        grid=(2, B, G),
        in_specs=[pl.BlockSpec((1, S, Dh), lambda a, b, g: (b, 0, col_kc + a * G + g)),
                  pl.BlockSpec((1, 2, HW), lambda a, b, g: (a, 0, 0)),
                  pl.BlockSpec((1, 2 * HW, Dh), lambda a, b, g: (a, 0, 0)),
                  pl.BlockSpec((1, 1, Dh), lambda a, b, g: (a, 0, 0)),
                  pl.BlockSpec((1, Dh, Dh), lambda a, b, g: (a, 0, 0))],
        out_specs=pl.BlockSpec((1, 1, 1, NC, Dh), lambda a, b, g: (a, b, g, 0, 0)),
        compiler_params=_cparams("parallel", "parallel", "parallel"),
        name="nsa_compress",
    )(proj, pos2, w1, b1, w2)


MASK_BIG = 2.0 ** 100


def _nsa_kernel(slopes_ref, q_ref, kc_ref, vc_ref, ks_ref, vs_ref, kw_ref, vw_ref, g_ref,
                paux_ref, saux_ref, caux_ref, mmat_ref, o_ref,
                qx_ref, sc_ref, sw_ref, sa_ref, sb_ref, score_ref, m_ref, acc_ref,
                *, TQ, TK, NC, NSEL, NTOP):
    R, Dh = NSA_GROUP, NSA_HEAD_DIM
    g = pl.program_id(1)
    q0 = pl.program_id(2) * TQ
    scale = Dh ** -0.5
    lane = lax.broadcasted_iota(I32, (1, LANES), 1)
    t1 = q0 + lax.broadcasted_iota(I32, (TQ, 1), 0)
    head = lambda r: slice(r * TQ, (r + 1) * TQ)

    def alibi_cols(r):
        hi, lo = slopes_ref[0, g * R + r], slopes_ref[1, g * R + r]
        c = jnp.where(lane == 0, 64.0 * hi, jnp.where(lane == 1, 64.0 * lo,
                                                      jnp.where(lane == 2, hi, jnp.where(lane == 3, lo, 0.0))))
        return jnp.broadcast_to(c, (TQ, LANES))

    for r in range(R):
        qx_ref[head(r), 0:Dh] = (q_ref[0, :, r * Dh:(r + 1) * Dh].astype(F32) * scale).astype(BF16)
        qx_ref[head(r), Dh:Dh + LANES] = alibi_cols(r).astype(BF16)

    def with_ones(v):
        return jnp.concatenate([v, jnp.ones(v.shape, v.dtype)], axis=1)

    def exp_pv(s, v1):
        m = jnp.broadcast_to(jnp.max(s, axis=-1, keepdims=True), (TQ, LANES))
        e = [jnp.exp(s[:, c * LANES:(c + 1) * LANES] - m) for c in range(s.shape[1] // LANES)]
        return e, _dot(jnp.concatenate(e, axis=1).astype(BF16), v1)

    WK = WINDOW + TQ
    ws = pl.multiple_of(jnp.maximum(q0 - WINDOW, 0), TQ)
    sc_ref[...] = _dot_nt(qx_ref[...], jnp.concatenate([kc_ref[0, 0, 0], caux_ref[...]], axis=1))
    sw_ref[...] = _dot_nt(qx_ref[...], jnp.concatenate([kw_ref[0, pl.ds(ws, WK), :], paux_ref[pl.ds(ws, WK), :]],
                                                       axis=1))

    cmp_end = lax.broadcasted_iota(I32, (1, NC), 1) * CMP_STRIDE + (CMP_BLOCK - 1)
    cmp_bias = jnp.where(cmp_end <= t1, 0.0, NEG)
    row_ok = q0 + lax.broadcasted_iota(I32, (TQ, LANES), 0) >= CMP_BLOCK - 1
    vc1 = with_ones(vc_ref[0, 0, 0])
    o_cmp = []
    imp = None
    for r in range(R):
        e, o2 = exp_pv(sc_ref[head(r), :] + cmp_bias, vc1)
        inv = jnp.where(row_ok, 1.0 / jnp.maximum(o2[:, Dh:], 1e-30), 0.0)
        o_cmp.append(o2[:, :Dh] * inv)
        p = jnp.concatenate([ec * inv for ec in e], axis=1)
        imp = p if imp is None else imp + p

    vw1 = with_ones(vw_ref[0, pl.ds(ws, WK), :])
    dw = t1 - (ws + lax.broadcasted_iota(I32, (1, WK), 1))
    win_bias = jnp.where((dw >= 0) & (dw < WINDOW), 0.0, NEG)
    o_win = []
    for r in range(R):
        _, o2 = exp_pv(sw_ref[head(r), :] + win_bias, vw1)
        o_win.append(o2[:, :Dh] * (1.0 / jnp.maximum(o2[:, Dh:], 1e-30)))

    mm = mmat_ref[...]
    i1 = imp.astype(BF16)
    rem = imp - i1.astype(F32)
    i2 = rem.astype(BF16)
    i3 = (rem - i2.astype(F32)).astype(BF16)
    imp_sel = _dot_nt(mm, i1) + _dot_nt(mm, i2) + _dot_nt(mm, i3)

    cur = (q0 + lax.broadcasted_iota(I32, (1, TQ), 1)) // SEL_BLOCK
    blk = lax.broadcasted_iota(I32, (NSEL, 1), 0)
    forced = (blk == 0) | (blk == cur) | (blk == cur - 1)
    score = jnp.where(blk <= cur, jnp.where(forced, BIG, imp_sel), -BIG)
    score_ref[...] = score

    def rank_pair(i2, cnt):
        for d in range(2):
            ii = 2 * i2 + d
            other = score_ref[pl.ds(ii, 1), :]
            beats = (other > score) | ((other == score) & (blk > ii))
            cnt = cnt + jnp.where(beats, 1.0, 0.0)
        return cnt

    n_blk = (q0 + TQ) // SEL_BLOCK
    cnt = lax.fori_loop(0, n_blk // 2, rank_pair, jnp.zeros((NSEL, TQ), F32))
    unsel = jnp.where(cnt < NTOP, 0.0, -1.0).astype(BF16)
    place = jnp.where(lax.broadcasted_iota(I32, (NSEL, LANES), 1)
                      == lax.broadcasted_iota(I32, (NSEL, LANES), 0) + LANES // 2, 1.0, 0.0).astype(BF16)
    unsel_l = _dot_tn(unsel, place)
    for r in range(R):
        qx_ref[head(r), Dh:Dh + LANES] = (alibi_cols(r) + unsel_l).astype(BF16)

    m_ref[...] = jnp.full(m_ref.shape, NEG, F32)
    acc_ref[...] = jnp.zeros(acc_ref.shape, F32)
    pos_in_tile = lax.broadcasted_iota(I32, (1, TK), 1)

    def issue_scores(kt, s_ref):
        k0 = pl.multiple_of(kt * TK, TK)
        kk = jnp.concatenate([ks_ref[0, pl.ds(k0, TK), :], saux_ref[pl.ds(k0, TK), :]], axis=1)
        s_ref[...] = _dot_nt(qx_ref[...], kk)

    def consume_scores(kt, s_ref, causal):
        k0 = pl.multiple_of(kt * TK, TK)
        vv1 = with_ones(vs_ref[0, pl.ds(k0, TK), :])
        if causal:
            causal_bias = jnp.where(t1 >= k0 + pos_in_tile, 0.0, -MASK_BIG)
        for r in range(R):
            sc = s_ref[head(r), :]
            if causal:
                sc = sc + causal_bias
            m_old = m_ref[r]
            m_new = jnp.maximum(m_old, jnp.broadcast_to(jnp.max(sc, axis=-1, keepdims=True), (TQ, LANES)))
            alpha = jnp.exp(m_old - m_new)
            p = jnp.concatenate([jnp.exp(sc[:, c * LANES:(c + 1) * LANES] - m_new) for c in range(TK // LANES)],
                                axis=1).astype(BF16)
            m_ref[r] = m_new
            acc_ref[r] = jnp.concatenate([alpha, alpha], axis=1) * acc_ref[r] + _dot(p, vv1)

    def sel_pair(k, carry):
        issue_scores(2 * k + 1, sb_ref)
        consume_scores(2 * k, sa_ref, False)
        issue_scores(2 * k + 2, sa_ref)
        consume_scores(2 * k + 1, sb_ref, False)
        return carry

    n_below = q0 // TK
    issue_scores(0, sa_ref)
    lax.fori_loop(0, n_below // 2, sel_pair, 0)

    @pl.when(n_below % 2 == 1)
    def _():
        issue_scores(n_below, sb_ref)
        consume_scores(n_below - 1, sa_ref, False)
        consume_scores(n_below, sb_ref, True)

    @pl.when(n_below % 2 == 0)
    def _():
        consume_scores(n_below, sa_ref, True)

    gt = _sigmoid(g_ref[0, 0])
    for r in range(R):
        o_sel = acc_ref[r, :, 0:Dh] * (1.0 / jnp.maximum(acc_ref[r, :, Dh:2 * Dh], 1e-30))
        o = (gt[:, 3 * r:3 * r + 1] * o_cmp[r] + gt[:, 3 * r + 1:3 * r + 2] * o_sel
             + gt[:, 3 * r + 2:3 * r + 3] * o_win[r])
        o_ref[0, :, r * Dh:(r + 1) * Dh] = o.astype(o_ref.dtype)


def _nsa_constants(S, NC, NSEL):
    def pos_cols(pos):
        col = jnp.arange(LANES)[None, :]
        hi, lo = (pos // 64)[:, None], (pos % 64)[:, None]
        return jnp.where(col < 2, hi, jnp.where(col < 4, lo, 0)).astype(F32)

    pos = jnp.arange(S)
    paux = pos_cols(pos)
    onehot = (jnp.arange(LANES)[None, :] - LANES // 2 == (pos // SEL_BLOCK)[:, None]) & (jnp.arange(LANES)[None, :] >= LANES // 2)
    saux = paux + jnp.where(onehot, MASK_BIG, 0.0)
    caux = pos_cols(jnp.arange(NC) * CMP_STRIDE + (CMP_BLOCK - 1))
    d = jnp.arange(NC)[None, :] - 4 * jnp.arange(NSEL)[:, None]
    mmat = jnp.where((d == -1) | (d == 3), 1.0, jnp.where((d >= 0) & (d <= 2), 2.0, 0.0))
    return paux.astype(BF16), saux.astype(BF16), caux.astype(BF16), mmat.astype(BF16)


def nsa_attention(proj, kvc, gates, slopes, *, col_q, col_ks, col_vs, col_kw, col_vw):
    B, S, _ = proj.shape
    G, R, Dh = NSA_KV_HEADS, NSA_GROUP, NSA_HEAD_DIM
    NC = kvc.shape[3]
    NSEL = S // SEL_BLOCK
    TQ, TK = min(NSA_TQ, S), min(NSA_TK, S)
    assert S % TQ == 0 and TK % TQ == 0 and S % TK == 0 and S >= WINDOW + TQ and NC == S // CMP_STRIDE
    assert NSEL <= LANES // 2 and S // 64 <= 256 and TQ % (2 * SEL_BLOCK) == 0
    s_hi = slopes.astype(BF16).astype(F32)
    s_lo = (slopes - s_hi).astype(BF16).astype(F32)
    paux, saux, caux, mmat = _nsa_constants(S, NC, NSEL)
    kern = functools.partial(_nsa_kernel, TQ=TQ, TK=TK, NC=NC, NSEL=NSEL, NTOP=min(SEL_TOPN, NSEL))
    kv_spec = lambda col: pl.BlockSpec((1, S, Dh), lambda b, g, i: (b, 0, col + g))
    whole = lambda a: pl.BlockSpec(a.shape, lambda b, g, i: (0, 0))
    return pl.pallas_call(
        kern,
        out_shape=jax.ShapeDtypeStruct((B, S, G * R * Dh), BF16),
        grid=(B, G, S // TQ),
        in_specs=[pl.BlockSpec(memory_space=pltpu.SMEM),
                  pl.BlockSpec((1, TQ, R * Dh), lambda b, g, i: (b, i, col_q // R + g)),
                  pl.BlockSpec((1, 1, 1, NC, Dh), lambda b, g, i: (0, b, g, 0, 0)),
                  pl.BlockSpec((1, 1, 1, NC, Dh), lambda b, g, i: (1, b, g, 0, 0)),
                  kv_spec(col_ks), kv_spec(col_vs), kv_spec(col_kw), kv_spec(col_vw),
                  pl.BlockSpec((1, 1, TQ, gates.shape[-1]), lambda b, g, i: (b, g, i, 0)),
                  whole(paux), whole(saux), whole(caux), whole(mmat)],
        out_specs=pl.BlockSpec((1, TQ, R * Dh), lambda b, g, i: (b, i, g)),
        scratch_shapes=[pltpu.VMEM((R * TQ, Dh + LANES), BF16), pltpu.VMEM((R * TQ, NC), F32),
                        pltpu.VMEM((R * TQ, WINDOW + TQ), F32), pltpu.VMEM((R * TQ, TK), F32),
                        pltpu.VMEM((R * TQ, TK), F32),
                        pltpu.VMEM((NSEL, TQ), F32), pltpu.VMEM((R, TQ, LANES), F32),
                        pltpu.VMEM((R, TQ, 2 * Dh), F32)],
        compiler_params=_cparams("parallel", "parallel", "arbitrary"),
        name="nsa_attention",
    )(jnp.stack([s_hi, s_lo]), proj, kvc, kvc, proj, proj, proj, proj, gates, paux, saux, caux, mmat)


def _gla_kernel(q_ref, k_ref, v_ref, r_ref, xa_ref, wa_ref, ba_ref, ng_ref, o_ref,
                st_ref, b_ref, q2_ref, kh_ref, qe_ref, kd_ref, oacc_ref, *, TS, C):
    DK2, DV2 = 2 * GLA_DK, 2 * GLA_DV
    NCH = TS // C
    LEVELS = [C >> (i + 1) for i in range(C.bit_length() - 1)]

    @pl.when(pl.program_id(2) == 0)
    def _():
        st_ref[...] = jnp.zeros_like(st_ref)

    xa = xa_ref[0]
    xa_hi = xa.astype(BF16)
    xa_lo = (xa - xa_hi.astype(F32)).astype(BF16)
    z = _dot(xa_hi, wa_ref[0]) + _dot(xa_lo, wa_ref[0]) + _dot(xa_hi, wa_ref[1]) + ba_ref[...]
    b = (jnp.minimum(z, 0.0) - jnp.log(1.0 + jnp.exp(-jnp.abs(z)))) * (1.0 / GLA_TAU)
    row = lax.broadcasted_iota(I32, (TS, 1), 0)
    step = 1
    while step < C:
        b = b + jnp.where(row % C >= step, pltpu.roll(b, step, 0), 0.0)
        step *= 2
    b_ref[...] = b

    head_a = lax.broadcasted_iota(I32, (1, DK2), 1) < GLA_DK
    q = q_ref[0].astype(F32) * (GLA_DK ** -0.5)
    k = k_ref[0].astype(F32)

    def ref_rows(h):
        if h >= 4:
            g = TS // (2 * h)
            return jnp.broadcast_to(b.reshape(g, 2 * h, DK2)[:, h - 1:h, :], (g, 2 * h, DK2)).reshape(TS, DK2)
        dn1 = pltpu.roll(b, 1, 0)
        if h == 1:
            return jnp.where(row % 2 == 0, b, dn1)
        m4 = row % 4
        return jnp.where(m4 == 0, pltpu.roll(b, TS - 1, 0),
                         jnp.where(m4 == 1, b, jnp.where(m4 == 2, dn1, pltpu.roll(b, 2, 0))))

    def put(lvl, qh, kh):
        q2_ref[lvl, :, 0:C, :] = jnp.where(head_a, qh, 0.0).astype(BF16).reshape(NCH, C, DK2)
        q2_ref[lvl, :, C:2 * C, :] = jnp.where(head_a, 0.0, qh).astype(BF16).reshape(NCH, C, DK2)
        kh_ref[lvl] = kh.astype(BF16)

    put(0, q, k)
    for lvl, h in enumerate(LEVELS, start=1):
        r = ref_rows(h)
        put(lvl, q * jnp.exp(jnp.minimum(b - r, 0.0)), k * jnp.exp(jnp.minimum(r - b, 0.0)))
    b_last = jnp.broadcast_to(b.reshape(NCH, C, DK2)[:, C - 1:C, :], (NCH, C, DK2)).reshape(TS, DK2)
    qe_ref[...] = (q * jnp.exp(b)).astype(BF16)
    kd_ref[...] = (k * jnp.exp(b_last - b)).astype(BF16)

    row2 = lax.broadcasted_iota(I32, (2 * C, 1), 0) % C
    col = lax.broadcasted_iota(I32, (1, C), 1)
    masks = [row2 == col] + [(row2 // (2 * h) == col // (2 * h)) & (row2 % (2 * h) >= h) & (col % (2 * h) < h)
                             for h in LEVELS]
    vk_same = (lax.broadcasted_iota(I32, (DV2, DK2), 0) < GLA_DV) == (lax.broadcasted_iota(I32, (DV2, DK2), 1) < GLA_DK)
    for c in range(NCH):
        rows = slice(c * C, (c + 1) * C)
        vb = v_ref[0, rows, :]
        a2 = jnp.zeros((2 * C, C), F32)
        for lvl, mask in enumerate(masks):
            a2 = a2 + jnp.where(mask, _dot_nt(q2_ref[lvl, c], kh_ref[lvl, rows, :]), 0.0)
        a2 = a2.astype(BF16)
        o = jnp.concatenate([_dot(a2[0:C], vb[:, :GLA_DV]), _dot(a2[C:2 * C], vb[:, GLA_DV:])], axis=1)
        st = st_ref[...]
        oacc_ref[rows, :] = o + _dot_nt(qe_ref[rows, :], st.astype(BF16))
        ds = jnp.where(vk_same, _dot_tn(vb, kd_ref[rows, :]), 0.0)
        st_ref[...] = st * jnp.exp(b_ref[(c + 1) * C - 1:(c + 1) * C, :]) + ds

    def ln(x):
        mu = jnp.mean(x, axis=-1, keepdims=True)
        xc = x - mu
        return xc * lax.rsqrt(jnp.mean(xc * xc, axis=-1, keepdims=True) + NORM_EPS)

    o = oacc_ref[...]
    on = jnp.concatenate([ln(o[:, :GLA_DV]), ln(o[:, GLA_DV:])], axis=1) * ng_ref[...]
    rr = r_ref[0].astype(F32)
    o_ref[0] = (on * (rr * _sigmoid(rr))).astype(o_ref.dtype)


def gla_attention(proj, xa, w_alpha, b_alpha, norm_g, *, col_q, col_k, col_v, col_r):
    B, S, _ = proj.shape
    HP = GLA_HEADS // 2
    DK2, DV2 = 2 * GLA_DK, 2 * GLA_DV
    TS, C = min(GLA_TS, S), GLA_C
    assert S % TS == 0 and TS % C == 0 and C & (C - 1) == 0 and C >= 16
    n_lvl = C.bit_length()
    wa_hi = w_alpha.astype(BF16)
    wa_split = jnp.stack([wa_hi, (w_alpha - wa_hi.astype(F32)).astype(BF16)])
    kern = functools.partial(_gla_kernel, TS=TS, C=C)
    return pl.pallas_call(
        kern,
        out_shape=jax.ShapeDtypeStruct((B, S, GLA_HEADS * GLA_DV), BF16),
        grid=(B, HP, S // TS),
        in_specs=[pl.BlockSpec((1, TS, DK2), lambda b, p, s: (b, s, col_q + p)),
                  pl.BlockSpec((1, TS, DK2), lambda b, p, s: (b, s, col_k + p)),
                  pl.BlockSpec((1, TS, DV2), lambda b, p, s: (b, s, col_v // 2 + p)),
                  pl.BlockSpec((1, TS, DV2), lambda b, p, s: (b, s, col_r // 2 + p)),
                  pl.BlockSpec((1, TS, GLA_GATE_RANK), lambda b, p, s: (b, s, 0)),
                  pl.BlockSpec((2, GLA_GATE_RANK, DK2), lambda b, p, s: (0, 0, p)),
                  pl.BlockSpec((1, DK2), lambda b, p, s: (0, p)),
                  pl.BlockSpec((1, DV2), lambda b, p, s: (0, p))],
        out_specs=pl.BlockSpec((1, TS, DV2), lambda b, p, s: (b, s, p)),
        scratch_shapes=[pltpu.VMEM((DV2, DK2), F32), pltpu.VMEM((TS, DK2), F32),
                        pltpu.VMEM((n_lvl, TS // C, 2 * C, DK2), BF16), pltpu.VMEM((n_lvl, TS, DK2), BF16),
                        pltpu.VMEM((TS, DK2), BF16), pltpu.VMEM((TS, DK2), BF16), pltpu.VMEM((TS, DV2), F32)],
        compiler_params=_cparams("parallel", "parallel", "arbitrary"),
        name="gla_attention",
    )(proj, proj, proj, proj, xa, wa_split, b_alpha.reshape(1, -1), norm_g.reshape(1, -1))


def _mix_kernel(oa_ref, ob_ref, wa_ref, wb_ref, ma_ref, mb_ref, o_ref):
    ya = _dot(oa_ref[...], wa_ref[...])
    yb = _dot(ob_ref[...], wb_ref[...])
    o_ref[...] = (_sigmoid(ma_ref[...].astype(F32)) * ya + _sigmoid(mb_ref[...].astype(F32)) * yb).astype(o_ref.dtype)


def gated_mix(o_nsa, o_gla, wa, wb, proj2d, col_ma, col_mb):
    T, KA = o_nsa.shape
    KB = o_gla.shape[1]
    N = wa.shape[1]
    tm, tn = min(MIX_TM, T), min(MIX_TN, N)
    ca, cb = col_ma * LANES // tn, col_mb * LANES // tn
    assert (col_ma * LANES) % tn == 0 and (col_mb * LANES) % tn == 0
    return pl.pallas_call(
        _mix_kernel,
        out_shape=jax.ShapeDtypeStruct((T, N), BF16),
        grid=(T // tm, N // tn),
        in_specs=[pl.BlockSpec((tm, KA), lambda i, j: (i, 0)),
                  pl.BlockSpec((tm, KB), lambda i, j: (i, 0)),
                  pl.BlockSpec((KA, tn), lambda i, j: (0, j)),
                  pl.BlockSpec((KB, tn), lambda i, j: (0, j)),
                  pl.BlockSpec((tm, tn), lambda i, j: (i, ca + j)),
                  pl.BlockSpec((tm, tn), lambda i, j: (i, cb + j))],
        out_specs=pl.BlockSpec((tm, tn), lambda i, j: (i, j)),
        compiler_params=_cparams("parallel", "arbitrary"),
        name="gated_mix",
    )(o_nsa, o_gla, wa, wb, proj2d, proj2d)


def _out_kernel(a_ref, w_ref, x_ref, o_ref):
    o_ref[...] = x_ref[...] + _dot(a_ref[...], w_ref[...])


def out_proj(mixed, w, x):
    T, K = mixed.shape
    N = w.shape[1]
    tm, tn = min(OUT_TM, T), min(OUT_TN, N)
    return pl.pallas_call(
        _out_kernel,
        out_shape=jax.ShapeDtypeStruct((T, N), F32),
        grid=(T // tm, N // tn),
        in_specs=[pl.BlockSpec((tm, K), lambda i, j: (i, 0)),
                  pl.BlockSpec((K, tn), lambda i, j: (0, j)),
                  pl.BlockSpec((tm, tn), lambda i, j: (i, j))],
        out_specs=pl.BlockSpec((tm, tn), lambda i, j: (i, j)),
        compiler_params=_cparams("parallel", "arbitrary"),
        name="out_proj",
    )(mixed, w, x)


def _router_kernel(h_ref, g_ref, w_ref, b_ref, id_ref, wt_ref, u_ref):
    x = h_ref[...]
    u = x * lax.rsqrt(jnp.mean(x * x, axis=-1, keepdims=True) + NORM_EPS) * g_ref[...]
    u_hi = u.astype(BF16)
    u_ref[...] = u_hi.reshape(u_ref.shape)
    u_lo = (u - u_hi.astype(F32)).astype(BF16)
    logit = _dot(u_hi, w_ref[0]) + _dot(u_lo, w_ref[0]) + _dot(u_hi, w_ref[1]) + b_ref[...]
    lane = lax.broadcasted_iota(I32, logit.shape, 1)
    big = jnp.int32(1 << 20)
    gmask = lane < N_GROUPS
    gl = jnp.where(gmask, logit, NEG)
    gmax = jnp.max(gl, axis=-1, keepdims=True)
    g_star = jnp.min(jnp.where(gmask & (gl == gmax), lane, big), axis=-1, keepdims=True)
    p_group = 1.0 / jnp.sum(jnp.where(gmask, jnp.exp(gl - gmax), 0.0), axis=-1, keepdims=True)
    e_lo = N_GROUPS + EXPERTS_PER_GROUP * g_star
    emask = (lane >= e_lo) & (lane < e_lo + EXPERTS_PER_GROUP)
    el = jnp.where(emask, logit, NEG)
    m1 = jnp.max(el, axis=-1, keepdims=True)
    i1 = jnp.min(jnp.where(emask & (el == m1), lane, big), axis=-1, keepdims=True)
    emask2 = emask & (lane != i1)
    el2 = jnp.where(emask2, logit, NEG)
    m2 = jnp.max(el2, axis=-1, keepdims=True)
    i2 = jnp.min(jnp.where(emask2 & (el2 == m2), lane, big), axis=-1, keepdims=True)
    e2 = jnp.exp(m2 - m1)
    w1 = p_group / (1.0 + e2)
    w2 = p_group * e2 / (1.0 + e2)
    id_ref[...] = jnp.where(lane == 0, i1 - N_GROUPS, jnp.where(lane == 1, i2 - N_GROUPS, 0))
    wt_ref[...] = jnp.where(lane == 0, w1, jnp.where(lane == 1, w2, 0.0))


def router(h, g, w_r, b_r):
    T, D = h.shape
    tm = min(ROUTE_TM, T)
    w_hi = w_r.astype(BF16)
    return pl.pallas_call(
        _router_kernel,
        out_shape=(jax.ShapeDtypeStruct((T, LANES), I32), jax.ShapeDtypeStruct((T, LANES), F32),
                   jax.ShapeDtypeStruct((T, D // LANES, LANES), BF16)),
        grid=(T // tm,),
        in_specs=[pl.BlockSpec((tm, D), lambda i: (i, 0)),
                  pl.BlockSpec((1, D), lambda i: (0, 0)),
                  pl.BlockSpec((2, D, LANES), lambda i: (0, 0, 0)),
                  pl.BlockSpec((1, LANES), lambda i: (0, 0))],
        out_specs=(pl.BlockSpec((tm, LANES), lambda i: (i, 0)), pl.BlockSpec((tm, LANES), lambda i: (i, 0)),
                   pl.BlockSpec((tm, D // LANES, LANES), lambda i: (i, 0, 0))),
        compiler_params=_cparams("parallel"),
        name="moe_router",
    )(h, g.reshape(1, D), jnp.stack([w_hi, (w_r - w_hi.astype(F32)).astype(BF16)]), b_r)


def _row_copy(src_ref, dst_ref, sem, src_row, dst_row):
    return pltpu.make_async_copy(src_ref.at[pl.ds(src_row, 1)], dst_ref.at[pl.ds(dst_row, 1)], sem)


def _dispatch_rows_kernel(dest_ref, u_ref, init_hbm, out_hbm, sem, *, toks):
    def start(t, c):
        for k in range(2):
            _row_copy(u_ref, out_hbm, sem, t, dest_ref[0, 0, 2 * t + k]).start(priority=k)
        return c

    def wait(t, c):
        for k in range(2):
            _row_copy(u_ref, out_hbm, sem, t, 0).wait()
        return c

    lax.fori_loop(0, toks, start, 0)
    lax.fori_loop(0, toks, wait, 0)


def dispatch_rows(u3, dest, n_rows):
    T = u3.shape[0]
    toks = min(DISPATCH_TOKENS, T)
    assert T % toks == 0
    out_shape = jax.ShapeDtypeStruct((n_rows,) + u3.shape[1:], u3.dtype)
    return pl.pallas_call(
        functools.partial(_dispatch_rows_kernel, toks=toks),
        out_shape=out_shape,
        grid=(T // toks,),
        in_specs=[pl.BlockSpec((1, 1, 2 * toks), lambda i: (i, 0, 0), memory_space=pltpu.SMEM),
                  pl.BlockSpec((toks,) + u3.shape[1:], lambda i: (i, 0, 0)),
                  pl.BlockSpec(memory_space=pl.ANY)],
        out_specs=pl.BlockSpec(memory_space=pl.ANY),
        scratch_shapes=[pltpu.SemaphoreType.DMA],
        input_output_aliases={2: 0},
        compiler_params=_cparams("arbitrary"),
        name="moe_dispatch_rows",
    )(dest.reshape(T // toks, 1, 2 * toks), u3, jnp.zeros(out_shape.shape, out_shape.dtype))


def _ffn_up_kernel(te_ref, new_ref, nv_ref, x_ref, wg_ref, wu_ref, o_ref, wg_bf, wu_bf):
    i = pl.program_id(1)
    used = i < nv_ref[0]
    fresh = new_ref[i] == 1

    def swiglu():
        x = x_ref[...].reshape(x_ref.shape[0], wg_bf.shape[0])
        a = _dot(x, wg_bf[...])
        o_ref[...] = (a * _sigmoid(a) * _dot(x, wu_bf[...])).astype(o_ref.dtype)

    @pl.when(fresh)
    def _():
        wg_bf[...] = wg_ref[0].astype(BF16)
        wu_bf[...] = wu_ref[0].astype(BF16)
        swiglu()

    pl.when(jnp.logical_and(used, jnp.logical_not(fresh)))(swiglu)

    @pl.when(jnp.logical_not(used))
    def _():
        o_ref[...] = jnp.zeros_like(o_ref)


def ffn_up(x_sorted, tables, w_gate, w_up):
    NP = x_sorted.shape[0]
    E, D, DE = w_gate.shape
    tm, cj = MOE_TM, min(MOE_CJ, DE)
    return pl.pallas_call(
        _ffn_up_kernel,
        out_shape=jax.ShapeDtypeStruct((NP, DE), BF16),
        grid_spec=pltpu.PrefetchScalarGridSpec(
            num_scalar_prefetch=3,
            grid=(DE // cj, NP // tm),
            in_specs=[pl.BlockSpec((tm,) + x_sorted.shape[1:], lambda j, i, te, nw, nv: (i, 0, 0)),
                      pl.BlockSpec((1, D, cj), lambda j, i, te, nw, nv: (te[i], 0, j)),
                      pl.BlockSpec((1, D, cj), lambda j, i, te, nw, nv: (te[i], 0, j))],
            out_specs=pl.BlockSpec((tm, cj), lambda j, i, te, nw, nv: (i, j)),
            scratch_shapes=[pltpu.VMEM((D, cj), BF16), pltpu.VMEM((D, cj), BF16)]),
        compiler_params=_cparams("arbitrary", "arbitrary"),
        name="moe_ffn_up",
    )(*tables, x_sorted, w_gate, w_up)


def _ffn_down_kernel(te_ref, new_ref, nv_ref, h_ref, wd_ref, o_ref, wd_bf):
    i = pl.program_id(1)
    used = i < nv_ref[0]
    fresh = new_ref[i] == 1

    def down():
        o_ref[...] = _dot(h_ref[...], wd_bf[...]).astype(o_ref.dtype).reshape(o_ref.shape)

    @pl.when(fresh)
    def _():
        wd_bf[...] = wd_ref[0].astype(BF16)
        down()

    pl.when(jnp.logical_and(used, jnp.logical_not(fresh)))(down)

    @pl.when(jnp.logical_not(used))
    def _():
        o_ref[...] = jnp.zeros_like(o_ref)


def ffn_down(h_sorted, tables, w_down):
    NP, DE = h_sorted.shape
    E, _, D = w_down.shape
    tm, cn = MOE_TM, min(MOE_CN, D)
    return pl.pallas_call(
        _ffn_down_kernel,
        out_shape=jax.ShapeDtypeStruct((NP, D // LANES, LANES), BF16),
        grid_spec=pltpu.PrefetchScalarGridSpec(
            num_scalar_prefetch=3,
            grid=(D // cn, NP // tm),
            in_specs=[pl.BlockSpec((tm, DE), lambda j, i, te, nw, nv: (i, 0)),
                      pl.BlockSpec((1, DE, cn), lambda j, i, te, nw, nv: (te[i], 0, j))],
            out_specs=pl.BlockSpec((tm, cn // LANES, LANES), lambda j, i, te, nw, nv: (i, j, 0)),
            scratch_shapes=[pltpu.VMEM((DE, cn), BF16)]),
        compiler_params=_cparams("arbitrary", "arbitrary"),
        name="moe_ffn_down",
    )(*tables, h_sorted, w_down)


def _combine_kernel(idx_ref, nxt_ref, h_ref, wt_ref, y_hbm, g_ref, o_ref, buf, sem, *, rows):
    i = pl.program_id(0)
    slot = i % 2

    def issue(table, s):
        def body(r, c):
            _row_copy(y_hbm, buf.at[s, 0], sem.at[s], table[0, 0, 2 * r], r).start(priority=0)
            _row_copy(y_hbm, buf.at[s, 1], sem.at[s], table[0, 0, 2 * r + 1], r).start(priority=1)
            return c
        lax.fori_loop(0, rows, body, 0)

    @pl.when(i == 0)
    def _():
        issue(idx_ref, 0)

    @pl.when(i + 1 < pl.num_programs(0))
    def _():
        issue(nxt_ref, 1 - slot)

    def wait(r, c):
        _row_copy(y_hbm, buf.at[slot, 0], sem.at[slot], 0, r).wait()
        _row_copy(y_hbm, buf.at[slot, 1], sem.at[slot], 0, r).wait()
        return c

    lax.fori_loop(0, rows, wait, 0)
    wt = wt_ref[...]
    y0 = buf[slot, 0].reshape(h_ref.shape).astype(F32)
    y1 = buf[slot, 1].reshape(h_ref.shape).astype(F32)
    x = h_ref[...] + wt[:, 0:1] * y0 + wt[:, 1:2] * y1
    o_ref[...] = x * lax.rsqrt(jnp.mean(x * x, axis=-1, keepdims=True) + NORM_EPS) * g_ref[...]


def combine_norm(h, wts, dest, y3, g):
    T, D = h.shape
    rows = min(COMBINE_ROWS, T)
    n = T // rows
    table = dest.reshape(n, 1, 2 * rows)
    return pl.pallas_call(
        functools.partial(_combine_kernel, rows=rows),
        out_shape=jax.ShapeDtypeStruct((T, D), F32),
        grid=(n,),
        in_specs=[pl.BlockSpec((1, 1, 2 * rows), lambda i: (i, 0, 0), memory_space=pltpu.SMEM),
                  pl.BlockSpec((1, 1, 2 * rows), lambda i: (jnp.minimum(i + 1, n - 1), 0, 0),
                               memory_space=pltpu.SMEM),
                  pl.BlockSpec((rows, D), lambda i: (i, 0)),
                  pl.BlockSpec((rows, LANES), lambda i: (i, 0)),
                  pl.BlockSpec(memory_space=pl.ANY),
                  pl.BlockSpec((1, D), lambda i: (0, 0))],
        out_specs=pl.BlockSpec((rows, D), lambda i: (i, 0)),
        scratch_shapes=[pltpu.VMEM((2, 2, rows) + y3.shape[1:], y3.dtype), pltpu.SemaphoreType.DMA((2,))],
        compiler_params=_cparams("arbitrary"),
        name="moe_combine_norm",
    )(table, table, h, wts, y3, g.reshape(1, D))


def _dispatch_tables(ids, tm):
    T = ids.shape[0]
    E = N_EXPERTS
    eid = ids.reshape(-1)
    onehot = (eid[:, None] == jnp.arange(E, dtype=I32)[None, :]).astype(I32)
    csum = jnp.cumsum(onehot, axis=0)
    rank = jnp.sum(csum * onehot, axis=1) - 1
    counts = csum[-1]
    padded = ((counts + tm - 1) // tm) * tm
    ends = jnp.cumsum(padded)
    dest = jnp.sum(onehot * (ends - padded)[None, :], axis=1) + rank
    n_rows = 2 * T + E * tm
    tile_start = jnp.arange(n_rows // tm, dtype=I32) * tm
    tile_expert = jnp.sum((ends[None, :] <= tile_start[:, None]).astype(I32), axis=1)
    last_used = jnp.max(jnp.where(counts > 0, jnp.arange(E, dtype=I32), 0))
    tile_expert = jnp.minimum(tile_expert, last_used)
    tile_new = jnp.concatenate([jnp.ones((1,), I32), (tile_expert[1:] != tile_expert[:-1]).astype(I32)])
    n_tiles_used = (ends[-1] // tm).reshape(1)
    return dest.astype(I32), n_rows, (tile_expert, tile_new, n_tiles_used.astype(I32))


def _forward(x, norm_mix_g, w_in, cmp_k_pos, cmp_k_w1, cmp_k_b1, cmp_k_w2, cmp_v_pos, cmp_v_w1, cmp_v_b1, cmp_v_w2,
             gla_w_alpha, gla_b_alpha, gla_norm_g, w_branch_nsa, w_branch_gla, w_out, norm_ffn_g,
             w_router_group, b_router_group, w_router_expert, b_router_expert, w_exp_gate, w_exp_up, w_exp_down,
             norm_final_g):
    B, S, D = x.shape
    T = B * S
    G, R, Dh = NSA_KV_HEADS, NSA_GROUP, NSA_HEAD_DIM
    NSA_Q, NSA_KV = NSA_HEADS * Dh, G * Dh
    GQK, GV = GLA_HEADS * GLA_DK, GLA_HEADS * GLA_DV
    h = x.reshape(T, D)
    assert w_in.shape[0] == 1, "the final norm is fused into the (single) layer's combine step"
    for l in range(1):
        w = w_in[l]
        o_ng = NSA_Q + 6 * NSA_KV
        o_gq = o_ng + 3 * NSA_HEADS
        o_ga = o_gq + 2 * GQK + 2 * GV
        o_ma = o_ga + GLA_GATE_RANK
        n_small = 3 * NSA_HEADS + GLA_GATE_RANK
        w_small = jnp.concatenate([w[:, o_ng:o_gq], w[:, o_ga:o_ma], jnp.zeros((D, LANES - n_small), F32)],
                                  axis=1).astype(BF16)
        c_kc = NSA_Q // LANES
        c_ks, c_vs, c_kw, c_vw = c_kc + 2 * G, c_kc + 3 * G, c_kc + 4 * G, c_kc + 5 * G
        c_gk = GQK // LANES
        c_gv = c_gk + GQK // LANES
        c_gr = c_gv + GV // LANES

        xn = rmsnorm(h, norm_mix_g[l])
        proj_nsa = matmul(xn, w[:, :o_ng].astype(BF16), BF16, PROJ_TM, PROJ_TN, "proj_nsa").reshape(B, S, -1)
        proj_gla = matmul(xn, w[:, o_gq:o_ga].astype(BF16), BF16, PROJ_TM, PROJ_TN, "proj_gla").reshape(B, S, -1)
        proj_mix = matmul(xn, w[:, o_ma:].astype(BF16), BF16, PROJ_TM, PROJ_TN, "proj_mix")
        small = matmul(xn, w_small, F32, PROJ_TM, LANES, "proj_small")

        pos2 = jnp.stack([cmp_k_pos[l], cmp_v_pos[l]]).reshape(2, 2, CMP_STRIDE * Dh)
        w1 = jnp.stack([cmp_k_w1[l], cmp_v_w1[l]]).astype(BF16)
        b1 = jnp.stack([cmp_k_b1[l], cmp_v_b1[l]]).reshape(2, 1, Dh)
        w2 = jnp.stack([cmp_k_w2[l], cmp_v_w2[l]]).astype(BF16)
        kvc = compress(proj_nsa, c_kc, pos2, w1, b1, w2)
        gates = small[:, :3 * NSA_HEADS].reshape(B, S, G, 3 * R).transpose(0, 2, 1, 3)
        slopes = jnp.exp2(-8.0 * jnp.arange(1, NSA_HEADS + 1, dtype=F32) / NSA_HEADS)
        o_nsa = nsa_attention(proj_nsa, kvc, gates, slopes, col_q=0, col_ks=c_ks, col_vs=c_vs,
                              col_kw=c_kw, col_vw=c_vw)

        xa = small[:, 3 * NSA_HEADS:n_small].reshape(B, S, GLA_GATE_RANK)
        o_gla = gla_attention(proj_gla, xa, gla_w_alpha[l], gla_b_alpha[l], gla_norm_g[l],
                              col_q=0, col_k=c_gk, col_v=c_gv, col_r=c_gr)

        mixed = gated_mix(o_nsa.reshape(T, -1), o_gla.reshape(T, -1), w_branch_nsa[l].astype(BF16),
                          w_branch_gla[l].astype(BF16), proj_mix, 0, D // LANES)
        h = out_proj(mixed, w_out[l].astype(BF16), h)

        n_r = N_GROUPS + N_EXPERTS
        w_r = jnp.concatenate([w_router_group[l], w_router_expert[l], jnp.zeros((D, LANES - n_r), F32)], axis=1)
        b_r = jnp.concatenate([b_router_group[l], b_router_expert[l], jnp.zeros((LANES - n_r,), F32)]).reshape(1, LANES)
        ids, wts, u = router(h, norm_ffn_g[l], w_r, b_r)
        dest, n_rows, tables = _dispatch_tables(ids[:, :2], MOE_TM)
        x_sorted = dispatch_rows(u, dest, n_rows)
        h_sorted = ffn_up(x_sorted, tables, w_exp_gate[l], w_exp_up[l])
        y_sorted = ffn_down(h_sorted, tables, w_exp_down[l])
        h = combine_norm(h, wts, dest, y_sorted, norm_final_g)
    return h.reshape(B, S, D)


def kernel(x, norm_mix_g, w_in, cmp_k_pos, cmp_k_w1, cmp_k_b1, cmp_k_w2, cmp_v_pos, cmp_v_w1, cmp_v_b1, cmp_v_w2, gla_w_alpha, gla_b_alpha, gla_norm_g, w_branch_nsa, w_branch_gla, w_out, norm_ffn_g, w_router_group, b_router_group, w_router_expert, b_router_expert, w_exp_gate, w_exp_up, w_exp_down, norm_final_g):
    return _forward(x, norm_mix_g, w_in, cmp_k_pos, cmp_k_w1, cmp_k_b1, cmp_k_w2, cmp_v_pos, cmp_v_w1, cmp_v_b1,
                    cmp_v_w2, gla_w_alpha, gla_b_alpha, gla_norm_g, w_branch_nsa, w_branch_gla, w_out, norm_ffn_g,
                    w_router_group, b_router_group, w_router_expert, b_router_expert, w_exp_gate, w_exp_up,
                    w_exp_down, norm_final_g)
```

```python
import functools

import jax
import jax.numpy as jnp
from jax import lax
from jax.experimental import pallas as pl
from jax.experimental.pallas import tpu as pltpu

F32 = jnp.float32
BF16 = jnp.bfloat16
I32 = jnp.int32

NSA_HEAD_DIM = 128
NSA_KV_HEADS = 4
NSA_GROUP = 4
NSA_HEADS = NSA_KV_HEADS * NSA_GROUP
CMP_BLOCK = 32
CMP_STRIDE = 16
SEL_BLOCK = 64
SEL_TOPN = 16
WINDOW = 512
BIG = 1e9
GLA_HEADS = 16
GLA_DK = 64
GLA_DV = 128
GLA_GATE_RANK = 16
GLA_TAU = 16.0
N_GROUPS = 4
EXPERTS_PER_GROUP = 8
N_EXPERTS = N_GROUPS * EXPERTS_PER_GROUP
NORM_EPS = 1e-6

LANES = 128
VMEM_LIMIT = 56 * 1024 * 1024
NEG = -1e30

NORM_ROWS = 512
WEIGHT_PREP_ROWS = 128
PROJ_TM, PROJ_TN = 2048, 512
NSA_TQ, NSA_TK = 512, 512
GLA_TS, GLA_C = 512, 64
MIX_TM, MIX_TN = 1024, 1024
OUT_TM, OUT_TN = 1024, 1024
ROUTE_TM = 256
MOE_TM = 512
MOE_CJ = 512
MOE_CN = 2048
DISPATCH_TOKENS = 512
COMBINE_ROWS = 256


def _cparams(*sem):
    return pltpu.CompilerParams(dimension_semantics=sem, vmem_limit_bytes=VMEM_LIMIT)


def _dot(a, b, **kw):
    return jnp.dot(a, b, preferred_element_type=F32, **kw)


def _dot_nt(a, b, **kw):
    return lax.dot_general(a, b, (((1,), (1,)), ((), ())), preferred_element_type=F32, **kw)


def _dot_tn(a, b, **kw):
    return lax.dot_general(a, b, (((0,), (0,)), ((), ())), preferred_element_type=F32, **kw)


def _sigmoid(x):
    return 1.0 / (1.0 + jnp.exp(-x))


def _masked_softmax(s, mask):
    sm = jnp.where(mask, s, NEG)
    m = jnp.max(sm, axis=-1, keepdims=True)
    e = jnp.where(mask, jnp.exp(sm - m), 0.0)
    return e * (1.0 / jnp.maximum(jnp.sum(e, axis=-1, keepdims=True), 1e-30))


def _rmsnorm_kernel(x_ref, g_ref, o_ref):
    x = x_ref[...]
    ms = jnp.mean(x * x, axis=-1, keepdims=True)
    o_ref[...] = (x * lax.rsqrt(ms + NORM_EPS) * g_ref[...]).astype(o_ref.dtype)


def rmsnorm(x, g, out_dtype=BF16, rows=NORM_ROWS):
    T, D = x.shape
    rows = min(rows, T)
    return pl.pallas_call(
        _rmsnorm_kernel,
        out_shape=jax.ShapeDtypeStruct((T, D), out_dtype),
        grid=(T // rows,),
        in_specs=[pl.BlockSpec((rows, D), lambda i: (i, 0)),
                  pl.BlockSpec((1, D), lambda i: (0, 0))],
        out_specs=pl.BlockSpec((rows, D), lambda i: (i, 0)),
        compiler_params=_cparams("parallel"),
        name="rmsnorm",
    )(x, g.reshape(1, D))


def _mm_kernel(a_ref, b_ref, o_ref):
    o_ref[...] = _dot(a_ref[...], b_ref[...]).astype(o_ref.dtype)


def matmul(a, b, out_dtype, tm, tn, name):
    M, K = a.shape
    N = b.shape[1]
    tm, tn = min(tm, M), min(tn, N)
    return pl.pallas_call(
        _mm_kernel,
        out_shape=jax.ShapeDtypeStruct((M, N), out_dtype),
        grid=(M // tm, N // tn),
        in_specs=[pl.BlockSpec((tm, K), lambda i, j: (i, 0)),
                  pl.BlockSpec((K, tn), lambda i, j: (0, j))],
        out_specs=pl.BlockSpec((tm, tn), lambda i, j: (i, j)),
        compiler_params=_cparams("parallel", "arbitrary"),
        name=name,
    )(a, b)


def _compress_kernel(seq_ref, pos_ref, w1_ref, b1_ref, w2_ref, o_ref):
    seq = seq_ref[0]
    nc = seq.shape[0] // CMP_STRIDE
    x = seq.reshape(nc, CMP_STRIDE * seq.shape[1]).astype(F32)
    half = x.shape[1]
    pos = pos_ref[0]
    w1 = w1_ref[0]
    u0 = _dot((x + pos[0:1, :]).astype(BF16), w1[:half, :])
    u1 = _dot((x + pos[1:2, :]).astype(BF16), w1[half:, :])
    pre = u0 + pltpu.roll(u1, nc - 1, 0) + b1_ref[0]
    h = 0.5 * pre * (1.0 + jnp.tanh(0.7978845608028654 * (pre + 0.044715 * pre * pre * pre)))
    o_ref[0, 0, 0] = _dot(h.astype(BF16), w2_ref[0]).astype(o_ref.dtype)


def compress(proj, col_kc, pos2, w1, b1, w2):
    B, S, _ = proj.shape
    G = NSA_KV_HEADS
    Dh = w2.shape[-1]
    NC, HW = S // CMP_STRIDE, CMP_STRIDE * Dh
    return pl.pallas_call(
        _compress_kernel,
        out_shape=jax.ShapeDtypeStruct((2, B, G, NC, Dh), BF16),
        grid=(2, B, G),
        in_specs=[pl.BlockSpec((1, S, Dh), lambda a, b, g: (b, 0, col_kc + a * G + g)),
                  pl.BlockSpec((1, 2, HW), lambda a, b, g: (a, 0, 0)),
                  pl.BlockSpec((1, 2 * HW, Dh), lambda a, b, g: (a, 0, 0)),
                  pl.BlockSpec((1, 1, Dh), lambda a, b, g: (a, 0, 0)),
                  pl.BlockSpec((1, Dh, Dh), lambda a, b, g: (a, 0, 0))],
        out_specs=pl.BlockSpec((1, 1, 1, NC, Dh), lambda a, b, g: (a, b, g, 0, 0)),
        compiler_params=_cparams("parallel", "parallel", "parallel"),
        name="nsa_compress",
    )(proj, pos2, w1, b1, w2)


MASK_BIG = 2.0 ** 100


def _nsa_kernel(slopes_ref, q_ref, kc_ref, vc_ref, ks_ref, vs_ref, kw_ref, vw_ref, g_ref,
                paux_ref, saux_ref, caux_ref, mmat_ref, o_ref,
                qx_ref, sc_ref, sw_ref, sa_ref, sb_ref, score_ref, m_ref, acc_ref,
                *, TQ, TK, NC, NSEL, NTOP):
    R, Dh = NSA_GROUP, NSA_HEAD_DIM
    g = pl.program_id(1)
    q0 = pl.program_id(2) * TQ
    scale = Dh ** -0.5
    lane = lax.broadcasted_iota(I32, (1, LANES), 1)
    t1 = q0 + lax.broadcasted_iota(I32, (TQ, 1), 0)
    head = lambda r: slice(r * TQ, (r + 1) * TQ)

    def alibi_cols(r):
        hi, lo = slopes_ref[0, g * R + r], slopes_ref[1, g * R + r]
        c = jnp.where(lane == 0, 64.0 * hi, jnp.where(lane == 1, 64.0 * lo,
                                                      jnp.where(lane == 2, hi, jnp.where(lane == 3, lo, 0.0))))
        return jnp.broadcast_to(c, (TQ, LANES))

    for r in range(R):
        qx_ref[head(r), 0:Dh] = (q_ref[0, :, r * Dh:(r + 1) * Dh].astype(F32) * scale).astype(BF16)
        qx_ref[head(r), Dh:Dh + LANES] = alibi_cols(r).astype(BF16)

    def with_ones(v):
        return jnp.concatenate([v, jnp.ones(v.shape, v.dtype)], axis=1)

    def exp_pv(s, v1):
        m = jnp.broadcast_to(jnp.max(s, axis=-1, keepdims=True), (TQ, LANES))
        e = [jnp.exp(s[:, c * LANES:(c + 1) * LANES] - m) for c in range(s.shape[1] // LANES)]
        return e, _dot(jnp.concatenate(e, axis=1).astype(BF16), v1)

    WK = WINDOW + TQ
    ws = pl.multiple_of(jnp.maximum(q0 - WINDOW, 0), TQ)
    sc_ref[...] = _dot_nt(qx_ref[...], jnp.concatenate([kc_ref[0, 0, 0], caux_ref[...]], axis=1))
    sw_ref[...] = _dot_nt(qx_ref[...], jnp.concatenate([kw_ref[0, pl.ds(ws, WK), :], paux_ref[pl.ds(ws, WK), :]],
                                                       axis=1))

    cmp_end = lax.broadcasted_iota(I32, (1, NC), 1) * CMP_STRIDE + (CMP_BLOCK - 1)
    cmp_bias = jnp.where(cmp_end <= t1, 0.0, NEG)
    row_ok = q0 + lax.broadcasted_iota(I32, (TQ, LANES), 0) >= CMP_BLOCK - 1
    vc1 = with_ones(vc_ref[0, 0, 0])
    o_cmp = []
    imp = None
    for r in range(R):
        e, o2 = exp_pv(sc_ref[head(r), :] + cmp_bias, vc1)
        inv = jnp.where(row_ok, 1.0 / jnp.maximum(o2[:, Dh:], 1e-30), 0.0)
        o_cmp.append(o2[:, :Dh] * inv)
        p = jnp.concatenate([ec * inv for ec in e], axis=1)
        imp = p if imp is None else imp + p

    vw1 = with_ones(vw_ref[0, pl.ds(ws, WK), :])
    dw = t1 - (ws + lax.broadcasted_iota(I32, (1, WK), 1))
    win_bias = jnp.where((dw >= 0) & (dw < WINDOW), 0.0, NEG)
    o_win = []
    for r in range(R):
        _, o2 = exp_pv(sw_ref[head(r), :] + win_bias, vw1)
        o_win.append(o2[:, :Dh] * (1.0 / jnp.maximum(o2[:, Dh:], 1e-30)))

    mm = mmat_ref[...]
    i1 = imp.astype(BF16)
    rem = imp - i1.astype(F32)
    i2 = rem.astype(BF16)
    i3 = (rem - i2.astype(F32)).astype(BF16)
    imp_sel = _dot_nt(mm, i1) + _dot_nt(mm, i2) + _dot_nt(mm, i3)

    cur = (q0 + lax.broadcasted_iota(I32, (1, TQ), 1)) // SEL_BLOCK
    blk = lax.broadcasted_iota(I32, (NSEL, 1), 0)
    forced = (blk == 0) | (blk == cur) | (blk == cur - 1)
    score = jnp.where(blk <= cur, jnp.where(forced, BIG, imp_sel), -BIG)
    score_ref[...] = score

    def rank_pair(i2, cnt):
        for d in range(2):
            ii = 2 * i2 + d
            other = score_ref[pl.ds(ii, 1), :]
            beats = (other > score) | ((other == score) & (blk > ii))
            cnt = cnt + jnp.where(beats, 1.0, 0.0)
        return cnt

    n_blk = (q0 + TQ) // SEL_BLOCK
    cnt = lax.fori_loop(0, n_blk // 2, rank_pair, jnp.zeros((NSEL, TQ), F32))
    unsel = jnp.where(cnt < NTOP, 0.0, -1.0).astype(BF16)
    place = jnp.where(lax.broadcasted_iota(I32, (NSEL, LANES), 1)
                      == lax.broadcasted_iota(I32, (NSEL, LANES), 0) + LANES // 2, 1.0, 0.0).astype(BF16)
    unsel_l = _dot_tn(unsel, place)
    for r in range(R):
        qx_ref[head(r), Dh:Dh + LANES] = (alibi_cols(r) + unsel_l).astype(BF16)

    m_ref[...] = jnp.full(m_ref.shape, NEG, F32)
    acc_ref[...] = jnp.zeros(acc_ref.shape, F32)
    pos_in_tile = lax.broadcasted_iota(I32, (1, TK), 1)

    def issue_scores(kt, s_ref):
        k0 = pl.multiple_of(kt * TK, TK)
        kk = jnp.concatenate([ks_ref[0, pl.ds(k0, TK), :], saux_ref[pl.ds(k0, TK), :]], axis=1)
        s_ref[...] = _dot_nt(qx_ref[...], kk)

    def consume_scores(kt, s_ref, causal):
        k0 = pl.multiple_of(kt * TK, TK)
        vv1 = with_ones(vs_ref[0, pl.ds(k0, TK), :])
        if causal:
            causal_bias = jnp.where(t1 >= k0 + pos_in_tile, 0.0, -MASK_BIG)
        for r in range(R):
            sc = s_ref[head(r), :]
            if causal:
                sc = sc + causal_bias
            m_old = m_ref[r]
            m_new = jnp.maximum(m_old, jnp.broadcast_to(jnp.max(sc, axis=-1, keepdims=True), (TQ, LANES)))
            alpha = jnp.exp(m_old - m_new)
            p = jnp.concatenate([jnp.exp(sc[:, c * LANES:(c + 1) * LANES] - m_new) for c in range(TK // LANES)],
                                axis=1).astype(BF16)
            m_ref[r] = m_new
            acc_ref[r] = jnp.concatenate([alpha, alpha], axis=1) * acc_ref[r] + _dot(p, vv1)

    def sel_pair(k, carry):
        issue_scores(2 * k + 1, sb_ref)
        consume_scores(2 * k, sa_ref, False)
        issue_scores(2 * k + 2, sa_ref)
        consume_scores(2 * k + 1, sb_ref, False)
        return carry

    n_below = q0 // TK
    issue_scores(0, sa_ref)
    lax.fori_loop(0, n_below // 2, sel_pair, 0)

    @pl.when(n_below % 2 == 1)
    def _():
        issue_scores(n_below, sb_ref)
        consume_scores(n_below - 1, sa_ref, False)
        consume_scores(n_below, sb_ref, True)

    @pl.when(n_below % 2 == 0)
    def _():
        consume_scores(n_below, sa_ref, True)

    gt = _sigmoid(g_ref[0, 0])
    for r in range(R):
        o_sel = acc_ref[r, :, 0:Dh] * (1.0 / jnp.maximum(acc_ref[r, :, Dh:2 * Dh], 1e-30))
        o = (gt[:, 3 * r:3 * r + 1] * o_cmp[r] + gt[:, 3 * r + 1:3 * r + 2] * o_sel
             + gt[:, 3 * r + 2:3 * r + 3] * o_win[r])
        o_ref[0, :, r * Dh:(r + 1) * Dh] = o.astype(o_ref.dtype)


def _nsa_constants(S, NC, NSEL):
    def pos_cols(pos):
        col = jnp.arange(LANES)[None, :]
        hi, lo = (pos // 64)[:, None], (pos % 64)[:, None]
        return jnp.where(col < 2, hi, jnp.where(col < 4, lo, 0)).astype(F32)

    pos = jnp.arange(S)
    paux = pos_cols(pos)
    onehot = (jnp.arange(LANES)[None, :] - LANES // 2 == (pos // SEL_BLOCK)[:, None]) & (jnp.arange(LANES)[None, :] >= LANES // 2)
    saux = paux + jnp.where(onehot, MASK_BIG, 0.0)
    caux = pos_cols(jnp.arange(NC) * CMP_STRIDE + (CMP_BLOCK - 1))
    d = jnp.arange(NC)[None, :] - 4 * jnp.arange(NSEL)[:, None]
    mmat = jnp.where((d == -1) | (d == 3), 1.0, jnp.where((d >= 0) & (d <= 2), 2.0, 0.0))
    return paux.astype(BF16), saux.astype(BF16), caux.astype(BF16), mmat.astype(BF16)


def nsa_attention(proj, kvc, gates, slopes, *, col_q, col_ks, col_vs, col_kw, col_vw):
    B, S, _ = proj.shape
    G, R, Dh = NSA_KV_HEADS, NSA_GROUP, NSA_HEAD_DIM
    NC = kvc.shape[3]
    NSEL = S // SEL_BLOCK
    TQ, TK = min(NSA_TQ, S), min(NSA_TK, S)
    assert S % TQ == 0 and TK % TQ == 0 and S % TK == 0 and S >= WINDOW + TQ and NC == S // CMP_STRIDE
    assert NSEL <= LANES // 2 and S // 64 <= 256 and TQ % (2 * SEL_BLOCK) == 0
    s_hi = slopes.astype(BF16).astype(F32)
    s_lo = (slopes - s_hi).astype(BF16).astype(F32)
    paux, saux, caux, mmat = _nsa_constants(S, NC, NSEL)
    kern = functools.partial(_nsa_kernel, TQ=TQ, TK=TK, NC=NC, NSEL=NSEL, NTOP=min(SEL_TOPN, NSEL))
    kv_spec = lambda col: pl.BlockSpec((1, S, Dh), lambda b, g, i: (b, 0, col + g))
    whole = lambda a: pl.BlockSpec(a.shape, lambda b, g, i: (0, 0))
    return pl.pallas_call(
        kern,
        out_shape=jax.ShapeDtypeStruct((B, S, G * R * Dh), BF16),
        grid=(B, G, S // TQ),
        in_specs=[pl.BlockSpec(memory_space=pltpu.SMEM),
                  pl.BlockSpec((1, TQ, R * Dh), lambda b, g, i: (b, i, col_q // R + g)),
                  pl.BlockSpec((1, 1, 1, NC, Dh), lambda b, g, i: (0, b, g, 0, 0)),
                  pl.BlockSpec((1, 1, 1, NC, Dh), lambda b, g, i: (1, b, g, 0, 0)),
                  kv_spec(col_ks), kv_spec(col_vs), kv_spec(col_kw), kv_spec(col_vw),
                  pl.BlockSpec((1, 1, TQ, gates.shape[-1]), lambda b, g, i: (b, g, i, 0)),
                  whole(paux), whole(saux), whole(caux), whole(mmat)],
        out_specs=pl.BlockSpec((1, TQ, R * Dh), lambda b, g, i: (b, i, g)),
        scratch_shapes=[pltpu.VMEM((R * TQ, Dh + LANES), BF16), pltpu.VMEM((R * TQ, NC), F32),
                        pltpu.VMEM((R * TQ, WINDOW + TQ), F32), pltpu.VMEM((R * TQ, TK), F32),
                        pltpu.VMEM((R * TQ, TK), F32),
                        pltpu.VMEM((NSEL, TQ), F32), pltpu.VMEM((R, TQ, LANES), F32),
                        pltpu.VMEM((R, TQ, 2 * Dh), F32)],
        compiler_params=_cparams("parallel", "parallel", "arbitrary"),
        name="nsa_attention",
    )(jnp.stack([s_hi, s_lo]), proj, kvc, kvc, proj, proj, proj, proj, gates, paux, saux, caux, mmat)


def _gla_kernel(q_ref, k_ref, v_ref, r_ref, xa_ref, wa_ref, ba_ref, ng_ref, o_ref,
                st_ref, b_ref, q2_ref, kh_ref, qe_ref, kd_ref, oacc_ref, *, TS, C):
    DK2, DV2 = 2 * GLA_DK, 2 * GLA_DV
    NCH = TS // C
    LEVELS = [C >> (i + 1) for i in range(C.bit_length() - 1)]

    @pl.when(pl.program_id(2) == 0)
    def _():
        st_ref[...] = jnp.zeros_like(st_ref)

    xa = xa_ref[0]
    xa_hi = xa.astype(BF16)
    xa_lo = (xa - xa_hi.astype(F32)).astype(BF16)
    z = _dot(xa_hi, wa_ref[0]) + _dot(xa_lo, wa_ref[0]) + _dot(xa_hi, wa_ref[1]) + ba_ref[...]
    b = (jnp.minimum(z, 0.0) - jnp.log(1.0 + jnp.exp(-jnp.abs(z)))) * (1.0 / GLA_TAU)
    row = lax.broadcasted_iota(I32, (TS, 1), 0)
    step = 1
    while step < C:
        b = b + jnp.where(row % C >= step, pltpu.roll(b, step, 0), 0.0)
        step *= 2
    b_ref[...] = b

    head_a = lax.broadcasted_iota(I32, (1, DK2), 1) < GLA_DK
    q = q_ref[0].astype(F32) * (GLA_DK ** -0.5)
    k = k_ref[0].astype(F32)

    def ref_rows(h):
        if h >= 4:
            g = TS // (2 * h)
            return jnp.broadcast_to(b.reshape(g, 2 * h, DK2)[:, h - 1:h, :], (g, 2 * h, DK2)).reshape(TS, DK2)
        dn1 = pltpu.roll(b, 1, 0)
        if h == 1:
            return jnp.where(row % 2 == 0, b, dn1)
        m4 = row % 4
        return jnp.where(m4 == 0, pltpu.roll(b, TS - 1, 0),
                         jnp.where(m4 == 1, b, jnp.where(m4 == 2, dn1, pltpu.roll(b, 2, 0))))

    def put(lvl, qh, kh):
        q2_ref[lvl, :, 0:C, :] = jnp.where(head_a, qh, 0.0).astype(BF16).reshape(NCH, C, DK2)
        q2_ref[lvl, :, C:2 * C, :] = jnp.where(head_a, 0.0, qh).astype(BF16).reshape(NCH, C, DK2)
        kh_ref[lvl] = kh.astype(BF16)

    put(0, q, k)
    for lvl, h in enumerate(LEVELS, start=1):
        r = ref_rows(h)
        put(lvl, q * jnp.exp(jnp.minimum(b - r, 0.0)), k * jnp.exp(jnp.minimum(r - b, 0.0)))
    b_last = jnp.broadcast_to(b.reshape(NCH, C, DK2)[:, C - 1:C, :], (NCH, C, DK2)).reshape(TS, DK2)
    qe_ref[...] = (q * jnp.exp(b)).astype(BF16)
    kd_ref[...] = (k * jnp.exp(b_last - b)).astype(BF16)

    row2 = lax.broadcasted_iota(I32, (2 * C, 1), 0) % C
    col = lax.broadcasted_iota(I32, (1, C), 1)
    masks = [row2 == col] + [(row2 // (2 * h) == col // (2 * h)) & (row2 % (2 * h) >= h) & (col % (2 * h) < h)
                             for h in LEVELS]
    vk_same = (lax.broadcasted_iota(I32, (DV2, DK2), 0) < GLA_DV) == (lax.broadcasted_iota(I32, (DV2, DK2), 1) < GLA_DK)
    for c in range(NCH):
        rows = slice(c * C, (c + 1) * C)
        vb = v_ref[0, rows, :]
        a2 = jnp.zeros((2 * C, C), F32)
        for lvl, mask in enumerate(masks):
            a2 = a2 + jnp.where(mask, _dot_nt(q2_ref[lvl, c], kh_ref[lvl, rows, :]), 0.0)
        a2 = a2.astype(BF16)
        o = jnp.concatenate([_dot(a2[0:C], vb[:, :GLA_DV]), _dot(a2[C:2 * C], vb[:, GLA_DV:])], axis=1)
        st = st_ref[...]
        oacc_ref[rows, :] = o + _dot_nt(qe_ref[rows, :], st.astype(BF16))
        ds = jnp.where(vk_same, _dot_tn(vb, kd_ref[rows, :]), 0.0)
        st_ref[...] = st * jnp.exp(b_ref[(c + 1) * C - 1:(c + 1) * C, :]) + ds

    def ln(x):
        mu = jnp.mean(x, axis=-1, keepdims=True)
        xc = x - mu
        return xc * lax.rsqrt(jnp.mean(xc * xc, axis=-1, keepdims=True) + NORM_EPS)

    o = oacc_ref[...]
    on = jnp.concatenate([ln(o[:, :GLA_DV]), ln(o[:, GLA_DV:])], axis=1) * ng_ref[...]
    rr = r_ref[0].astype(F32)
    o_ref[0] = (on * (rr * _sigmoid(rr))).astype(o_ref.dtype)


def gla_attention(proj, xa, w_alpha, b_alpha, norm_g, *, col_q, col_k, col_v, col_r):
    B, S, _ = proj.shape
    HP = GLA_HEADS // 2
    DK2, DV2 = 2 * GLA_DK, 2 * GLA_DV
    TS, C = min(GLA_TS, S), GLA_C
    assert S % TS == 0 and TS % C == 0 and C & (C - 1) == 0 and C >= 16
    n_lvl = C.bit_length()
    wa_hi = w_alpha.astype(BF16)
    wa_split = jnp.stack([wa_hi, (w_alpha - wa_hi.astype(F32)).astype(BF16)])
    kern = functools.partial(_gla_kernel, TS=TS, C=C)
    return pl.pallas_call(
        kern,
        out_shape=jax.ShapeDtypeStruct((B, S, GLA_HEADS * GLA_DV), BF16),
        grid=(B, HP, S // TS),
        in_specs=[pl.BlockSpec((1, TS, DK2), lambda b, p, s: (b, s, col_q + p)),
                  pl.BlockSpec((1, TS, DK2), lambda b, p, s: (b, s, col_k + p)),
                  pl.BlockSpec((1, TS, DV2), lambda b, p, s: (b, s, col_v // 2 + p)),
                  pl.BlockSpec((1, TS, DV2), lambda b, p, s: (b, s, col_r // 2 + p)),
                  pl.BlockSpec((1, TS, GLA_GATE_RANK), lambda b, p, s: (b, s, 0)),
                  pl.BlockSpec((2, GLA_GATE_RANK, DK2), lambda b, p, s: (0, 0, p)),
                  pl.BlockSpec((1, DK2), lambda b, p, s: (0, p)),
                  pl.BlockSpec((1, DV2), lambda b, p, s: (0, p))],
        out_specs=pl.BlockSpec((1, TS, DV2), lambda b, p, s: (b, s, p)),
        scratch_shapes=[pltpu.VMEM((DV2, DK2), F32), pltpu.VMEM((TS, DK2), F32),
                        pltpu.VMEM((n_lvl, TS // C, 2 * C, DK2), BF16), pltpu.VMEM((n_lvl, TS, DK2), BF16),
                        pltpu.VMEM((TS, DK2), BF16), pltpu.VMEM((TS, DK2), BF16), pltpu.VMEM((TS, DV2), F32)],
        compiler_params=_cparams("parallel", "parallel", "arbitrary"),
        name="gla_attention",
    )(proj, proj, proj, proj, xa, wa_split, b_alpha.reshape(1, -1), norm_g.reshape(1, -1))


def _mix_kernel(oa_ref, ob_ref, wa_ref, wb_ref, ma_ref, mb_ref, o_ref):
    ya = _dot(oa_ref[...], wa_ref[...])
    yb = _dot(ob_ref[...], wb_ref[...])
    o_ref[...] = (_sigmoid(ma_ref[...].astype(F32)) * ya + _sigmoid(mb_ref[...].astype(F32)) * yb).astype(o_ref.dtype)


def gated_mix(o_nsa, o_gla, wa, wb, proj2d, col_ma, col_mb):
    T, KA = o_nsa.shape
    KB = o_gla.shape[1]
    N = wa.shape[1]
    tm, tn = min(MIX_TM, T), min(MIX_TN, N)
    ca, cb = col_ma * LANES // tn, col_mb * LANES // tn
    assert (col_ma * LANES) % tn == 0 and (col_mb * LANES) % tn == 0
    return pl.pallas_call(
        _mix_kernel,
        out_shape=jax.ShapeDtypeStruct((T, N), BF16),
        grid=(T // tm, N // tn),
        in_specs=[pl.BlockSpec((tm, KA), lambda i, j: (i, 0)),
                  pl.BlockSpec((tm, KB), lambda i, j: (i, 0)),
                  pl.BlockSpec((KA, tn), lambda i, j: (0, j)),
                  pl.BlockSpec((KB, tn), lambda i, j: (0, j)),
                  pl.BlockSpec((tm, tn), lambda i, j: (i, ca + j)),
                  pl.BlockSpec((tm, tn), lambda i, j: (i, cb + j))],
        out_specs=pl.BlockSpec((tm, tn), lambda i, j: (i, j)),
        compiler_params=_cparams("parallel", "arbitrary"),
        name="gated_mix",
    )(o_nsa, o_gla, wa, wb, proj2d, proj2d)


def _out_kernel(a_ref, w_ref, x_ref, o_ref):
    o_ref[...] = x_ref[...] + _dot(a_ref[...], w_ref[...])


def out_proj(mixed, w, x):
    T, K = mixed.shape
    N = w.shape[1]
    tm, tn = min(OUT_TM, T), min(OUT_TN, N)
    return pl.pallas_call(
        _out_kernel,
        out_shape=jax.ShapeDtypeStruct((T, N), F32),
        grid=(T // tm, N // tn),
        in_specs=[pl.BlockSpec((tm, K), lambda i, j: (i, 0)),
                  pl.BlockSpec((K, tn), lambda i, j: (0, j)),
                  pl.BlockSpec((tm, tn), lambda i, j: (i, j))],
        out_specs=pl.BlockSpec((tm, tn), lambda i, j: (i, j)),
        compiler_params=_cparams("parallel", "arbitrary"),
        name="out_proj",
    )(mixed, w, x)


def _router_kernel(h_ref, g_ref, w_ref, b_ref, id_ref, wt_ref, u_ref):
    x = h_ref[...]
    u = x * lax.rsqrt(jnp.mean(x * x, axis=-1, keepdims=True) + NORM_EPS) * g_ref[...]
    u_hi = u.astype(BF16)
    u_ref[...] = u_hi.reshape(u_ref.shape)
    u_lo = (u - u_hi.astype(F32)).astype(BF16)
    logit = _dot(u_hi, w_ref[0]) + _dot(u_lo, w_ref[0]) + _dot(u_hi, w_ref[1]) + b_ref[...]
    lane = lax.broadcasted_iota(I32, logit.shape, 1)
    big = jnp.int32(1 << 20)
    gmask = lane < N_GROUPS
    gl = jnp.where(gmask, logit, NEG)
    gmax = jnp.max(gl, axis=-1, keepdims=True)
    g_star = jnp.min(jnp.where(gmask & (gl == gmax), lane, big), axis=-1, keepdims=True)
    p_group = 1.0 / jnp.sum(jnp.where(gmask, jnp.exp(gl - gmax), 0.0), axis=-1, keepdims=True)
    e_lo = N_GROUPS + EXPERTS_PER_GROUP * g_star
    emask = (lane >= e_lo) & (lane < e_lo + EXPERTS_PER_GROUP)
    el = jnp.where(emask, logit, NEG)
    m1 = jnp.max(el, axis=-1, keepdims=True)
    i1 = jnp.min(jnp.where(emask & (el == m1), lane, big), axis=-1, keepdims=True)
    emask2 = emask & (lane != i1)
    el2 = jnp.where(emask2, logit, NEG)
    m2 = jnp.max(el2, axis=-1, keepdims=True)
    i2 = jnp.min(jnp.where(emask2 & (el2 == m2), lane, big), axis=-1, keepdims=True)
    e2 = jnp.exp(m2 - m1)
    w1 = p_group / (1.0 + e2)
    w2 = p_group * e2 / (1.0 + e2)
    id_ref[...] = jnp.where(lane == 0, i1 - N_GROUPS, jnp.where(lane == 1, i2 - N_GROUPS, 0))
    wt_ref[...] = jnp.where(lane == 0, w1, jnp.where(lane == 1, w2, 0.0))


def router(h, g, w_r, b_r):
    T, D = h.shape
    tm = min(ROUTE_TM, T)
    w_hi = w_r.astype(BF16)
    return pl.pallas_call(
        _router_kernel,
        out_shape=(jax.ShapeDtypeStruct((T, LANES), I32), jax.ShapeDtypeStruct((T, LANES), F32),
                   jax.ShapeDtypeStruct((T, D // LANES, LANES), BF16)),
        grid=(T // tm,),
        in_specs=[pl.BlockSpec((tm, D), lambda i: (i, 0)),
                  pl.BlockSpec((1, D), lambda i: (0, 0)),
                  pl.BlockSpec((2, D, LANES), lambda i: (0, 0, 0)),
                  pl.BlockSpec((1, LANES), lambda i: (0, 0))],
        out_specs=(pl.BlockSpec((tm, LANES), lambda i: (i, 0)), pl.BlockSpec((tm, LANES), lambda i: (i, 0)),
                   pl.BlockSpec((tm, D // LANES, LANES), lambda i: (i, 0, 0))),
        compiler_params=_cparams("parallel"),
        name="moe_router",
    )(h, g.reshape(1, D), jnp.stack([w_hi, (w_r - w_hi.astype(F32)).astype(BF16)]), b_r)


def _row_copy(src_ref, dst_ref, sem, src_row, dst_row):
    return pltpu.make_async_copy(src_ref.at[pl.ds(src_row, 1)], dst_ref.at[pl.ds(dst_row, 1)], sem)


def _dispatch_rows_kernel(dest_ref, u_ref, init_hbm, out_hbm, sem, *, toks):
    def start(t, c):
        for k in range(2):
            _row_copy(u_ref, out_hbm, sem, t, dest_ref[0, 0, 2 * t + k]).start(priority=k)
        return c

    def wait(t, c):
        for k in range(2):
            _row_copy(u_ref, out_hbm, sem, t, 0).wait()
        return c

    lax.fori_loop(0, toks, start, 0)
    lax.fori_loop(0, toks, wait, 0)


def dispatch_rows(u3, dest, n_rows):
    T = u3.shape[0]
    toks = min(DISPATCH_TOKENS, T)
    assert T % toks == 0
    out_shape = jax.ShapeDtypeStruct((n_rows,) + u3.shape[1:], u3.dtype)
    return pl.pallas_call(
        functools.partial(_dispatch_rows_kernel, toks=toks),
        out_shape=out_shape,
        grid=(T // toks,),
        in_specs=[pl.BlockSpec((1, 1, 2 * toks), lambda i: (i, 0, 0), memory_space=pltpu.SMEM),
                  pl.BlockSpec((toks,) + u3.shape[1:], lambda i: (i, 0, 0)),
                  pl.BlockSpec(memory_space=pl.ANY)],
        out_specs=pl.BlockSpec(memory_space=pl.ANY),
        scratch_shapes=[pltpu.SemaphoreType.DMA],
        input_output_aliases={2: 0},
        compiler_params=_cparams("arbitrary"),
        name="moe_dispatch_rows",
    )(dest.reshape(T // toks, 1, 2 * toks), u3, jnp.zeros(out_shape.shape, out_shape.dtype))


def _ffn_up_kernel(te_ref, new_ref, nv_ref, x_ref, wg_ref, wu_ref, o_ref, wg_bf, wu_bf):
    i = pl.program_id(1)
    used = i < nv_ref[0]
    fresh = new_ref[i] == 1

    def swiglu():
        x = x_ref[...].reshape(x_ref.shape[0], wg_bf.shape[0])
        a = _dot(x, wg_bf[...])
        o_ref[...] = (a * _sigmoid(a) * _dot(x, wu_bf[...])).astype(o_ref.dtype)

    @pl.when(fresh)
    def _():
        wg_bf[...] = wg_ref[0].astype(BF16)
        wu_bf[...] = wu_ref[0].astype(BF16)
        swiglu()

    pl.when(jnp.logical_and(used, jnp.logical_not(fresh)))(swiglu)

    @pl.when(jnp.logical_not(used))
    def _():
        o_ref[...] = jnp.zeros_like(o_ref)


def ffn_up(x_sorted, tables, w_gate, w_up):
    NP = x_sorted.shape[0]
    E, D, DE = w_gate.shape
    tm, cj = MOE_TM, min(MOE_CJ, DE)
    return pl.pallas_call(
        _ffn_up_kernel,
        out_shape=jax.ShapeDtypeStruct((NP, DE), BF16),
        grid_spec=pltpu.PrefetchScalarGridSpec(
            num_scalar_prefetch=3,
            grid=(DE // cj, NP // tm),
            in_specs=[pl.BlockSpec((tm,) + x_sorted.shape[1:], lambda j, i, te, nw, nv: (i, 0, 0)),
                      pl.BlockSpec((1, D, cj), lambda j, i, te, nw, nv: (te[i], 0, j)),
                      pl.BlockSpec((1, D, cj), lambda j, i, te, nw, nv: (te[i], 0, j))],
            out_specs=pl.BlockSpec((tm, cj), lambda j, i, te, nw, nv: (i, j)),
            scratch_shapes=[pltpu.VMEM((D, cj), BF16), pltpu.VMEM((D, cj), BF16)]),
        compiler_params=_cparams("arbitrary", "arbitrary"),
        name="moe_ffn_up",
    )(*tables, x_sorted, w_gate, w_up)


def _ffn_down_kernel(te_ref, new_ref, nv_ref, h_ref, wd_ref, o_ref, wd_bf):
    i = pl.program_id(1)
    used = i < nv_ref[0]
    fresh = new_ref[i] == 1

    def down():
        o_ref[...] = _dot(h_ref[...], wd_bf[...]).astype(o_ref.dtype).reshape(o_ref.shape)

    @pl.when(fresh)
    def _():
        wd_bf[...] = wd_ref[0].astype(BF16)
        down()

    pl.when(jnp.logical_and(used, jnp.logical_not(fresh)))(down)

    @pl.when(jnp.logical_not(used))
    def _():
        o_ref[...] = jnp.zeros_like(o_ref)


def ffn_down(h_sorted, tables, w_down):
    NP, DE = h_sorted.shape
    E, _, D = w_down.shape
    tm, cn = MOE_TM, min(MOE_CN, D)
    return pl.pallas_call(
        _ffn_down_kernel,
        out_shape=jax.ShapeDtypeStruct((NP, D // LANES, LANES), BF16),
        grid_spec=pltpu.PrefetchScalarGridSpec(
            num_scalar_prefetch=3,
            grid=(D // cn, NP // tm),
            in_specs=[pl.BlockSpec((tm, DE), lambda j, i, te, nw, nv: (i, 0)),
                      pl.BlockSpec((1, DE, cn), lambda j, i, te, nw, nv: (te[i], 0, j))],
            out_specs=pl.BlockSpec((tm, cn // LANES, LANES), lambda j, i, te, nw, nv: (i, j, 0)),
            scratch_shapes=[pltpu.VMEM((DE, cn), BF16)]),
        compiler_params=_cparams("arbitrary", "arbitrary"),
        name="moe_ffn_down",
    )(*tables, h_sorted, w_down)


def _combine_kernel(idx_ref, nxt_ref, h_ref, wt_ref, y_hbm, g_ref, o_ref, buf, sem, *, rows):
    i = pl.program_id(0)
    slot = i % 2

    def issue(table, s):
        def body(r, c):
            _row_copy(y_hbm, buf.at[s, 0], sem.at[s], table[0, 0, 2 * r], r).start(priority=0)
            _row_copy(y_hbm, buf.at[s, 1], sem.at[s], table[0, 0, 2 * r + 1], r).start(priority=1)
            return c
        lax.fori_loop(0, rows, body, 0)

    @pl.when(i == 0)
    def _():
        issue(idx_ref, 0)

    @pl.when(i + 1 < pl.num_programs(0))
    def _():
        issue(nxt_ref, 1 - slot)

    def wait(r, c):
        _row_copy(y_hbm, buf.at[slot, 0], sem.at[slot], 0, r).wait()
        _row_copy(y_hbm, buf.at[slot, 1], sem.at[slot], 0, r).wait()
        return c

    lax.fori_loop(0, rows, wait, 0)
    wt = wt_ref[...]
    y0 = buf[slot, 0].reshape(h_ref.shape).astype(F32)
    y1 = buf[slot, 1].reshape(h_ref.shape).astype(F32)
    x = h_ref[...] + wt[:, 0:1] * y0 + wt[:, 1:2] * y1
    o_ref[...] = x * lax.rsqrt(jnp.mean(x * x, axis=-1, keepdims=True) + NORM_EPS) * g_ref[...]


def combine_norm(h, wts, dest, y3, g):
    T, D = h.shape
    rows = min(COMBINE_ROWS, T)
    n = T // rows
    table = dest.reshape(n, 1, 2 * rows)
    return pl.pallas_call(
        functools.partial(_combine_kernel, rows=rows),
        out_shape=jax.ShapeDtypeStruct((T, D), F32),
        grid=(n,),
        in_specs=[pl.BlockSpec((1, 1, 2 * rows), lambda i: (i, 0, 0), memory_space=pltpu.SMEM),
                  pl.BlockSpec((1, 1, 2 * rows), lambda i: (jnp.minimum(i + 1, n - 1), 0, 0),
                               memory_space=pltpu.SMEM),
                  pl.BlockSpec((rows, D), lambda i: (i, 0)),
                  pl.BlockSpec((rows, LANES), lambda i: (i, 0)),
                  pl.BlockSpec(memory_space=pl.ANY),
                  pl.BlockSpec((1, D), lambda i: (0, 0))],
        out_specs=pl.BlockSpec((rows, D), lambda i: (i, 0)),
        scratch_shapes=[pltpu.VMEM((2, 2, rows) + y3.shape[1:], y3.dtype), pltpu.SemaphoreType.DMA((2,))],
        compiler_params=_cparams("arbitrary"),
        name="moe_combine_norm",
    )(table, table, h, wts, y3, g.reshape(1, D))


def _dispatch_tables(ids, tm):
    T = ids.shape[0]
    E = N_EXPERTS
    eid = ids.reshape(-1)
    onehot = (eid[:, None] == jnp.arange(E, dtype=I32)[None, :]).astype(I32)
    csum = jnp.cumsum(onehot, axis=0)
    rank = jnp.sum(csum * onehot, axis=1) - 1
    counts = csum[-1]
    padded = ((counts + tm - 1) // tm) * tm
    ends = jnp.cumsum(padded)
    dest = jnp.sum(onehot * (ends - padded)[None, :], axis=1) + rank
    n_rows = 2 * T + E * tm
    tile_start = jnp.arange(n_rows // tm, dtype=I32) * tm
    tile_expert = jnp.sum((ends[None, :] <= tile_start[:, None]).astype(I32), axis=1)
    last_used = jnp.max(jnp.where(counts > 0, jnp.arange(E, dtype=I32), 0))
    tile_expert = jnp.minimum(tile_expert, last_used)
    tile_new = jnp.concatenate([jnp.ones((1,), I32), (tile_expert[1:] != tile_expert[:-1]).astype(I32)])
    n_tiles_used = (ends[-1] // tm).reshape(1)
    return dest.astype(I32), n_rows, (tile_expert, tile_new, n_tiles_used.astype(I32))


def _split_cast_kernel(w_ref, *o_refs, bounds):
    for o_ref, (lo, hi) in zip(o_refs, bounds):
        o_ref[...] = w_ref[:, lo:hi].astype(o_ref.dtype)


def split_cast(w, bounds, rows=WEIGHT_PREP_ROWS):
    K, N = w.shape
    rows = min(rows, K)
    return pl.pallas_call(
        functools.partial(_split_cast_kernel, bounds=bounds),
        out_shape=tuple(jax.ShapeDtypeStruct((K, hi - lo), BF16) for lo, hi in bounds),
        grid=(K // rows,),
        in_specs=[pl.BlockSpec((rows, N), lambda i: (i, 0))],
        out_specs=tuple(pl.BlockSpec((rows, hi - lo), lambda i: (i, 0)) for lo, hi in bounds),
        compiler_params=_cparams("parallel"),
        name="proj_weight_prep",
    )(w)


def _forward(x, norm_mix_g, w_in, cmp_k_pos, cmp_k_w1, cmp_k_b1, cmp_k_w2, cmp_v_pos, cmp_v_w1, cmp_v_b1, cmp_v_w2,
             gla_w_alpha, gla_b_alpha, gla_norm_g, w_branch_nsa, w_branch_gla, w_out, norm_ffn_g,
             w_router_group, b_router_group, w_router_expert, b_router_expert, w_exp_gate, w_exp_up, w_exp_down,
             norm_final_g):
    B, S, D = x.shape
    T = B * S
    G, R, Dh = NSA_KV_HEADS, NSA_GROUP, NSA_HEAD_DIM
    NSA_Q, NSA_KV = NSA_HEADS * Dh, G * Dh
    GQK, GV = GLA_HEADS * GLA_DK, GLA_HEADS * GLA_DV
    h = x.reshape(T, D)
    assert w_in.shape[0] == 1, "the final norm is fused into the (single) layer's combine step"
    for l in range(1):
        w = w_in[l]
        o_ng = NSA_Q + 6 * NSA_KV
        o_gq = o_ng + 3 * NSA_HEADS
        o_ga = o_gq + 2 * GQK + 2 * GV
        o_ma = o_ga + GLA_GATE_RANK
        n_small = 3 * NSA_HEADS + GLA_GATE_RANK
        w_small = jnp.concatenate([w[:, o_ng:o_gq], w[:, o_ga:o_ma], jnp.zeros((D, LANES - n_small), F32)],
                                  axis=1).astype(BF16)
        c_kc = NSA_Q // LANES
        c_ks, c_vs, c_kw, c_vw = c_kc + 2 * G, c_kc + 3 * G, c_kc + 4 * G, c_kc + 5 * G
        c_gk = GQK // LANES
        c_gv = c_gk + GQK // LANES
        c_gr = c_gv + GV // LANES

        xn = rmsnorm(h, norm_mix_g[l])
        w_nsa, w_gla, w_mix = split_cast(w, ((0, o_ng), (o_gq, o_ga), (o_ma, w.shape[1])))
        proj_nsa = matmul(xn, w_nsa, BF16, PROJ_TM, PROJ_TN, "proj_nsa").reshape(B, S, -1)
        proj_gla = matmul(xn, w_gla, BF16, PROJ_TM, PROJ_TN, "proj_gla").reshape(B, S, -1)
        proj_mix = matmul(xn, w_mix, BF16, PROJ_TM, PROJ_TN, "proj_mix")
        small = matmul(xn, w_small, F32, PROJ_TM, LANES, "proj_small")

        pos2 = jnp.stack([cmp_k_pos[l], cmp_v_pos[l]]).reshape(2, 2, CMP_STRIDE * Dh)
        w1 = jnp.stack([cmp_k_w1[l], cmp_v_w1[l]]).astype(BF16)
        b1 = jnp.stack([cmp_k_b1[l], cmp_v_b1[l]]).reshape(2, 1, Dh)
        w2 = jnp.stack([cmp_k_w2[l], cmp_v_w2[l]]).astype(BF16)
        kvc = compress(proj_nsa, c_kc, pos2, w1, b1, w2)
        gates = small[:, :3 * NSA_HEADS].reshape(B, S, G, 3 * R).transpose(0, 2, 1, 3)
        slopes = jnp.exp2(-8.0 * jnp.arange(1, NSA_HEADS + 1, dtype=F32) / NSA_HEADS)
        o_nsa = nsa_attention(proj_nsa, kvc, gates, slopes, col_q=0, col_ks=c_ks, col_vs=c_vs,
                              col_kw=c_kw, col_vw=c_vw)

        xa = small[:, 3 * NSA_HEADS:n_small].reshape(B, S, GLA_GATE_RANK)
        o_gla = gla_attention(proj_gla, xa, gla_w_alpha[l], gla_b_alpha[l], gla_norm_g[l],
                              col_q=0, col_k=c_gk, col_v=c_gv, col_r=c_gr)

        mixed = gated_mix(o_nsa.reshape(T, -1), o_gla.reshape(T, -1), w_branch_nsa[l].astype(BF16),
                          w_branch_gla[l].astype(BF16), proj_mix, 0, D // LANES)
        h = out_proj(mixed, w_out[l].astype(BF16), h)

        n_r = N_GROUPS + N_EXPERTS
        w_r = jnp.concatenate([w_router_group[l], w_router_expert[l], jnp.zeros((D, LANES - n_r), F32)], axis=1)
        b_r = jnp.concatenate([b_router_group[l], b_router_expert[l], jnp.zeros((LANES - n_r,), F32)]).reshape(1, LANES)
        ids, wts, u = router(h, norm_ffn_g[l], w_r, b_r)
        dest, n_rows, tables = _dispatch_tables(ids[:, :2], MOE_TM)
        x_sorted = dispatch_rows(u, dest, n_rows)
        h_sorted = ffn_up(x_sorted, tables, w_exp_gate[l], w_exp_up[l])
        y_sorted = ffn_down(h_sorted, tables, w_exp_down[l])
        h = combine_norm(h, wts, dest, y_sorted, norm_final_g)
    return h.reshape(B, S, D)


def kernel(x, norm_mix_g, w_in, cmp_k_pos, cmp_k_w1, cmp_k_b1, cmp_k_w2, cmp_v_pos, cmp_v_w1, cmp_v_b1, cmp_v_w2, gla_w_alpha, gla_b_alpha, gla_norm_g, w_branch_nsa, w_branch_gla, w_out, norm_ffn_g, w_router_group, b_router_group, w_router_expert, b_router_expert, w_exp_gate, w_exp_up, w_exp_down, norm_final_g):
    return _forward(x, norm_mix_g, w_in, cmp_k_pos, cmp_k_w1, cmp_k_b1, cmp_k_w2, cmp_v_pos, cmp_v_w1, cmp_v_b1,
                    cmp_v_w2, gla_w_alpha, gla_b_alpha, gla_norm_g, w_branch_nsa, w_branch_gla, w_out, norm_ffn_g,
                    w_router_group, b_router_group, w_router_expert, b_router_expert, w_exp_gate, w_exp_up,
                    w_exp_down, norm_final_g)
```
